```python
import math
import jax, jax.numpy as jnp
from jax import lax
import numpy as np

D_MODEL = 1024
BATCH = 8
SEQ = 8192
DEPTH = 4

N_MIXERS = 4
HEAD_DIM = 64
N_HEADS = D_MODEL // HEAD_DIM
ROPE_DIM = HEAD_DIM // 4
ROPE_THETA = 500000.0
Q_BLOCK = 128
SB_SUB = 64
CUMSUM_BLOCK = 128
DSA_KV_HEADS = 1
IDX_HEADS = 8
IDX_DIM = HEAD_DIM
IDX_TOPK_MAX = 256
GLA_HEADS = 4
GLA_DK = D_MODEL // 2 // GLA_HEADS
GLA_DV = D_MODEL // GLA_HEADS
GLA_GATE_RANK = 16
GLA_TAU = 16.0
GLA_CHUNK = 64
N_EXPERTS = 64
TOP_K = 8
N_GROUPS = 8
TOPK_GROUPS = 4
D_EXPERT = 256
D_SHARED = 256
ROUTED_SCALE = 2.5
EXPERT_BLOCK = 512
PLE_DIM = 256
DEEPNORM_ALPHA = (2 * DEPTH) ** 0.25
DEEPNORM_BETA = (8 * DEPTH) ** -0.25
LN_EPS = 1e-5
RMS_EPS = 1e-6

kernel_name = "hybrid_fox_sb_dsa_gla_moe_trunk"


def _split(t, sizes):
    return jnp.split(t, [int(v) for v in np.cumsum(sizes)[:-1]], axis=-1)


def layer_norm(x, g, b):
    xf = x.astype(jnp.float32)
    mu = jnp.mean(xf, axis=-1, keepdims=True)
    var = jnp.mean(jnp.square(xf - mu), axis=-1, keepdims=True)
    y = (xf - mu) * lax.rsqrt(var + LN_EPS) * g.astype(jnp.float32) + b.astype(jnp.float32)
    return y.astype(x.dtype)


def blocked_cumsum(t):
    B, S, H = t.shape
    tr = t.astype(jnp.float32).reshape(B, S // CUMSUM_BLOCK, CUMSUM_BLOCK, H)
    within = jnp.cumsum(tr, axis=2)
    tot = within[:, :, -1]
    prefix = jnp.cumsum(tot, axis=1) - tot
    return (within + prefix[:, :, None]).reshape(B, S, H)


def partial_rope(t, positions):
    half = ROPE_DIM // 2
    inv_freq = jnp.exp(-math.log(ROPE_THETA) * 2.0 * jnp.arange(half, dtype=jnp.float32) / ROPE_DIM)
    ang = positions.astype(jnp.float32)[:, :, None] * inv_freq
    cos = jnp.cos(ang)[:, :, None, :]
    sin = jnp.sin(ang)[:, :, None, :]
    tf = t.astype(jnp.float32)
    x1, x2 = tf[..., :half], tf[..., half:ROPE_DIM]
    out = jnp.concatenate([x1 * cos - x2 * sin, x2 * cos + x1 * sin, tf[..., ROPE_DIM:]], axis=-1)
    return out.astype(t.dtype)


def _causal_query_blocks(block_fn, S):
    outs = [block_fn(start, start + Q_BLOCK) for start in range(0, S, Q_BLOCK)]
    return jnp.concatenate(outs, axis=1)


def fox_mixer(x, w_in, b_f, w_out):
    B, S, D = x.shape
    H, dh = N_HEADS, HEAD_DIM
    q, k, v, f = _split(x @ w_in, [H * dh, H * dh, H * dh, H])
    q = q.reshape(B, S, H, dh)
    k = k.reshape(B, S, H, dh)
    v = v.reshape(B, S, H, dh)
    log_f = jax.nn.log_sigmoid((f + b_f).astype(jnp.float32))
    c = blocked_cumsum(log_f).transpose(0, 2, 1)
    scale = dh ** -0.5

    def block(start, L):
        qb = q[:, start:L]
        s = jnp.einsum("bqhd,bkhd->bhqk", qb, k[:, :L]).astype(jnp.float32) * scale
        s = s + (c[:, :, start:L, None] - c[:, :, None, :L])
        causal = jnp.arange(L)[None, :] <= (start + jnp.arange(Q_BLOCK))[:, None]
        s = jnp.where(causal, s, -jnp.inf)
        pr = jax.nn.softmax(s, axis=-1).astype(v.dtype)
        return jnp.einsum("bhqk,bkhd->bqhd", pr, v[:, :L]).reshape(B, Q_BLOCK, D)

    return _causal_query_blocks(block, S) @ w_out


def stick_breaking_mixer(x, w_in, w_out):
    B, S, D = x.shape
    H, dh = N_HEADS, HEAD_DIM
    q, k, v = _split(x @ w_in, [H * dh, H * dh, H * dh])
    q = q.reshape(B, S, H, dh)
    k = k.reshape(B, S, H, dh)
    v = v.reshape(B, S, H, dh)
    scale = dh ** -0.5
    tri_sub = (jnp.arange(SB_SUB)[:, None] > jnp.arange(SB_SUB)[None, :]).astype(jnp.float32)

    def block(start, L):
        nsb = L // SB_SUB
        z = jnp.einsum("bqhd,bkhd->bhqk", q[:, start:L], k[:, :L]).astype(jnp.float32) * scale
        strict = jnp.arange(L)[None, :] < (start + jnp.arange(Q_BLOCK))[:, None]
        u = jnp.where(strict, jax.nn.log_sigmoid(-z), 0.0)
        ur = u.reshape(B, H, Q_BLOCK, nsb, SB_SUB)
        rest_in = jnp.einsum("bhqnj,js->bhqns", ur, tri_sub)
        tot = jnp.sum(ur, axis=-1)
        tri_blk = (jnp.arange(nsb)[:, None] > jnp.arange(nsb)[None, :]).astype(jnp.float32)
        after = jnp.einsum("bhqm,mn->bhqn", tot, tri_blk)
        rest = (rest_in + after[..., None]).reshape(B, H, Q_BLOCK, L)
        a = jnp.where(strict, jnp.exp(jax.nn.log_sigmoid(z) + rest), 0.0).astype(v.dtype)
        return jnp.einsum("bhqk,bkhd->bqhd", a, v[:, :L]).reshape(B, Q_BLOCK, D)

    return _causal_query_blocks(block, S) @ w_out


def dsa_mixer(x, positions, w_in, w_out):
    B, S, D = x.shape
    H, G, dh = N_HEADS, DSA_KV_HEADS, HEAD_DIM
    HI, dI = IDX_HEADS, IDX_DIM
    q, k, v, qi, ki, wi = _split(x @ w_in, [H * dh, G * dh, G * dh, HI * dI, dI, HI])
    q = partial_rope(q.reshape(B, S, H, dh), positions)
    k = partial_rope(k.reshape(B, S, G, dh), positions)
    v = v.reshape(B, S, G, dh)
    qi = partial_rope(qi.reshape(B, S, HI, dI), positions)
    ki = partial_rope(ki.reshape(B, S, 1, dI), positions)[:, :, 0]
    wi = wi * (HI ** -0.5 * dI ** -0.5)
    topk = min(IDX_TOPK_MAX, S // 4)
    key_pos = jnp.arange(S)
    scale = dh ** -0.5

    def block(start):
        qpos = start + jnp.arange(Q_BLOCK)
        qib = lax.dynamic_slice_in_dim(qi, start, Q_BLOCK, axis=1)
        wib = lax.dynamic_slice_in_dim(wi, start, Q_BLOCK, axis=1)
        logits = jnp.einsum("bqhd,bkd->bqhk", qib, ki)
        score = jnp.einsum("bqhk,bqh->bqk", jax.nn.relu(logits), wib).astype(jnp.float32)
        causal = key_pos[None, :] <= qpos[:, None]
        score = jnp.where(causal[None], score, -jnp.inf)
        _, idx = lax.top_k(score, topk)
        valid = idx <= qpos[None, :, None]
        ksel = jax.vmap(lambda kb, ib: kb[ib])(k, idx)
        vsel = jax.vmap(lambda vb, ib: vb[ib])(v, idx)
        qb = lax.dynamic_slice_in_dim(q, start, Q_BLOCK, axis=1).reshape(B, Q_BLOCK, G, H // G, dh)
        s = jnp.einsum("bqgrd,bqkgd->bqgrk", qb, ksel).astype(jnp.float32) * scale
        s = jnp.where(valid[:, :, None, None, :], s, -jnp.inf)
        pr = jax.nn.softmax(s, axis=-1).astype(v.dtype)
        o = jnp.einsum("bqgrk,bqkgd->bqgrd", pr, vsel)
        return o.reshape(B, Q_BLOCK, D)

    starts = jnp.arange(S // Q_BLOCK, dtype=jnp.int32) * Q_BLOCK
    out = lax.map(block, starts)
    return jnp.swapaxes(out, 0, 1).reshape(B, S, D) @ w_out


def gla_mixer(x, w_in, w_a2, b_a, norm_g, w_out):
    B, S, D = x.shape
    H, dk, dv, C = GLA_HEADS, GLA_DK, GLA_DV, GLA_CHUNK
    N = S // C
    q, k, v, g, a1 = _split(x @ w_in, [H * dk, H * dk, H * dv, H * dv, GLA_GATE_RANK])
    log_a = jax.nn.log_sigmoid((a1 @ w_a2 + b_a).astype(jnp.float32)) / GLA_TAU

    def chunks(t, d):
        return t.astype(jnp.float32).reshape(B, N, C, H, d)

    q = chunks(q, dk) * (dk ** -0.5)
    k = chunks(k, dk)
    v = chunks(v, dv)
    b = jnp.cumsum(chunks(log_a, dk), axis=2)
    b_last = b[:, :, -1:]
    q_in = q * jnp.exp(b)
    k_in = k * jnp.exp(-b)
    k_out = k * jnp.exp(b_last - b)
    causal = jnp.tril(jnp.ones((C, C), dtype=bool))
    att = jnp.where(causal, jnp.einsum("bnqhd,bnkhd->bnhqk", q_in, k_in), 0.0)
    o_intra = jnp.einsum("bnhqk,bnkhv->bnqhv", att, v)

    def step(state, inp):
        qc, kc, vc, dec = inp
        o = jnp.einsum("bqhd,bhdv->bqhv", qc, state)
        state = state * dec[..., None] + jnp.einsum("bkhd,bkhv->bhdv", kc, vc)
        return state, o

    xs = (jnp.moveaxis(q_in, 1, 0), jnp.moveaxis(k_out, 1, 0), jnp.moveaxis(v, 1, 0),
          jnp.moveaxis(jnp.exp(b_last[:, :, 0]), 1, 0))
    _, o_inter = lax.scan(step, jnp.zeros((B, H, dk, dv), jnp.float32), xs)
    o = (o_intra + jnp.moveaxis(o_inter, 0, 1)).reshape(B, S, H, dv)
    o = o * lax.rsqrt(jnp.mean(jnp.square(o), axis=-1, keepdims=True) + RMS_EPS) * norm_g.astype(jnp.float32)
    o = o * jax.nn.silu(g.astype(jnp.float32).reshape(B, S, H, dv))
    return o.reshape(B, S, H * dv).astype(x.dtype) @ w_out


def moe_ffn(x, w_router, b_router, w_gu, w_down, ws_gu, ws_down):
    B, S, D = x.shape
    T = B * S
    E, M = N_EXPERTS, EXPERT_BLOCK
    xt = x.reshape(T, D)
    scores = jax.nn.sigmoid((xt @ w_router).astype(jnp.float32))
    sel = scores + b_router.astype(jnp.float32)
    grp_score = jnp.sum(lax.top_k(sel.reshape(T, N_GROUPS, E // N_GROUPS), 2)[0], axis=-1)
    _, top_groups = lax.top_k(grp_score, TOPK_GROUPS)
    group_mask = jnp.sum(jax.nn.one_hot(top_groups, N_GROUPS, dtype=jnp.float32), axis=-2) > 0
    sel = jnp.where(jnp.repeat(group_mask, E // N_GROUPS, axis=-1), sel, -jnp.inf)
    _, idx = lax.top_k(sel, TOP_K)
    w = jnp.take_along_axis(scores, idx, axis=-1)
    w = w / jnp.sum(w, axis=-1, keepdims=True) * ROUTED_SCALE

    TK = T * TOP_K
    e_flat = idx.reshape(TK)
    tok = jnp.arange(TK, dtype=jnp.int32) // TOP_K
    w_flat = w.reshape(TK)
    counts = jnp.zeros((E,), jnp.int32).at[e_flat].add(1)
    padded = (counts + M - 1) // M * M
    off = jnp.cumsum(counts) - counts
    pend = jnp.cumsum(padded)
    poff = pend - padded
    order = jnp.argsort(e_flat)
    e_s = e_flat[order]
    dest = poff[e_s] + (jnp.arange(TK, dtype=jnp.int32) - off[e_s])
    P = (TK + M - 1) // M * M + E * M
    nb = P // M
    row_tok = jnp.zeros((P,), jnp.int32).at[dest].set(tok[order])
    row_w = jnp.zeros((P,), jnp.float32).at[dest].set(w_flat[order])
    block_exp = jnp.clip(jnp.searchsorted(pend, jnp.arange(nb, dtype=jnp.int32) * M, side="right"), 0, E - 1)

    def block(args):
        e, toks, wts = args
        xb = xt[toks]
        gt, up = jnp.split(xb @ w_gu[e], 2, axis=-1)
        h = jax.nn.silu(gt) * up
        return (h @ w_down[e]) * wts[:, None].astype(xb.dtype)

    out = lax.map(block, (block_exp, row_tok.reshape(nb, M), row_w.reshape(nb, M))).reshape(P, D)
    routed = jnp.zeros((T, D), x.dtype).at[row_tok].add(out)
    gs, us = jnp.split(xt @ ws_gu, 2, axis=-1)
    shared = (jax.nn.silu(gs) * us) @ ws_down
    return (routed + shared).reshape(B, S, D)


def per_layer_embedding(x, p_i, w_proj, w_gate, b_gate):
    return jax.nn.sigmoid(x @ w_gate + b_gate) * (p_i @ w_proj)


def setup_inputs(seed: int = 0) -> dict:
    key = jax.random.key(seed)
    ks = iter(jax.random.split(key, 32))

    def nrm(shape, scale):
        return jax.random.normal(next(ks), shape, jnp.float32) * scale

    n_a, n_b, n_c, n_d = [(DEPTH - m + N_MIXERS - 1) // N_MIXERS for m in range(N_MIXERS)]
    D = D_MODEL
    out_scale = D ** -0.5 * DEEPNORM_BETA
    fox_cols = 3 * N_HEADS * HEAD_DIM + N_HEADS
    sb_cols = 3 * N_HEADS * HEAD_DIM
    dsa_cols = N_HEADS * HEAD_DIM + 2 * DSA_KV_HEADS * HEAD_DIM + IDX_HEADS * IDX_DIM + IDX_DIM + IDX_HEADS
    gla_cols = 2 * GLA_HEADS * GLA_DK + 2 * GLA_HEADS * GLA_DV + GLA_GATE_RANK
    offsets = jax.random.randint(next(ks), (BATCH, 1), 0, 1024, dtype=jnp.int32)
    positions = offsets + jnp.arange(SEQ, dtype=jnp.int32)[None, :]
    return {
        "x": nrm((BATCH, SEQ, D), 1.0),
        "p": nrm((DEPTH, BATCH, SEQ, PLE_DIM), 1.0),
        "positions": positions,
        "ln1_g": 1.0 + nrm((DEPTH, D), 0.02),
        "ln1_b": nrm((DEPTH, D), 0.02),
        "ln2_g": 1.0 + nrm((DEPTH, D), 0.02),
        "ln2_b": nrm((DEPTH, D), 0.02),
        "fox_w_in": nrm((n_a, D, fox_cols), D ** -0.5),
        "fox_b_f": nrm((n_a, N_HEADS), 0.02),
        "fox_w_out": nrm((n_a, D, D), out_scale),
        "sb_w_in": nrm((n_b, D, sb_cols), D ** -0.5),
        "sb_w_out": nrm((n_b, D, D), out_scale),
        "dsa_w_in": nrm((n_c, D, dsa_cols), D ** -0.5),
        "dsa_w_out": nrm((n_c, D, D), out_scale),
        "gla_w_in": nrm((n_d, D, gla_cols), D ** -0.5),
        "gla_w_a2": nrm((n_d, GLA_GATE_RANK, GLA_HEADS * GLA_DK), GLA_GATE_RANK ** -0.5),
        "gla_b_a": nrm((n_d, GLA_HEADS * GLA_DK), 0.02),
        "gla_norm_g": 1.0 + nrm((n_d, GLA_DV), 0.02),
        "gla_w_out": nrm((n_d, GLA_HEADS * GLA_DV, D), (GLA_HEADS * GLA_DV) ** -0.5 * DEEPNORM_BETA),
        "moe_w_router": nrm((DEPTH, D, N_EXPERTS), D ** -0.5),
        "moe_b_router": nrm((DEPTH, N_EXPERTS), 0.01),
        "moe_w_gu": nrm((DEPTH, N_EXPERTS, D, 2 * D_EXPERT), D ** -0.5),
        "moe_w_down": nrm((DEPTH, N_EXPERTS, D_EXPERT, D), D_EXPERT ** -0.5 * DEEPNORM_BETA),
        "shared_w_gu": nrm((DEPTH, D, 2 * D_SHARED), D ** -0.5),
        "shared_w_down": nrm((DEPTH, D_SHARED, D), D_SHARED ** -0.5 * DEEPNORM_BETA),
        "ple_w_proj": nrm((DEPTH, PLE_DIM, D), PLE_DIM ** -0.5),
        "ple_w_gate": nrm((DEPTH, D, D), D ** -0.5),
        "ple_b_gate": nrm((DEPTH, D), 0.02),
    }


def reference(x, p, positions, ln1_g, ln1_b, ln2_g, ln2_b,
              fox_w_in, fox_b_f, fox_w_out, sb_w_in, sb_w_out, dsa_w_in, dsa_w_out,
              gla_w_in, gla_w_a2, gla_b_a, gla_norm_g, gla_w_out,
              moe_w_router, moe_b_router, moe_w_gu, moe_w_down, shared_w_gu, shared_w_down,
              ple_w_proj, ple_w_gate, ple_b_gate):
    for i in range(DEPTH):
        m, j = i % N_MIXERS, i // N_MIXERS
        if m == 0:
            h = fox_mixer(x, fox_w_in[j], fox_b_f[j], fox_w_out[j])
        elif m == 1:
            h = stick_breaking_mixer(x, sb_w_in[j], sb_w_out[j])
        elif m == 2:
            h = dsa_mixer(x, positions, dsa_w_in[j], dsa_w_out[j])
        else:
            h = gla_mixer(x, gla_w_in[j], gla_w_a2[j], gla_b_a[j], gla_norm_g[j], gla_w_out[j])
        x = layer_norm(DEEPNORM_ALPHA * x + h, ln1_g[i], ln1_b[i])
        f = moe_ffn(x, moe_w_router[i], moe_b_router[i], moe_w_gu[i], moe_w_down[i],
                    shared_w_gu[i], shared_w_down[i])
        x = layer_norm(DEEPNORM_ALPHA * x + f, ln2_g[i], ln2_b[i])
        x = x + per_layer_embedding(x, p[i], ple_w_proj[i], ple_w_gate[i], ple_b_gate[i])
    return x
```

```python
import functools
import math

import jax
import jax.numpy as jnp
from jax import lax
from jax.experimental import pallas as pl
from jax.experimental.pallas import tpu as pltpu

F32 = jnp.float32
BF16 = jnp.bfloat16
I32 = jnp.int32

D_MODEL = 1024
HEAD_DIM = 64
N_HEADS = D_MODEL // HEAD_DIM
N_PAIRS = N_HEADS // 2
LANES = 128
ROPE_DIM = HEAD_DIM // 4
ROPE_THETA = 500000.0
CUMSUM_BLOCK = 128
IDX_HEADS = 8
IDX_DIM = HEAD_DIM
IDX_TOPK_MAX = 256
GLA_HEADS = 4
GLA_DK = D_MODEL // 2 // GLA_HEADS
GLA_DV = D_MODEL // GLA_HEADS
GLA_GATE_RANK = 16
GLA_TAU = 16.0
GLA_CHUNK = 64
N_EXPERTS = 64
TOP_K = 8
N_GROUPS = 8
GROUP_SIZE = N_EXPERTS // N_GROUPS
TOPK_GROUPS = 4
D_EXPERT = 256
D_SHARED = 256
ROUTED_SCALE = 2.5
EXPERT_BLOCK = 512
PLE_DIM = 256
DEPTH = 4
DEEPNORM_ALPHA = (2 * DEPTH) ** 0.25
LN_EPS = 1e-5
RMS_EPS = 1e-6
INT_MIN = -(2 ** 31)
VMEM_LIMIT = 56 * 1024 * 1024


def _cparams(sem):
    return pltpu.CompilerParams(dimension_semantics=sem, vmem_limit_bytes=VMEM_LIMIT)


def _pick(n, cands):
    for c in cands:
        if n % c == 0:
            return c
    raise ValueError(f"no tile for {n}")


def _dot(a, b):
    return jnp.dot(a, b, preferred_element_type=F32)


def _dot_nt(a, b):
    return lax.dot_general(a, b, (((1,), (1,)), ((), ())), preferred_element_type=F32)


def _dot_tn(a, b):
    return lax.dot_general(a, b, (((0,), (0,)), ((), ())), preferred_element_type=F32)


def _mm_body(x_ref, w_ref, o_ref):
    o_ref[...] = _dot(x_ref[...].astype(BF16), w_ref[...].astype(BF16)).astype(o_ref.dtype)


def matmul(x, w, out_dtype, tm=512):
    M, K = x.shape
    N = w.shape[1]
    tm = min(tm, M)
    tn = _pick(N, (512, 384, 256, 128))
    return pl.pallas_call(
        _mm_body,
        grid=(M // tm, N // tn),
        in_specs=[pl.BlockSpec((tm, K), lambda i, j: (i, 0)),
                  pl.BlockSpec((K, tn), lambda i, j: (0, j))],
        out_specs=pl.BlockSpec((tm, tn), lambda i, j: (i, j)),
        out_shape=jax.ShapeDtypeStruct((M, N), out_dtype),
        compiler_params=_cparams(("parallel", "parallel")),
        name="matmul",
    )(x, w)


def _mm_rope_body(x_ref, w_ref, cos_ref, sa_ref, sb_ref, o_ref):
    y = _dot(x_ref[...].astype(BF16), w_ref[...].astype(BF16))
    cos, sa, sb = cos_ref[...], sa_ref[...], sb_ref[...]
    for c in range(y.shape[1] // LANES):
        yc = y[:, c * LANES:(c + 1) * LANES]
        oc = yc * cos + pltpu.roll(yc, LANES - ROPE_DIM // 2, 1) * sa + pltpu.roll(yc, ROPE_DIM // 2, 1) * sb
        o_ref[:, c * LANES:(c + 1) * LANES] = oc.astype(o_ref.dtype)


def matmul_rope(x, w, cos, sa, sb, out_dtype, tm=512):
    M, K = x.shape
    N = w.shape[1]
    tm = min(tm, M)
    tn = _pick(N, (256, 128))
    tab = pl.BlockSpec((tm, LANES), lambda i, j: (i, 0))
    return pl.pallas_call(
        _mm_rope_body,
        grid=(M // tm, N // tn),
        in_specs=[pl.BlockSpec((tm, K), lambda i, j: (i, 0)),
                  pl.BlockSpec((K, tn), lambda i, j: (0, j)), tab, tab, tab],
        out_specs=pl.BlockSpec((tm, tn), lambda i, j: (i, j)),
        out_shape=jax.ShapeDtypeStruct((M, N), out_dtype),
        compiler_params=_cparams(("parallel", "parallel")),
        name="matmul_rope",
    )(x, w, cos, sa, sb)


def _layer_norm(y, g, b):
    mu = jnp.mean(y, axis=-1, keepdims=True)
    d = y - mu
    var = jnp.mean(d * d, axis=-1, keepdims=True)
    return d * lax.rsqrt(var + LN_EPS) * g + b


def _mm_res_ln_body(a_ref, w_ref, x_ref, g_ref, b_ref, o_ref):
    h = _dot(a_ref[...].astype(BF16), w_ref[...].astype(BF16))
    o_ref[...] = _layer_norm(DEEPNORM_ALPHA * x_ref[...] + h, g_ref[...], b_ref[...])


def matmul_residual_ln(a, w, x, g, b, tm=512):
    M, K = a.shape
    N = w.shape[1]
    tm = min(tm, M)
    return pl.pallas_call(
        _mm_res_ln_body,
        grid=(M // tm,),
        in_specs=[pl.BlockSpec((tm, K), lambda i: (i, 0)),
                  pl.BlockSpec((K, N), lambda i: (0, 0)),
                  pl.BlockSpec((tm, N), lambda i: (i, 0)),
                  pl.BlockSpec((1, N), lambda i: (0, 0)),
                  pl.BlockSpec((1, N), lambda i: (0, 0))],
        out_specs=pl.BlockSpec((tm, N), lambda i: (i, 0)),
        out_shape=jax.ShapeDtypeStruct((M, N), F32),
        compiler_params=_cparams(("parallel",)),
        name="matmul_residual_ln",
    )(a, w, x, g.reshape(1, N), b.reshape(1, N))


def _pair_masks(q2):
    lane = lax.broadcasted_iota(I32, q2.shape, 1)
    lo = lane < HEAD_DIM
    zero = jnp.zeros_like(q2)
    return lo, (jnp.where(lo, q2, zero), jnp.where(lo, zero, q2))


def _fox_body(q_ref, k_ref, v_ref, c_ref, o_ref, m_sc, l_sc, acc_sc, *, t, scale):
    qi = pl.program_id(2)
    lo, qs = _pair_masks(q_ref[0])
    m_sc[...] = jnp.full(m_sc.shape, -jnp.inf, F32)
    l_sc[...] = jnp.zeros(l_sc.shape, F32)
    acc_sc[...] = jnp.zeros(acc_sc.shape, F32)
    cref = [c_ref[0, 0, h, pl.ds(qi, 1), :][:, 0:1] for h in range(2)]

    def block(kb, masked):
        start = pl.multiple_of(kb * t, t)
        k2 = k_ref[0, pl.ds(start, t), :]
        v2 = v_ref[0, pl.ds(start, t), :]
        for h in range(2):
            s = _dot_nt(qs[h], k2) * scale + (cref[h] - c_ref[0, 0, h, pl.ds(kb, 1), :])
            if masked:
                row = lax.broadcasted_iota(I32, (t, t), 0)
                col = lax.broadcasted_iota(I32, (t, t), 1)
                s = jnp.where(col <= row, s, -jnp.inf)
            m_old = m_sc[h]
            m_new = jnp.maximum(m_old, jnp.max(s, axis=-1, keepdims=True))
            alpha = jnp.exp(m_old - m_new)
            p = jnp.exp(s - m_new)
            l_sc[h] = alpha * l_sc[h] + jnp.sum(p, axis=-1, keepdims=True)
            acc_sc[h] = alpha * acc_sc[h] + _dot(p.astype(BF16), v2)
            m_sc[h] = m_new

    def full_block(kb, carry):
        block(kb, False)
        return carry

    lax.fori_loop(0, qi, full_block, 0)
    block(qi, True)
    o = jnp.where(lo, acc_sc[0] / l_sc[0], acc_sc[1] / l_sc[1])
    o_ref[0] = o.astype(o_ref.dtype)


def fox_attention(qkv, c5, t):
    B, S, _ = qkv.shape
    nq = S // t
    kern = functools.partial(_fox_body, t=t, scale=HEAD_DIM ** -0.5)
    return pl.pallas_call(
        kern,
        grid=(B, N_PAIRS, nq),
        in_specs=[pl.BlockSpec((1, t, LANES), lambda b, p, i: (b, i, p)),
                  pl.BlockSpec((1, S, LANES), lambda b, p, i: (b, 0, N_PAIRS + p)),
                  pl.BlockSpec((1, S, LANES), lambda b, p, i: (b, 0, 2 * N_PAIRS + p)),
                  pl.BlockSpec((1, 1, 2, nq, t), lambda b, p, i: (b, p, 0, 0, 0))],
        out_specs=pl.BlockSpec((1, t, LANES), lambda b, p, i: (b, i, p)),
        out_shape=jax.ShapeDtypeStruct((B, S, D_MODEL), BF16),
        scratch_shapes=[pltpu.VMEM((2, t, 1), F32), pltpu.VMEM((2, t, 1), F32),
                        pltpu.VMEM((2, t, LANES), F32)],
        compiler_params=_cparams(("parallel", "parallel", "arbitrary")),
        name="fox_attention",
    )(qkv, qkv, qkv, c5)


def _sb_body(q_ref, k_ref, v_ref, tri_ref, o_ref, after_sc, acc_sc, *, t, scale):
    qi = pl.program_id(2)
    lo, qs = _pair_masks(q_ref[0])
    after_sc[...] = jnp.zeros(after_sc.shape, F32)
    acc_sc[...] = jnp.zeros(acc_sc.shape, F32)
    tri = tri_ref[...]

    def block(kb, masked):
        start = pl.multiple_of(kb * t, t)
        k2 = k_ref[0, pl.ds(start, t), :]
        v2 = v_ref[0, pl.ds(start, t), :]
        for h in range(2):
            z = _dot_nt(qs[h], k2) * scale
            u = -(jnp.maximum(z, 0.0) + jnp.log1p(jnp.exp(-jnp.abs(z))))
            if masked:
                row = lax.broadcasted_iota(I32, (t, t), 0)
                col = lax.broadcasted_iota(I32, (t, t), 1)
                strict = col < row
                log_beta = z + u
                u = jnp.where(strict, u, 0.0)
            else:
                log_beta = z + u
            rest = _dot(u.astype(BF16), tri) + after_sc[h]
            a = jnp.exp(log_beta + rest)
            if masked:
                a = jnp.where(strict, a, 0.0)
            acc_sc[h] = acc_sc[h] + _dot(a.astype(BF16), v2)
            after_sc[h] = after_sc[h] + jnp.sum(u, axis=-1, keepdims=True)

    block(qi, True)

    def full_block(j, carry):
        block(qi - 1 - j, False)
        return carry

    lax.fori_loop(0, qi, full_block, 0)
    o_ref[0] = jnp.where(lo, acc_sc[0], acc_sc[1]).astype(o_ref.dtype)


def sb_attention(qkv, t):
    B, S, _ = qkv.shape
    nq = S // t
    idx = jnp.arange(t)
    tri = (idx[:, None] > idx[None, :]).astype(BF16)
    kern = functools.partial(_sb_body, t=t, scale=HEAD_DIM ** -0.5)
    return pl.pallas_call(
        kern,
        grid=(B, N_PAIRS, nq),
        in_specs=[pl.BlockSpec((1, t, LANES), lambda b, p, i: (b, i, p)),
                  pl.BlockSpec((1, S, LANES), lambda b, p, i: (b, 0, N_PAIRS + p)),
                  pl.BlockSpec((1, S, LANES), lambda b, p, i: (b, 0, 2 * N_PAIRS + p)),
                  pl.BlockSpec((t, t), lambda b, p, i: (0, 0))],
        out_specs=pl.BlockSpec((1, t, LANES), lambda b, p, i: (b, i, p)),
        out_shape=jax.ShapeDtypeStruct((B, S, D_MODEL), BF16),
        scratch_shapes=[pltpu.VMEM((2, t, 1), F32), pltpu.VMEM((2, t, LANES), F32)],
        compiler_params=_cparams(("parallel", "parallel", "arbitrary")),
        name="sb_attention",
    )(qkv, qkv, qkv, tri)


def _sortable_key(x):
    bits = pltpu.bitcast(x, I32)
    return bits ^ ((bits >> 31) & jnp.int32(0x7FFFFFFF))


def _dsa_body(q_ref, qi_ref, k_ref, ki_ref, v_ref, wi_ref, upper_ref, o_ref,
              keys_sc, thr_sc, need_sc, flag_sc, m_sc, l_sc, acc_sc, *, tq, tk, topk, scale):
    qb = pl.program_id(1)
    p = pl.program_id(2)
    nch = (qb * tq) // tk + 1
    qpos = qb * tq + lax.broadcasted_iota(I32, (tq, tk), 0)
    col = lax.broadcasted_iota(I32, (tq, tk), 1)

    @pl.when(p == 0)
    def _select():
        wi = wi_ref[0]

        def score_chunk(c, carry):
            start = pl.multiple_of(c * tk, tk)
            kidup = ki_ref[0, pl.ds(start, tk), :]
            score = jnp.zeros((tq, tk), F32)
            for hp in range(IDX_HEADS // 2):
                _, halves = _pair_masks(qi_ref[0, :, hp * LANES:(hp + 1) * LANES])
                for j in range(2):
                    h = 2 * hp + j
                    logits = _dot_nt(halves[j], kidup)
                    score = score + jnp.maximum(logits, 0.0) * wi[:, h:h + 1]
            key = jnp.where(c * tk + col <= qpos, _sortable_key(score), jnp.int32(INT_MIN))
            keys_sc[c] = key
            return carry

        lax.fori_loop(0, nch, score_chunk, 0)

        def count_ge(thr):
            def body(c, cnt):
                kc = keys_sc[c]
                for j in range(tk // LANES):
                    cnt = cnt + jnp.where(kc[:, j * LANES:(j + 1) * LANES] >= thr, 1, 0)
                return cnt
            cnt = lax.fori_loop(0, nch, body, jnp.zeros((tq, LANES), I32))
            return jnp.sum(cnt, axis=-1, keepdims=True)

        def bit_step(i, thr):
            cand = thr + jnp.left_shift(jnp.int32(1), 31 - i)
            return jnp.where(count_ge(cand) >= topk, cand, thr)

        thr = lax.fori_loop(0, 32, bit_step, jnp.full((tq, 1), INT_MIN, I32))
        n_ge = count_ge(thr)
        n_gt = count_ge(thr + 1)
        need = topk - n_gt
        thr_sc[...] = thr
        need_sc[...] = need
        tie = jnp.logical_and(n_ge - n_gt > need, thr > INT_MIN)
        flag_sc[0] = jnp.max(jnp.where(tie, 1, 0))

    lo, qs = _pair_masks(q_ref[0])
    m_sc[...] = jnp.full(m_sc.shape, -1e30, F32)
    l_sc[...] = jnp.zeros(l_sc.shape, F32)
    acc_sc[...] = jnp.zeros(acc_sc.shape, F32)
    thr = thr_sc[...]

    def attend(c, sel):
        start = pl.multiple_of(c * tk, tk)
        kdup = k_ref[0, pl.ds(start, tk), :]
        vdup = v_ref[0, pl.ds(start, tk), :]
        for h in range(2):
            s = jnp.where(sel, _dot_nt(qs[h], kdup) * scale, -1e30)
            m_old = m_sc[h]
            m_new = jnp.maximum(m_old, jnp.max(s, axis=-1, keepdims=True))
            alpha = jnp.exp(m_old - m_new)
            pr = jnp.exp(s - m_new)
            l_sc[h] = alpha * l_sc[h] + jnp.sum(pr, axis=-1, keepdims=True)
            acc_sc[h] = alpha * acc_sc[h] + _dot(pr.astype(BF16), vdup)
            m_sc[h] = m_new

    @pl.when(flag_sc[0] == 0)
    def _no_ties():
        def body(c, carry):
            kc = keys_sc[c]
            attend(c, jnp.logical_and(kc >= thr, kc > INT_MIN))
            return carry
        lax.fori_loop(0, nch, body, 0)

    @pl.when(flag_sc[0] != 0)
    def _ties():
        need = need_sc[...].astype(F32)

        def body(c, seen):
            kc = keys_sc[c]
            eq = kc == thr
            eqf = jnp.where(eq, 1.0, 0.0)
            rank = _dot(eqf.astype(BF16), upper_ref[...]) + seen
            sel = jnp.logical_or(kc > thr, jnp.logical_and(eq, rank < need))
            attend(c, jnp.logical_and(sel, kc > INT_MIN))
            return seen + jnp.sum(eqf, axis=-1, keepdims=True)
        lax.fori_loop(0, nch, body, jnp.zeros((tq, 1), F32))

    o = jnp.where(lo, acc_sc[0] / l_sc[0], acc_sc[1] / l_sc[1])
    o_ref[0] = o.astype(o_ref.dtype)


def dsa_attention(proj, vdup, wi, tq, tk):
    B, S, _ = proj.shape
    topk = min(IDX_TOPK_MAX, S // 4)
    idx = jnp.arange(tk)
    upper = (idx[:, None] < idx[None, :]).astype(BF16)
    kern = functools.partial(_dsa_body, tq=tq, tk=tk, topk=topk, scale=HEAD_DIM ** -0.5)
    qi_blk = D_MODEL // (IDX_HEADS * IDX_DIM)
    k_blk = (D_MODEL + IDX_HEADS * IDX_DIM) // LANES
    return pl.pallas_call(
        kern,
        grid=(B, S // tq, N_PAIRS),
        in_specs=[pl.BlockSpec((1, tq, LANES), lambda b, i, p: (b, i, p)),
                  pl.BlockSpec((1, tq, IDX_HEADS * IDX_DIM), lambda b, i, p: (b, i, qi_blk)),
                  pl.BlockSpec((1, S, LANES), lambda b, i, p: (b, 0, k_blk)),
                  pl.BlockSpec((1, S, LANES), lambda b, i, p: (b, 0, k_blk + 1)),
                  pl.BlockSpec((1, S, LANES), lambda b, i, p: (b, 0, 0)),
                  pl.BlockSpec((1, tq, LANES), lambda b, i, p: (b, i, 0)),
                  pl.BlockSpec((tk, tk), lambda b, i, p: (0, 0))],
        out_specs=pl.BlockSpec((1, tq, LANES), lambda b, i, p: (b, i, p)),
        out_shape=jax.ShapeDtypeStruct((B, S, D_MODEL), BF16),
        scratch_shapes=[pltpu.VMEM((S // tk, tq, tk), I32),
                        pltpu.VMEM((tq, 1), I32), pltpu.VMEM((tq, 1), I32),
                        pltpu.SMEM((1,), I32),
                        pltpu.VMEM((2, tq, 1), F32), pltpu.VMEM((2, tq, 1), F32),
                        pltpu.VMEM((2, tq, LANES), F32)],
        compiler_params=_cparams(("parallel", "arbitrary", "arbitrary")),
        name="dsa_attention",
    )(proj, proj, proj, proj, vdup, wi, upper)


def _split3(x):
    h1 = x.astype(BF16)
    r1 = x - h1.astype(F32)
    h2 = r1.astype(BF16)
    h3 = (r1 - h2.astype(F32)).astype(BF16)
    return h1, h2, h3


def _gla_body(qkvg_ref, a1_ref, wa2_ref, ba_ref, ng_ref, o_ref, state_sc, *, ts):
    C, H, dk, dv = GLA_CHUNK, GLA_HEADS, GLA_DK, GLA_DV

    @pl.when(pl.program_id(1) == 0)
    def _reset():
        state_sc[...] = jnp.zeros(state_sc.shape, F32)

    row = lax.broadcasted_iota(I32, (C, C), 0)
    colc = lax.broadcasted_iota(I32, (C, C), 1)
    causal = colc <= row
    tril = jnp.where(causal, 1.0, 0.0).astype(BF16)
    wa2 = wa2_ref[...].astype(BF16)
    ng = ng_ref[...]
    v_off, g_off = 2 * H * dk, 2 * H * dk + H * dv
    for n in range(ts // C):
        rows = slice(n * C, (n + 1) * C)
        za = _dot(a1_ref[0, rows, :].astype(BF16), wa2) + ba_ref[...]
        log_a = (jnp.minimum(za, 0.0) - jnp.log1p(jnp.exp(-jnp.abs(za)))) / GLA_TAU
        h1, h2, h3 = _split3(log_a)
        bcum = _dot(tril, h1) + _dot(tril, h2) + _dot(tril, h3)
        eb = jnp.exp(bcum)
        enb = jnp.exp(-bcum)
        b_last = bcum[C - 1:C, :]
        eout = jnp.exp(b_last - bcum)
        dec = jnp.exp(b_last)
        for h in range(H):
            ks = slice(h * dk, (h + 1) * dk)
            q = qkvg_ref[0, rows, h * dk:(h + 1) * dk] * (dk ** -0.5)
            k = qkvg_ref[0, rows, H * dk + h * dk:H * dk + (h + 1) * dk]
            v = qkvg_ref[0, rows, v_off + h * dv:v_off + (h + 1) * dv].astype(BF16)
            q_in = (q * eb[:, ks]).astype(BF16)
            k_in = (k * enb[:, ks]).astype(BF16)
            k_out = (k * eout[:, ks]).astype(BF16)
            att = jnp.where(causal, _dot_nt(q_in, k_in), 0.0)
            state = state_sc[h]
            o = _dot(att.astype(BF16), v) + _dot(q_in, state.astype(BF16))
            dec_col = jnp.transpose(jnp.broadcast_to(dec[:, ks], (dk, dk)))[:, 0:1]
            state_sc[h] = state * dec_col + _dot_tn(k_out, v)
            o = o * lax.rsqrt(jnp.mean(o * o, axis=-1, keepdims=True) + RMS_EPS) * ng
            g = qkvg_ref[0, rows, g_off + h * dv:g_off + (h + 1) * dv]
            o = o * (g * jax.nn.sigmoid(g))
            o_ref[0, rows, h * dv:(h + 1) * dv] = o.astype(o_ref.dtype)


def gla_attention(qkvg, a1, w_a2p, b_a, norm_g, ts=512):
    B, S, W = qkvg.shape
    ts = min(ts, S)
    HK = GLA_HEADS * GLA_DK
    kern = functools.partial(_gla_body, ts=ts)
    return pl.pallas_call(
        kern,
        grid=(B, S // ts),
        in_specs=[pl.BlockSpec((1, ts, W), lambda b, i: (b, i, 0)),
                  pl.BlockSpec((1, ts, LANES), lambda b, i: (b, i, 0)),
                  pl.BlockSpec((LANES, HK), lambda b, i: (0, 0)),
                  pl.BlockSpec((1, HK), lambda b, i: (0, 0)),
                  pl.BlockSpec((1, GLA_DV), lambda b, i: (0, 0))],
        out_specs=pl.BlockSpec((1, ts, GLA_HEADS * GLA_DV), lambda b, i: (b, i, 0)),
        out_shape=jax.ShapeDtypeStruct((B, S, GLA_HEADS * GLA_DV), BF16),
        scratch_shapes=[pltpu.VMEM((GLA_HEADS, GLA_DK, GLA_DV), F32)],
        compiler_params=_cparams(("parallel", "arbitrary")),
        name="gla_attention",
    )(qkvg, a1, w_a2p, b_a.reshape(1, HK), norm_g.reshape(1, GLA_DV))


def _first_max(vals, ids, big):
    m = jnp.max(vals, axis=0, keepdims=True)
    first = jnp.min(jnp.where(vals == m, ids, big), axis=0, keepdims=True)
    return m, first


def _router_body(x_ref, wr_ref, br_ref, lower_ref, idx_ref, w_ref, rank_ref, cnt_ref, run_sc, *, tm):
    E, G, GS = N_EXPERTS, N_GROUPS, GROUP_SIZE

    @pl.when(pl.program_id(0) == 0)
    def _reset():
        run_sc[...] = jnp.zeros(run_sc.shape, F32)

    x = x_ref[...]
    x1 = x.astype(BF16)
    x2 = (x - x1.astype(F32)).astype(BF16)
    wr = wr_ref[...]
    w1 = wr.astype(BF16)
    w2 = (wr - w1.astype(F32)).astype(BF16)
    logits = _dot_nt(w1, x1) + _dot_nt(w1, x2) + _dot_nt(w2, x1)
    scores = jax.nn.sigmoid(logits)
    sel = scores + br_ref[...]
    neg = -jnp.inf
    eid = lax.broadcasted_iota(I32, (E, tm), 0)
    lid = lax.broadcasted_iota(I32, (GS, tm), 0)

    grp = []
    for g in range(G):
        sg = sel[g * GS:(g + 1) * GS, :]
        m1, f1 = _first_max(sg, lid, GS)
        m2 = jnp.max(jnp.where(lid == f1, neg, sg), axis=0, keepdims=True)
        grp.append(m1 + m2)
    grp = jnp.concatenate(grp, axis=0)
    gmask = jnp.zeros((G, tm), jnp.bool_)
    for _ in range(TOPK_GROUPS):
        _, f = _first_max(grp, lid, G)
        hit = lid == f
        gmask = jnp.logical_or(gmask, hit)
        grp = jnp.where(hit, neg, grp)
    emask = jnp.concatenate(
        [jnp.broadcast_to(gmask[g:g + 1, :], (GS, tm)) for g in range(G)], axis=0)
    sel = jnp.where(emask, sel, neg)

    chosen = jnp.zeros((E, tm), jnp.bool_)
    ids, ws = [], []
    for _ in range(TOP_K):
        _, f = _first_max(sel, eid, E)
        hit = eid == f
        ids.append(f)
        ws.append(jnp.sum(jnp.where(hit, scores, 0.0), axis=0, keepdims=True))
        chosen = jnp.logical_or(chosen, hit)
        sel = jnp.where(hit, neg, sel)
    ids = jnp.concatenate(ids, axis=0)
    ws = jnp.concatenate(ws, axis=0)
    ws = ws / jnp.sum(ws, axis=0, keepdims=True) * ROUTED_SCALE

    cf = jnp.where(chosen, 1.0, 0.0)
    before = _dot(cf.astype(BF16), lower_ref[...]) + run_sc[...]
    ranks = [jnp.sum(jnp.where(eid == ids[k:k + 1, :], before, 0.0), axis=0, keepdims=True)
             for k in range(TOP_K)]
    run_sc[...] = run_sc[...] + jnp.sum(cf, axis=1, keepdims=True)
    idx_ref[...] = ids
    w_ref[...] = ws
    rank_ref[...] = jnp.concatenate(ranks, axis=0).astype(I32)
    cnt_ref[...] = jnp.broadcast_to(run_sc[...], cnt_ref.shape).astype(I32)


def moe_router(x, w_router, b_router, tm=512):
    T, D = x.shape
    tm = min(tm, T)
    ar = jnp.arange(tm)
    lower = (ar[:, None] < ar[None, :]).astype(BF16)
    out = pl.BlockSpec((TOP_K, tm), lambda i: (0, i))
    return pl.pallas_call(
        functools.partial(_router_body, tm=tm),
        grid=(T // tm,),
        in_specs=[pl.BlockSpec((tm, D), lambda i: (i, 0)),
                  pl.BlockSpec((N_EXPERTS, D), lambda i: (0, 0)),
                  pl.BlockSpec((N_EXPERTS, 1), lambda i: (0, 0)),
                  pl.BlockSpec((tm, tm), lambda i: (0, 0))],
        out_specs=[out, out, out, pl.BlockSpec((N_EXPERTS, LANES), lambda i: (0, 0))],
        out_shape=[jax.ShapeDtypeStruct((TOP_K, T), I32), jax.ShapeDtypeStruct((TOP_K, T), F32),
                   jax.ShapeDtypeStruct((TOP_K, T), I32),
                   jax.ShapeDtypeStruct((N_EXPERTS, LANES), I32)],
        scratch_shapes=[pltpu.VMEM((N_EXPERTS, 1), F32)],
        compiler_params=_cparams(("arbitrary",)),
        name="moe_router",
    )(x, w_router.T, b_router.reshape(N_EXPERTS, 1), lower)


def _expert_body(be_ref, nreal_ref, x_ref, wgu_ref, wd_ref, o_ref):
    @pl.when(pl.program_id(0) < nreal_ref[0])
    def _():
        gu = _dot(x_ref[...], wgu_ref[0])
        gt, up = gu[:, :D_EXPERT], gu[:, D_EXPERT:]
        h = gt * jax.nn.sigmoid(gt) * up
        o_ref[...] = _dot(h.astype(BF16), wd_ref[0]).astype(o_ref.dtype)


def moe_experts(xs, block_exp, n_real, w_gu, w_down):
    P, D = xs.shape
    M = EXPERT_BLOCK
    grid_spec = pltpu.PrefetchScalarGridSpec(
        num_scalar_prefetch=2,
        grid=(P // M,),
        in_specs=[pl.BlockSpec((M, D), lambda i, be, nr: (i, 0)),
                  pl.BlockSpec((1, D, 2 * D_EXPERT), lambda i, be, nr: (be[i], 0, 0)),
                  pl.BlockSpec((1, D_EXPERT, D), lambda i, be, nr: (be[i], 0, 0))],
        out_specs=pl.BlockSpec((M, D), lambda i, be, nr: (i, 0)),
    )
    return pl.pallas_call(
        _expert_body,
        grid_spec=grid_spec,
        out_shape=jax.ShapeDtypeStruct((P, D), BF16),
        compiler_params=_cparams(("arbitrary",)),
        name="moe_experts",
    )(block_exp, n_real, xs, w_gu, w_down)


def _ffn_tail_body(x_ref, y_ref, w_ref, p_ref, wsg_ref, wsd_ref, g_ref, b_ref, wg_ref, bg_ref, wp_ref, o_ref):
    x = x_ref[...]
    routed = jnp.zeros(x.shape, F32)
    for k in range(TOP_K):
        routed = routed + y_ref[k].astype(F32) * w_ref[:, k:k + 1]
    gu = _dot(x.astype(BF16), wsg_ref[...].astype(BF16))
    gs, us = gu[:, :D_SHARED], gu[:, D_SHARED:]
    shared = _dot((gs * jax.nn.sigmoid(gs) * us).astype(BF16), wsd_ref[...].astype(BF16))
    x2 = _layer_norm(DEEPNORM_ALPHA * x + (routed + shared), g_ref[...], b_ref[...])
    gate = jax.nn.sigmoid(_dot(x2.astype(BF16), wg_ref[...].astype(BF16)) + bg_ref[...])
    pe = _dot(p_ref[...].astype(BF16), wp_ref[...].astype(BF16))
    o_ref[...] = x2 + gate * pe


def ffn_tail(x, yg, wt, p, ws_gu, ws_down, g, b, w_gate, b_gate, w_proj, tm=256):
    T, D = x.shape
    tm = min(tm, T)
    full = lambda shape: pl.BlockSpec(shape, lambda i: tuple(0 for _ in shape))
    return pl.pallas_call(
        _ffn_tail_body,
        grid=(T // tm,),
        in_specs=[pl.BlockSpec((tm, D), lambda i: (i, 0)),
                  pl.BlockSpec((TOP_K, tm, D), lambda i: (0, i, 0)),
                  pl.BlockSpec((tm, TOP_K), lambda i: (i, 0)),
                  pl.BlockSpec((tm, PLE_DIM), lambda i: (i, 0)),
                  full((D, 2 * D_SHARED)), full((D_SHARED, D)),
                  full((1, D)), full((1, D)), full((D, D)), full((1, D)), full((PLE_DIM, D))],
        out_specs=pl.BlockSpec((tm, D), lambda i: (i, 0)),
        out_shape=jax.ShapeDtypeStruct((T, D), F32),
        compiler_params=_cparams(("parallel",)),
        name="ffn_tail",
    )(x, yg, wt, p, ws_gu, ws_down, g.reshape(1, D), b.reshape(1, D), w_gate, b_gate.reshape(1, D), w_proj)


def _pad_cols(w, n):
    return jnp.pad(w, ((0, 0), (0, n - w.shape[1])))


def _fox_layer(x, w_in, b_f, t):
    B, S, D = x.shape
    xt = x.reshape(B * S, D)
    qkv = matmul(xt, w_in[:, :3 * D], BF16).reshape(B, S, 3 * D)
    f = matmul(xt, _pad_cols(w_in[:, 3 * D:], LANES), F32)[:, :N_HEADS].reshape(B, S, N_HEADS)
    log_f = jax.nn.log_sigmoid(f + b_f)
    tr = log_f.reshape(B, S // CUMSUM_BLOCK, CUMSUM_BLOCK, N_HEADS)
    within = jnp.cumsum(tr, axis=2)
    tot = within[:, :, -1]
    prefix = jnp.cumsum(tot, axis=1) - tot
    c = (within + prefix[:, :, None]).reshape(B, S, N_HEADS)
    c5 = c.transpose(0, 2, 1).reshape(B, N_PAIRS, 2, S // t, t)
    return fox_attention(qkv, c5, t).reshape(B * S, D)


def _sb_layer(x, w_in, t):
    B, S, D = x.shape
    qkv = matmul(x.reshape(B * S, D), w_in, BF16).reshape(B, S, 3 * D)
    return sb_attention(qkv, t).reshape(B * S, D)


def _rope_tables(positions):
    half = ROPE_DIM // 2
    inv_freq = jnp.exp(-math.log(ROPE_THETA) * 2.0 * jnp.arange(half, dtype=F32) / ROPE_DIM)
    ang = positions.astype(F32).reshape(-1, 1) * inv_freq
    cos, sin = jnp.cos(ang), jnp.sin(ang)
    T = ang.shape[0]
    rest = HEAD_DIM - ROPE_DIM
    cos64 = jnp.concatenate([cos, cos, jnp.ones((T, rest), F32)], axis=1)
    sa64 = jnp.concatenate([-sin, jnp.zeros((T, HEAD_DIM - half), F32)], axis=1)
    sb64 = jnp.concatenate([jnp.zeros((T, half), F32), sin, jnp.zeros((T, rest), F32)], axis=1)
    return tuple(jnp.tile(a, (1, LANES // HEAD_DIM)) for a in (cos64, sa64, sb64))


def _dsa_layer(x, positions, w_in, tq, tk):
    B, S, D = x.shape
    xt = x.reshape(B * S, D)
    dh = HEAD_DIM
    o_k, o_v, o_qi = D, D + dh, D + 2 * dh
    o_ki = o_qi + IDX_HEADS * IDX_DIM
    o_wi = o_ki + IDX_DIM
    wq, wk, wv = w_in[:, :o_k], w_in[:, o_k:o_v], w_in[:, o_v:o_qi]
    wqi, wki, wwi = w_in[:, o_qi:o_ki], w_in[:, o_ki:o_wi], w_in[:, o_wi:]
    w_rope = jnp.concatenate([wq, wqi, wk, wk, wki, wki], axis=1)
    cos, sa, sb = _rope_tables(positions)
    proj = matmul_rope(xt, w_rope, cos, sa, sb, BF16).reshape(B, S, -1)
    vdup = matmul(xt, jnp.concatenate([wv, wv], axis=1), BF16).reshape(B, S, LANES)
    wi = matmul(xt, _pad_cols(wwi * (IDX_HEADS ** -0.5 * IDX_DIM ** -0.5), LANES), F32).reshape(B, S, LANES)
    return dsa_attention(proj, vdup, wi, tq, tk).reshape(B * S, D)


def _gla_layer(x, w_in, w_a2, b_a, norm_g):
    B, S, D = x.shape
    xt = x.reshape(B * S, D)
    W = 2 * GLA_HEADS * GLA_DK + 2 * GLA_HEADS * GLA_DV
    qkvg = matmul(xt, w_in[:, :W], F32).reshape(B, S, W)
    a1 = matmul(xt, _pad_cols(w_in[:, W:], LANES), F32).reshape(B, S, LANES)
    w_a2p = jnp.pad(w_a2, ((0, LANES - GLA_GATE_RANK), (0, 0)))
    return gla_attention(qkvg, a1, w_a2p, b_a, norm_g).reshape(B * S, GLA_HEADS * GLA_DV)


def _moe_layer(x, p_i, w_router, b_router, w_gu, w_down, ws_gu, ws_down, g, b, w_gate, b_gate, w_proj):
    T, D = x.shape
    E, M = N_EXPERTS, EXPERT_BLOCK
    idx, wts, rank, cnt = moe_router(x, w_router, b_router)
    counts = cnt[:, 0]
    padded = (counts + M - 1) // M * M
    pend = jnp.cumsum(padded)
    poff = pend - padded
    dest = poff[idx] + rank
    P = (T * TOP_K + M - 1) // M * M + E * M
    nb = P // M
    tok = jnp.broadcast_to(jnp.arange(T, dtype=I32)[None, :], (TOP_K, T))
    row_tok = jnp.zeros((P,), I32).at[dest.reshape(-1)].set(tok.reshape(-1))
    block_exp = jnp.clip(jnp.searchsorted(pend, jnp.arange(nb, dtype=I32) * M, side="right"), 0, E - 1).astype(I32)
    n_real = (pend[-1] // M).astype(I32).reshape(1)
    xs = x.astype(BF16)[row_tok]
    ys = moe_experts(xs, block_exp, n_real, w_gu, w_down)
    yg = ys[dest]
    return ffn_tail(x, yg, wts.T, p_i, ws_gu, ws_down, g, b, w_gate, b_gate, w_proj)


def kernel(x, p, positions, ln1_g, ln1_b, ln2_g, ln2_b, fox_w_in, fox_b_f, fox_w_out, sb_w_in, sb_w_out, dsa_w_in, dsa_w_out, gla_w_in, gla_w_a2, gla_b_a, gla_norm_g, gla_w_out, moe_w_router, moe_b_router, moe_w_gu, moe_w_down, shared_w_gu, shared_w_down, ple_w_proj, ple_w_gate, ple_b_gate):
    B, S, D = x.shape
    T = B * S
    depth = p.shape[0]
    t_attn = min(256, S)
    w_gu_bf = moe_w_gu.astype(BF16)
    w_down_bf = moe_w_down.astype(BF16)
    xt = x.reshape(T, D)
    for i in range(depth):
        m, j = i % 4, i // 4
        xb = xt.reshape(B, S, D)
        if m == 0:
            a, w_out = _fox_layer(xb, fox_w_in[j], fox_b_f[j], t_attn), fox_w_out[j]
        elif m == 1:
            a, w_out = _sb_layer(xb, sb_w_in[j], t_attn), sb_w_out[j]
        elif m == 2:
            a, w_out = _dsa_layer(xb, positions, dsa_w_in[j], min(128, S), min(512, S)), dsa_w_out[j]
        else:
            a, w_out = _gla_layer(xb, gla_w_in[j], gla_w_a2[j], gla_b_a[j], gla_norm_g[j]), gla_w_out[j]
        xt = matmul_residual_ln(a, w_out, xt, ln1_g[i], ln1_b[i])
        xt = _moe_layer(xt, p[i].reshape(T, -1), moe_w_router[i], moe_b_router[i], w_gu_bf[i], w_down_bf[i],
                        shared_w_gu[i], shared_w_down[i], ln2_g[i], ln2_b[i],
                        ple_w_gate[i], ple_b_gate[i], ple_w_proj[i])
    return xt.reshape(B, S, D)
```

```python
import functools
import math

import jax
import jax.numpy as jnp
from jax import lax
from jax.experimental import pallas as pl
from jax.experimental.pallas import tpu as pltpu

F32 = jnp.float32
BF16 = jnp.bfloat16
I32 = jnp.int32

D_MODEL = 1024
HEAD_DIM = 64
N_HEADS = D_MODEL // HEAD_DIM
N_PAIRS = N_HEADS // 2
LANES = 128
ROPE_DIM = HEAD_DIM // 4
ROPE_THETA = 500000.0
IDX_HEADS = 8
IDX_DIM = HEAD_DIM
IDX_TOPK_MAX = 256
GLA_HEADS = 4
GLA_DK = D_MODEL // 2 // GLA_HEADS
GLA_DV = D_MODEL // GLA_HEADS
GLA_GATE_RANK = 16
GLA_TAU = 16.0
GLA_CHUNK = 64
N_EXPERTS = 64
TOP_K = 8
N_GROUPS = 8
GROUP_SIZE = N_EXPERTS // N_GROUPS
TOPK_GROUPS = 4
D_EXPERT = 256
D_SHARED = 256
ROUTED_SCALE = 2.5
EXPERT_BLOCK = 512
PLE_DIM = 256
DEPTH = 4
DEEPNORM_ALPHA = (2 * DEPTH) ** 0.25
LN_EPS = 1e-5
RMS_EPS = 1e-6
INT_MIN = -(2 ** 31)
VMEM_LIMIT = 56 * 1024 * 1024


def _cparams(sem):
    return pltpu.CompilerParams(dimension_semantics=sem, vmem_limit_bytes=VMEM_LIMIT)


def _pick(n, cands):
    for c in cands:
        if n % c == 0:
            return c
    raise ValueError(f"no tile for {n}")


def _dot(a, b):
    return jnp.dot(a, b, preferred_element_type=F32)


def _dot_nt(a, b):
    return lax.dot_general(a, b, (((1,), (1,)), ((), ())), preferred_element_type=F32)


def _dot_tn(a, b):
    return lax.dot_general(a, b, (((0,), (0,)), ((), ())), preferred_element_type=F32)


def _split3(x):
    h1 = x.astype(BF16)
    r1 = x - h1.astype(F32)
    h2 = r1.astype(BF16)
    h3 = (r1 - h2.astype(F32)).astype(BF16)
    return h1, h2, h3


def _log_sigmoid(x):
    return jnp.minimum(x, 0.0) - jnp.log1p(jnp.exp(-jnp.abs(x)))


def _mm_body(x_ref, w_ref, o_ref):
    o_ref[...] = _dot(x_ref[...].astype(BF16), w_ref[...].astype(BF16)).astype(o_ref.dtype)


def matmul(x, w, out_dtype, tm=512):
    M, K = x.shape
    N = w.shape[1]
    tm = min(tm, M)
    tn = _pick(N, (512, 384, 256, 128))
    return pl.pallas_call(
        _mm_body,
        grid=(M // tm, N // tn),
        in_specs=[pl.BlockSpec((tm, K), lambda i, j: (i, 0)),
                  pl.BlockSpec((K, tn), lambda i, j: (0, j))],
        out_specs=pl.BlockSpec((tm, tn), lambda i, j: (i, j)),
        out_shape=jax.ShapeDtypeStruct((M, N), out_dtype),
        compiler_params=_cparams(("parallel", "parallel")),
        name="matmul",
    )(x, w)


def _mm_rope_body(x_ref, w_ref, cos_ref, sa_ref, sb_ref, o_ref):
    y = _dot(x_ref[...].astype(BF16), w_ref[...].astype(BF16))
    cos, sa, sb = cos_ref[...], sa_ref[...], sb_ref[...]
    for c in range(y.shape[1] // LANES):
        yc = y[:, c * LANES:(c + 1) * LANES]
        oc = yc * cos + pltpu.roll(yc, LANES - ROPE_DIM // 2, 1) * sa + pltpu.roll(yc, ROPE_DIM // 2, 1) * sb
        o_ref[:, c * LANES:(c + 1) * LANES] = oc.astype(o_ref.dtype)


def matmul_rope(x, w, cos, sa, sb, out_dtype, tm=512):
    M, K = x.shape
    N = w.shape[1]
    tm = min(tm, M)
    tn = _pick(N, (256, 128))
    tab = pl.BlockSpec((tm, LANES), lambda i, j: (i, 0))
    return pl.pallas_call(
        _mm_rope_body,
        grid=(M // tm, N // tn),
        in_specs=[pl.BlockSpec((tm, K), lambda i, j: (i, 0)),
                  pl.BlockSpec((K, tn), lambda i, j: (0, j)), tab, tab, tab],
        out_specs=pl.BlockSpec((tm, tn), lambda i, j: (i, j)),
        out_shape=jax.ShapeDtypeStruct((M, N), out_dtype),
        compiler_params=_cparams(("parallel", "parallel")),
        name="matmul_rope",
    )(x, w, cos, sa, sb)


def _layer_norm(y, g, b):
    mu = jnp.mean(y, axis=-1, keepdims=True)
    d = y - mu
    var = jnp.mean(d * d, axis=-1, keepdims=True)
    return d * lax.rsqrt(var + LN_EPS) * g + b


def _mm_res_ln_body(a_ref, w_ref, x_ref, g_ref, b_ref, o_ref):
    h = _dot(a_ref[...].astype(BF16), w_ref[...].astype(BF16))
    o_ref[...] = _layer_norm(DEEPNORM_ALPHA * x_ref[...] + h, g_ref[...], b_ref[...])


def matmul_residual_ln(a, w, x, g, b, tm=512):
    M, K = a.shape
    N = w.shape[1]
    tm = min(tm, M)
    return pl.pallas_call(
        _mm_res_ln_body,
        grid=(M // tm,),
        in_specs=[pl.BlockSpec((tm, K), lambda i: (i, 0)),
                  pl.BlockSpec((K, N), lambda i: (0, 0)),
                  pl.BlockSpec((tm, N), lambda i: (i, 0)),
                  pl.BlockSpec((1, N), lambda i: (0, 0)),
                  pl.BlockSpec((1, N), lambda i: (0, 0))],
        out_specs=pl.BlockSpec((tm, N), lambda i: (i, 0)),
        out_shape=jax.ShapeDtypeStruct((M, N), F32),
        compiler_params=_cparams(("parallel",)),
        name="matmul_residual_ln",
    )(a, w, x, g.reshape(1, N), b.reshape(1, N))


def _mm_t_blocks_body(wt_ref, x_ref, o_ref):
    o_ref[0, 0, 0] = _dot_nt(wt_ref[...].astype(BF16), x_ref[...].astype(BF16)).astype(o_ref.dtype)


def _mm_t_rows_body(wt_ref, x_ref, o_ref):
    o_ref[0] = _dot_nt(wt_ref[...].astype(BF16), x_ref[...].astype(BF16)).astype(o_ref.dtype)


def matmul_t_blocks(xt, wt, B, t, out_dtype):
    T, K = xt.shape
    nk = T // B // t
    nj = wt.shape[0] // LANES
    return pl.pallas_call(
        _mm_t_blocks_body,
        grid=(B, nk, nj),
        in_specs=[pl.BlockSpec((LANES, K), lambda b, i, j: (j, 0)),
                  pl.BlockSpec((t, K), lambda b, i, j: (b * nk + i, 0))],
        out_specs=pl.BlockSpec((1, 1, 1, LANES, t), lambda b, i, j: (b, j, i, 0, 0)),
        out_shape=jax.ShapeDtypeStruct((B, nj, nk, LANES, t), out_dtype),
        compiler_params=_cparams(("parallel", "parallel", "parallel")),
        name="matmul_t_blocks",
    )(wt, xt)


def matmul_t_rows(xt, wt, B, out_dtype, tm=512):
    T, K = xt.shape
    S = T // B
    tm = min(tm, S)
    ns = S // tm
    return pl.pallas_call(
        _mm_t_rows_body,
        grid=(B, ns),
        in_specs=[pl.BlockSpec((LANES, K), lambda b, i: (0, 0)),
                  pl.BlockSpec((tm, K), lambda b, i: (b * ns + i, 0))],
        out_specs=pl.BlockSpec((1, LANES, tm), lambda b, i: (b, 0, i)),
        out_shape=jax.ShapeDtypeStruct((B, LANES, S), out_dtype),
        compiler_params=_cparams(("parallel", "parallel")),
        name="matmul_t_rows",
    )(wt, xt)


def _fox_gate_body(f_ref, bf_ref, tril_ref, place_ref, caug_ref, cb_ref, carry_sc, *, t):
    @pl.when(pl.program_id(1) == 0)
    def _reset():
        carry_sc[...] = jnp.zeros(carry_sc.shape, F32)

    log_f = _log_sigmoid(f_ref[0] + bf_ref[...])
    tril = tril_ref[...]
    h1, h2, h3 = _split3(log_f)
    c = _dot(tril, h1) + _dot(tril, h2) + _dot(tril, h3) + carry_sc[...]
    c0 = c[0:1, :]
    d1, d2, d3 = _split3(c0 - c)
    caug = _dot(d1, place_ref[0]) + _dot(d2, place_ref[1]) + _dot(d3, place_ref[2])
    caug_ref[0] = caug.astype(BF16)
    cb_ref[0, 0] = c0
    carry_sc[...] = c[t - 1:t, :]


def fox_gates(f, b_f, t):
    B, S, _ = f.shape
    nk = S // t
    ar = jnp.arange(t)
    tril = (ar[None, :] <= ar[:, None]).astype(BF16)
    hh = jnp.arange(LANES)
    place = jnp.stack([((hh[None, :] == 3 * hh[:, None] + j) & (hh[:, None] < N_HEADS)).astype(BF16)
                       for j in range(3)])
    return pl.pallas_call(
        functools.partial(_fox_gate_body, t=t),
        grid=(B, nk),
        in_specs=[pl.BlockSpec((1, t, LANES), lambda b, i: (b, i, 0)),
                  pl.BlockSpec((1, LANES), lambda b, i: (0, 0)),
                  pl.BlockSpec((t, t), lambda b, i: (0, 0)),
                  pl.BlockSpec((3, LANES, LANES), lambda b, i: (0, 0, 0))],
        out_specs=[pl.BlockSpec((1, t, LANES), lambda b, i: (b, i, 0)),
                   pl.BlockSpec((1, 1, 1, LANES), lambda b, i: (b, i, 0, 0))],
        out_shape=[jax.ShapeDtypeStruct((B, S, LANES), BF16),
                   jax.ShapeDtypeStruct((B, nk, 1, LANES), F32)],
        scratch_shapes=[pltpu.VMEM((1, LANES), F32)],
        compiler_params=_cparams(("parallel", "arbitrary")),
        name="fox_gates",
    )(f, _pad_cols(b_f.reshape(1, -1), LANES), tril, place)


def _pair_halves(q2):
    lane = lax.broadcasted_iota(I32, q2.shape, 1)
    lo = lane < HEAD_DIM
    zero = jnp.zeros_like(q2)
    return lane, (jnp.where(lo, q2, zero), jnp.where(lo, zero, q2))


def _merge_heads_t(o_a, o_b):
    sub = lax.broadcasted_iota(I32, o_a.shape, 0)
    return jnp.transpose(jnp.where(sub < HEAD_DIM, o_a, o_b))


def _fox_body(cb_ref, q_ref, k_ref, vt_ref, caug_ref, o_ref, m_sc, l_sc, acc_sc, *, t, nk):
    b, p, qb = pl.program_id(0), pl.program_id(1), pl.program_id(2)
    q2 = q_ref[0]
    lane, halves = _pair_halves(q2)
    rows = []
    for h in range(2):
        first = 3 * (2 * p + h)
        pick = jnp.logical_and(lane >= first, lane < first + 3)
        ones = jnp.where(pick, 1.0, 0.0).astype(BF16)
        rows.append(jnp.concatenate([halves[h], ones], axis=1))
    q_both = jnp.concatenate(rows, axis=0)
    m_sc[...] = jnp.full(m_sc.shape, -jnp.inf, F32)
    l_sc[...] = jnp.zeros(l_sc.shape, F32)
    acc_sc[...] = jnp.zeros(acc_sc.shape, F32)
    bases = [(b * N_HEADS + 2 * p + h) * nk for h in range(2)]

    def block(kb, masked):
        start = pl.multiple_of(kb * t, t)
        k_aug = jnp.concatenate([k_ref[0, pl.ds(start, t), :], caug_ref[0, pl.ds(start, t), :]], axis=1)
        s = _dot_nt(k_aug, q_both)
        if masked:
            kid = lax.broadcasted_iota(I32, (t, 2 * t), 0)
            qid = lax.broadcasted_iota(I32, (t, 2 * t), 1)
            qid = jnp.where(qid >= t, qid - t, qid)
            s = jnp.where(kid <= qid, s, -jnp.inf)
        off = jnp.concatenate([jnp.full((1, t), cb_ref[bs + qb] - cb_ref[bs + kb], F32) for bs in bases], axis=1)
        m_old = m_sc[...]
        m_new = jnp.maximum(m_old, jnp.max(s, axis=0, keepdims=True) + off)
        alpha = jnp.exp(m_old - m_new)
        pr = jnp.exp(s - (m_new - off))
        l_sc[...] = alpha * l_sc[...] + jnp.sum(pr, axis=0, keepdims=True)
        acc_sc[...] = alpha * acc_sc[...] + _dot(vt_ref[0, 0, kb], pr.astype(BF16))
        m_sc[...] = m_new

    def full_block(kb, carry):
        block(kb, False)
        return carry

    lax.fori_loop(0, qb, full_block, 0)
    block(qb, True)
    o = acc_sc[...] / l_sc[...]
    o_ref[0] = _merge_heads_t(o[:, :t], o[:, t:]).astype(o_ref.dtype)


def fox_attention(qk, vt, caug, cb, t):
    B, S, _ = qk.shape
    nk = S // t
    grid_spec = pltpu.PrefetchScalarGridSpec(
        num_scalar_prefetch=1,
        grid=(B, N_PAIRS, nk),
        in_specs=[pl.BlockSpec((1, t, LANES), lambda b, p, i, cb: (b, i, p)),
                  pl.BlockSpec((1, S, LANES), lambda b, p, i, cb: (b, 0, N_PAIRS + p)),
                  pl.BlockSpec((1, 1, nk, LANES, t), lambda b, p, i, cb: (b, p, 0, 0, 0)),
                  pl.BlockSpec((1, S, LANES), lambda b, p, i, cb: (b, 0, 0))],
        out_specs=pl.BlockSpec((1, t, LANES), lambda b, p, i, cb: (b, i, p)),
        scratch_shapes=[pltpu.VMEM((1, 2 * t), F32), pltpu.VMEM((1, 2 * t), F32),
                        pltpu.VMEM((LANES, 2 * t), F32)],
    )
    return pl.pallas_call(
        functools.partial(_fox_body, t=t, nk=nk),
        grid_spec=grid_spec,
        out_shape=jax.ShapeDtypeStruct((B, S, D_MODEL), BF16),
        compiler_params=_cparams(("parallel", "parallel", "arbitrary")),
        name="fox_attention",
    )(cb, qk, qk, vt, caug)


def _sb_body(q_ref, k_ref, vt_ref, upper_ref, o_ref, after_sc, acc_sc, *, t):
    qb = pl.program_id(2)
    _, halves = _pair_halves(q_ref[0])
    q_both = jnp.concatenate(halves, axis=0)
    after_sc[...] = jnp.zeros(after_sc.shape, F32)
    acc_sc[...] = jnp.zeros(acc_sc.shape, F32)
    upper = upper_ref[...]

    def block(kb, masked):
        start = pl.multiple_of(kb * t, t)
        z = _dot_nt(k_ref[0, pl.ds(start, t), :], q_both)
        lg = jnp.log(1.0 + jnp.exp(-jnp.abs(z)))
        u = jnp.minimum(-z, 0.0) - lg
        log_beta = jnp.minimum(z, 0.0) - lg
        if masked:
            kid = lax.broadcasted_iota(I32, (t, 2 * t), 0)
            qid = lax.broadcasted_iota(I32, (t, 2 * t), 1)
            strict = kid < jnp.where(qid >= t, qid - t, qid)
            u = jnp.where(strict, u, 0.0)
        rest = _dot(upper, u.astype(BF16)) + after_sc[...]
        a = jnp.exp(log_beta + rest)
        if masked:
            a = jnp.where(strict, a, 0.0)
        acc_sc[...] = acc_sc[...] + _dot(vt_ref[0, 0, kb], a.astype(BF16))
        after_sc[...] = after_sc[...] + jnp.sum(u, axis=0, keepdims=True)

    block(qb, True)

    def full_block(j, carry):
        block(qb - 1 - j, False)
        return carry

    lax.fori_loop(0, qb, full_block, 0)
    o = acc_sc[...]
    o_ref[0] = _merge_heads_t(o[:, :t], o[:, t:]).astype(o_ref.dtype)


def sb_attention(qk, vt, t):
    B, S, _ = qk.shape
    nk = S // t
    ar = jnp.arange(t)
    upper = (ar[None, :] > ar[:, None]).astype(BF16)
    return pl.pallas_call(
        functools.partial(_sb_body, t=t),
        grid=(B, N_PAIRS, nk),
        in_specs=[pl.BlockSpec((1, t, LANES), lambda b, p, i: (b, i, p)),
                  pl.BlockSpec((1, S, LANES), lambda b, p, i: (b, 0, N_PAIRS + p)),
                  pl.BlockSpec((1, 1, nk, LANES, t), lambda b, p, i: (b, p, 0, 0, 0)),
                  pl.BlockSpec((t, t), lambda b, p, i: (0, 0))],
        out_specs=pl.BlockSpec((1, t, LANES), lambda b, p, i: (b, i, p)),
        out_shape=jax.ShapeDtypeStruct((B, S, D_MODEL), BF16),
        scratch_shapes=[pltpu.VMEM((1, 2 * t), F32), pltpu.VMEM((LANES, 2 * t), F32)],
        compiler_params=_cparams(("parallel", "parallel", "arbitrary")),
        name="sb_attention",
    )(qk, qk, vt, upper)


def _sortable_key(x):
    bits = pltpu.bitcast(x, I32)
    return bits ^ ((bits >> 31) & jnp.int32(0x7FFFFFFF))


def _dsa_body(q_ref, qi_ref, k_ref, ki_ref, vt_ref, wi_ref, lower_ref, o_ref,
              keys_sc, thr_sc, need_sc, flag_sc, m_sc, l_sc, acc_sc, *, tq, tk, topk):
    qb = pl.program_id(1)
    p = pl.program_id(2)
    nch = (qb * tq) // tk + 1
    kid = lax.broadcasted_iota(I32, (tk, tq), 0)
    qpos = qb * tq + lax.broadcasted_iota(I32, (tk, tq), 1)

    @pl.when(p == 0)
    def _select():
        def score_chunk(c, carry):
            start = pl.multiple_of(c * tk, tk)
            kidup = ki_ref[0, pl.ds(start, tk), :]
            score = jnp.zeros((tk, tq), F32)
            for hp in range(IDX_HEADS // 2):
                _, halves = _pair_halves(qi_ref[0, :, hp * LANES:(hp + 1) * LANES])
                for j in range(2):
                    h = 2 * hp + j
                    score = score + jnp.maximum(_dot_nt(kidup, halves[j]), 0.0) * wi_ref[0, h:h + 1, :]
            keys_sc[c] = jnp.where(c * tk + kid <= qpos, _sortable_key(score), jnp.int32(INT_MIN))
            return carry

        lax.fori_loop(0, nch, score_chunk, 0)

        def count_ge(thr):
            def body(c, cnt):
                kc = keys_sc[c]
                for j in range(tk // 8):
                    cnt = cnt + jnp.where(kc[j * 8:(j + 1) * 8, :] >= thr, 1, 0)
                return cnt
            cnt = lax.fori_loop(0, nch, body, jnp.zeros((8, tq), I32))
            return jnp.sum(cnt, axis=0, keepdims=True)

        def bit_step(i, thr):
            cand = thr + jnp.left_shift(jnp.int32(1), 31 - i)
            return jnp.where(count_ge(cand) >= topk, cand, thr)

        thr = lax.fori_loop(0, 32, bit_step, jnp.full((1, tq), INT_MIN, I32))
        n_ge = count_ge(thr)
        n_gt = count_ge(thr + 1)
        need = topk - n_gt
        thr_sc[...] = thr
        need_sc[...] = need
        tie = jnp.logical_and(n_ge - n_gt > need, thr > INT_MIN)
        flag_sc[0] = jnp.max(jnp.where(tie, 1, 0))

    _, halves = _pair_halves(q_ref[0])
    q_both = jnp.concatenate(halves, axis=0)
    m_sc[...] = jnp.full(m_sc.shape, -1e30, F32)
    l_sc[...] = jnp.zeros(l_sc.shape, F32)
    acc_sc[...] = jnp.zeros(acc_sc.shape, F32)
    thr = thr_sc[...]

    def attend(c, sel):
        start = pl.multiple_of(c * tk, tk)
        s = _dot_nt(k_ref[0, pl.ds(start, tk), :], q_both)
        s = jnp.where(jnp.concatenate([sel, sel], axis=1), s, -1e30)
        m_old = m_sc[...]
        m_new = jnp.maximum(m_old, jnp.max(s, axis=0, keepdims=True))
        alpha = jnp.exp(m_old - m_new)
        pr = jnp.exp(s - m_new)
        l_sc[...] = alpha * l_sc[...] + jnp.sum(pr, axis=0, keepdims=True)
        acc_sc[...] = alpha * acc_sc[...] + _dot(vt_ref[0, c], pr.astype(BF16))
        m_sc[...] = m_new

    @pl.when(flag_sc[0] == 0)
    def _no_ties():
        def body(c, carry):
            kc = keys_sc[c]
            attend(c, jnp.logical_and(kc >= thr, kc > INT_MIN))
            return carry
        lax.fori_loop(0, nch, body, 0)

    @pl.when(flag_sc[0] != 0)
    def _ties():
        need = need_sc[...].astype(F32)

        def body(c, seen):
            kc = keys_sc[c]
            eq = kc == thr
            eqf = jnp.where(eq, 1.0, 0.0)
            rank = _dot(lower_ref[...], eqf.astype(BF16)) + seen
            sel = jnp.logical_or(kc > thr, jnp.logical_and(eq, rank < need))
            attend(c, jnp.logical_and(sel, kc > INT_MIN))
            return seen + jnp.sum(eqf, axis=0, keepdims=True)
        lax.fori_loop(0, nch, body, jnp.zeros((1, tq), F32))

    o = acc_sc[...] / l_sc[...]
    o_ref[0] = _merge_heads_t(o[:, :tq], o[:, tq:]).astype(o_ref.dtype)


def dsa_attention(proj, vt, wit, tq, tk):
    B, S, _ = proj.shape
    topk = min(IDX_TOPK_MAX, S // 4)
    ar = jnp.arange(tk)
    lower = (ar[None, :] < ar[:, None]).astype(BF16)
    kern = functools.partial(_dsa_body, tq=tq, tk=tk, topk=topk)
    qi_blk = D_MODEL // (IDX_HEADS * IDX_DIM)
    k_blk = (D_MODEL + IDX_HEADS * IDX_DIM) // LANES
    nk = S // tk
    return pl.pallas_call(
        kern,
        grid=(B, S // tq, N_PAIRS),
        in_specs=[pl.BlockSpec((1, tq, LANES), lambda b, i, p: (b, i, p)),
                  pl.BlockSpec((1, tq, IDX_HEADS * IDX_DIM), lambda b, i, p: (b, i, qi_blk)),
                  pl.BlockSpec((1, S, LANES), lambda b, i, p: (b, 0, k_blk)),
                  pl.BlockSpec((1, S, LANES), lambda b, i, p: (b, 0, k_blk + 1)),
                  pl.BlockSpec((1, nk, LANES, tk), lambda b, i, p: (b, 0, 0, 0)),
                  pl.BlockSpec((1, LANES, tq), lambda b, i, p: (b, 0, i)),
                  pl.BlockSpec((tk, tk), lambda b, i, p: (0, 0))],
        out_specs=pl.BlockSpec((1, tq, LANES), lambda b, i, p: (b, i, p)),
        out_shape=jax.ShapeDtypeStruct((B, S, D_MODEL), BF16),
        scratch_shapes=[pltpu.VMEM((nk, tk, tq), I32),
                        pltpu.VMEM((1, tq), I32), pltpu.VMEM((1, tq), I32),
                        pltpu.SMEM((1,), I32),
                        pltpu.VMEM((1, 2 * tq), F32), pltpu.VMEM((1, 2 * tq), F32),
                        pltpu.VMEM((LANES, 2 * tq), F32)],
        compiler_params=_cparams(("parallel", "arbitrary", "arbitrary")),
        name="dsa_attention",
    )(proj, proj, proj, proj, vt, wit, lower)


def _gla_body(qkvg_ref, a1_ref, wa2_ref, ba_ref, ng_ref, o_ref, state_sc, *, ts):
    C, H, dk, dv = GLA_CHUNK, GLA_HEADS, GLA_DK, GLA_DV

    @pl.when(pl.program_id(1) == 0)
    def _reset():
        state_sc[...] = jnp.zeros(state_sc.shape, F32)

    row = lax.broadcasted_iota(I32, (C, C), 0)
    colc = lax.broadcasted_iota(I32, (C, C), 1)
    causal = colc <= row
    tril = jnp.where(causal, 1.0, 0.0).astype(BF16)
    wa2 = wa2_ref[...].astype(BF16)
    ng = ng_ref[...]
    v_off, g_off = 2 * H * dk, 2 * H * dk + H * dv
    for n in range(ts // C):
        rows = slice(n * C, (n + 1) * C)
        za = _dot(a1_ref[0, rows, :].astype(BF16), wa2) + ba_ref[...]
        log_a = _log_sigmoid(za) / GLA_TAU
        h1, h2, h3 = _split3(log_a)
        bcum = _dot(tril, h1) + _dot(tril, h2) + _dot(tril, h3)
        eb = jnp.exp(bcum)
        enb = jnp.exp(-bcum)
        b_last = bcum[C - 1:C, :]
        eout = jnp.exp(b_last - bcum)
        dec = jnp.exp(b_last)
        for h in range(H):
            ks = slice(h * dk, (h + 1) * dk)
            q = qkvg_ref[0, rows, h * dk:(h + 1) * dk] * (dk ** -0.5)
            k = qkvg_ref[0, rows, H * dk + h * dk:H * dk + (h + 1) * dk]
            v = qkvg_ref[0, rows, v_off + h * dv:v_off + (h + 1) * dv].astype(BF16)
            q_in = (q * eb[:, ks]).astype(BF16)
            k_in = (k * enb[:, ks]).astype(BF16)
            k_out = (k * eout[:, ks]).astype(BF16)
            att = jnp.where(causal, _dot_nt(q_in, k_in), 0.0)
            state = state_sc[h]
            o = _dot(att.astype(BF16), v) + _dot(q_in, state.astype(BF16))
            dec_col = jnp.transpose(jnp.broadcast_to(dec[:, ks], (dk, dk)))[:, 0:1]
            state_sc[h] = state * dec_col + _dot_tn(k_out, v)
            o = o * lax.rsqrt(jnp.mean(o * o, axis=-1, keepdims=True) + RMS_EPS) * ng
            g = qkvg_ref[0, rows, g_off + h * dv:g_off + (h + 1) * dv]
            o = o * (g * jax.nn.sigmoid(g))
            o_ref[0, rows, h * dv:(h + 1) * dv] = o.astype(o_ref.dtype)


def gla_attention(qkvg, a1, w_a2p, b_a, norm_g, ts=512):
    B, S, W = qkvg.shape
    ts = min(ts, S)
    HK = GLA_HEADS * GLA_DK
    kern = functools.partial(_gla_body, ts=ts)
    return pl.pallas_call(
        kern,
        grid=(B, S // ts),
        in_specs=[pl.BlockSpec((1, ts, W), lambda b, i: (b, i, 0)),
                  pl.BlockSpec((1, ts, LANES), lambda b, i: (b, i, 0)),
                  pl.BlockSpec((LANES, HK), lambda b, i: (0, 0)),
                  pl.BlockSpec((1, HK), lambda b, i: (0, 0)),
                  pl.BlockSpec((1, GLA_DV), lambda b, i: (0, 0))],
        out_specs=pl.BlockSpec((1, ts, GLA_HEADS * GLA_DV), lambda b, i: (b, i, 0)),
        out_shape=jax.ShapeDtypeStruct((B, S, GLA_HEADS * GLA_DV), BF16),
        scratch_shapes=[pltpu.VMEM((GLA_HEADS, GLA_DK, GLA_DV), F32)],
        compiler_params=_cparams(("parallel", "arbitrary")),
        name="gla_attention",
    )(qkvg, a1, w_a2p, b_a.reshape(1, HK), norm_g.reshape(1, GLA_DV))


def _first_max(vals, ids, big):
    m = jnp.max(vals, axis=0, keepdims=True)
    first = jnp.min(jnp.where(vals == m, ids, big), axis=0, keepdims=True)
    return m, first


def _router_body(x_ref, wr_ref, br_ref, lower_ref, idx_ref, w_ref, rank_ref, cnt_ref, run_sc, *, tm):
    E, G, GS = N_EXPERTS, N_GROUPS, GROUP_SIZE

    @pl.when(pl.program_id(0) == 0)
    def _reset():
        run_sc[...] = jnp.zeros(run_sc.shape, F32)

    x = x_ref[...]
    x1 = x.astype(BF16)
    x2 = (x - x1.astype(F32)).astype(BF16)
    wr = wr_ref[...]
    w1 = wr.astype(BF16)
    w2 = (wr - w1.astype(F32)).astype(BF16)
    logits = _dot_nt(w1, x1) + _dot_nt(w1, x2) + _dot_nt(w2, x1)
    scores = jax.nn.sigmoid(logits)
    sel = scores + br_ref[...]
    neg = -jnp.inf
    eid = lax.broadcasted_iota(I32, (E, tm), 0)
    lid = lax.broadcasted_iota(I32, (GS, tm), 0)

    grp = []
    for g in range(G):
        sg = sel[g * GS:(g + 1) * GS, :]
        m1, f1 = _first_max(sg, lid, GS)
        m2 = jnp.max(jnp.where(lid == f1, neg, sg), axis=0, keepdims=True)
        grp.append(m1 + m2)
    grp = jnp.concatenate(grp, axis=0)
    gmask = jnp.zeros((G, tm), jnp.bool_)
    for _ in range(TOPK_GROUPS):
        _, f = _first_max(grp, lid, G)
        hit = lid == f
        gmask = jnp.logical_or(gmask, hit)
        grp = jnp.where(hit, neg, grp)
    emask = jnp.concatenate(
        [jnp.broadcast_to(gmask[g:g + 1, :], (GS, tm)) for g in range(G)], axis=0)
    sel = jnp.where(emask, sel, neg)

    chosen = jnp.zeros((E, tm), jnp.bool_)
    ids, ws = [], []
    for _ in range(TOP_K):
        _, f = _first_max(sel, eid, E)
        hit = eid == f
        ids.append(f)
        ws.append(jnp.sum(jnp.where(hit, scores, 0.0), axis=0, keepdims=True))
        chosen = jnp.logical_or(chosen, hit)
        sel = jnp.where(hit, neg, sel)
    ids = jnp.concatenate(ids, axis=0)
    ws = jnp.concatenate(ws, axis=0)
    ws = ws / jnp.sum(ws, axis=0, keepdims=True) * ROUTED_SCALE

    cf = jnp.where(chosen, 1.0, 0.0)
    before = _dot(cf.astype(BF16), lower_ref[...]) + run_sc[...]
    ranks = [jnp.sum(jnp.where(eid == ids[k:k + 1, :], before, 0.0), axis=0, keepdims=True)
             for k in range(TOP_K)]
    run_sc[...] = run_sc[...] + jnp.sum(cf, axis=1, keepdims=True)
    idx_ref[...] = ids
    w_ref[...] = ws
    rank_ref[...] = jnp.concatenate(ranks, axis=0).astype(I32)
    cnt_ref[...] = jnp.broadcast_to(run_sc[...], cnt_ref.shape).astype(I32)


def moe_router(x, w_router, b_router, tm=512):
    T, D = x.shape
    tm = min(tm, T)
    ar = jnp.arange(tm)
    lower = (ar[:, None] < ar[None, :]).astype(BF16)
    out = pl.BlockSpec((TOP_K, tm), lambda i: (0, i))
    return pl.pallas_call(
        functools.partial(_router_body, tm=tm),
        grid=(T // tm,),
        in_specs=[pl.BlockSpec((tm, D), lambda i: (i, 0)),
                  pl.BlockSpec((N_EXPERTS, D), lambda i: (0, 0)),
                  pl.BlockSpec((N_EXPERTS, 1), lambda i: (0, 0)),
                  pl.BlockSpec((tm, tm), lambda i: (0, 0))],
        out_specs=[out, out, out, pl.BlockSpec((N_EXPERTS, LANES), lambda i: (0, 0))],
        out_shape=[jax.ShapeDtypeStruct((TOP_K, T), I32), jax.ShapeDtypeStruct((TOP_K, T), F32),
                   jax.ShapeDtypeStruct((TOP_K, T), I32),
                   jax.ShapeDtypeStruct((N_EXPERTS, LANES), I32)],
        scratch_shapes=[pltpu.VMEM((N_EXPERTS, 1), F32)],
        compiler_params=_cparams(("arbitrary",)),
        name="moe_router",
    )(x, w_router.T, b_router.reshape(N_EXPERTS, 1), lower)


def _expert_body(be_ref, nreal_ref, x_ref, wgu_ref, wd_ref, o_ref):
    @pl.when(pl.program_id(0) < nreal_ref[0])
    def _():
        gu = _dot(x_ref[...], wgu_ref[0])
        gt, up = gu[:, :D_EXPERT], gu[:, D_EXPERT:]
        h = gt * jax.nn.sigmoid(gt) * up
        o_ref[...] = _dot(h.astype(BF16), wd_ref[0]).astype(o_ref.dtype)


def moe_experts(xs, block_exp, n_real, w_gu, w_down):
    P, D = xs.shape
    M = EXPERT_BLOCK
    grid_spec = pltpu.PrefetchScalarGridSpec(
        num_scalar_prefetch=2,
        grid=(P // M,),
        in_specs=[pl.BlockSpec((M, D), lambda i, be, nr: (i, 0)),
                  pl.BlockSpec((1, D, 2 * D_EXPERT), lambda i, be, nr: (be[i], 0, 0)),
                  pl.BlockSpec((1, D_EXPERT, D), lambda i, be, nr: (be[i], 0, 0))],
        out_specs=pl.BlockSpec((M, D), lambda i, be, nr: (i, 0)),
    )
    return pl.pallas_call(
        _expert_body,
        grid_spec=grid_spec,
        out_shape=jax.ShapeDtypeStruct((P, D), BF16),
        compiler_params=_cparams(("arbitrary",)),
        name="moe_experts",
    )(block_exp, n_real, xs, w_gu, w_down)


def _ffn_tail_body(x_ref, y_ref, w_ref, p_ref, wsg_ref, wsd_ref, g_ref, b_ref, wg_ref, bg_ref, wp_ref, o_ref):
    x = x_ref[...]
    routed = jnp.zeros(x.shape, F32)
    for k in range(TOP_K):
        routed = routed + y_ref[k].astype(F32) * w_ref[:, k:k + 1]
    gu = _dot(x.astype(BF16), wsg_ref[...].astype(BF16))
    gs, us = gu[:, :D_SHARED], gu[:, D_SHARED:]
    shared = _dot((gs * jax.nn.sigmoid(gs) * us).astype(BF16), wsd_ref[...].astype(BF16))
    x2 = _layer_norm(DEEPNORM_ALPHA * x + (routed + shared), g_ref[...], b_ref[...])
    gate = jax.nn.sigmoid(_dot(x2.astype(BF16), wg_ref[...].astype(BF16)) + bg_ref[...])
    pe = _dot(p_ref[...].astype(BF16), wp_ref[...].astype(BF16))
    o_ref[...] = x2 + gate * pe


def ffn_tail(x, yg, wt, p, ws_gu, ws_down, g, b, w_gate, b_gate, w_proj, tm=256):
    T, D = x.shape
    tm = min(tm, T)
    full = lambda shape: pl.BlockSpec(shape, lambda i: tuple(0 for _ in shape))
    return pl.pallas_call(
        _ffn_tail_body,
        grid=(T // tm,),
        in_specs=[pl.BlockSpec((tm, D), lambda i: (i, 0)),
                  pl.BlockSpec((TOP_K, tm, D), lambda i: (0, i, 0)),
                  pl.BlockSpec((tm, TOP_K), lambda i: (i, 0)),
                  pl.BlockSpec((tm, PLE_DIM), lambda i: (i, 0)),
                  full((D, 2 * D_SHARED)), full((D_SHARED, D)),
                  full((1, D)), full((1, D)), full((D, D)), full((1, D)), full((PLE_DIM, D))],
        out_specs=pl.BlockSpec((tm, D), lambda i: (i, 0)),
        out_shape=jax.ShapeDtypeStruct((T, D), F32),
        compiler_params=_cparams(("parallel",)),
        name="ffn_tail",
    )(x, yg, wt, p, ws_gu, ws_down, g.reshape(1, D), b.reshape(1, D), w_gate, b_gate.reshape(1, D), w_proj)


def _pad_cols(w, n):
    return jnp.pad(w, ((0, 0), (0, n - w.shape[1])))


def _qk_weights(wq, wk):
    return jnp.concatenate([wq * HEAD_DIM ** -0.5, wk], axis=1)


def _fox_layer(x, w_in, b_f, t):
    B, S, D = x.shape
    xt = x.reshape(B * S, D)
    qk = matmul(xt, _qk_weights(w_in[:, :D], w_in[:, D:2 * D]), BF16).reshape(B, S, 2 * D)
    vt = matmul_t_blocks(xt, w_in[:, 2 * D:3 * D].T, B, t, BF16)
    f = matmul(xt, _pad_cols(w_in[:, 3 * D:], LANES), F32).reshape(B, S, LANES)
    caug, cbs = fox_gates(f, b_f, t)
    cb = cbs[:, :, 0, :N_HEADS].transpose(0, 2, 1).reshape(-1)
    return fox_attention(qk, vt, caug, cb, t).reshape(B * S, D)


def _sb_layer(x, w_in, t):
    B, S, D = x.shape
    xt = x.reshape(B * S, D)
    qk = matmul(xt, _qk_weights(w_in[:, :D], w_in[:, D:2 * D]), BF16).reshape(B, S, 2 * D)
    vt = matmul_t_blocks(xt, w_in[:, 2 * D:].T, B, t, BF16)
    return sb_attention(qk, vt, t).reshape(B * S, D)


def _rope_tables(positions):
    half = ROPE_DIM // 2
    inv_freq = jnp.exp(-math.log(ROPE_THETA) * 2.0 * jnp.arange(half, dtype=F32) / ROPE_DIM)
    ang = positions.astype(F32).reshape(-1, 1) * inv_freq
    cos, sin = jnp.cos(ang), jnp.sin(ang)
    T = ang.shape[0]
    rest = HEAD_DIM - ROPE_DIM
    cos64 = jnp.concatenate([cos, cos, jnp.ones((T, rest), F32)], axis=1)
    sa64 = jnp.concatenate([-sin, jnp.zeros((T, HEAD_DIM - half), F32)], axis=1)
    sb64 = jnp.concatenate([jnp.zeros((T, half), F32), sin, jnp.zeros((T, rest), F32)], axis=1)
    return tuple(jnp.tile(a, (1, LANES // HEAD_DIM)) for a in (cos64, sa64, sb64))


def _dsa_layer(x, positions, w_in, tq, tk):
    B, S, D = x.shape
    xt = x.reshape(B * S, D)
    dh = HEAD_DIM
    o_k, o_v, o_qi = D, D + dh, D + 2 * dh
    o_ki = o_qi + IDX_HEADS * IDX_DIM
    o_wi = o_ki + IDX_DIM
    wq, wk, wv = w_in[:, :o_k], w_in[:, o_k:o_v], w_in[:, o_v:o_qi]
    wqi, wki, wwi = w_in[:, o_qi:o_ki], w_in[:, o_ki:o_wi], w_in[:, o_wi:]
    w_rope = jnp.concatenate([wq * dh ** -0.5, wqi, wk, wk, wki, wki], axis=1)
    cos, sa, sb = _rope_tables(positions)
    proj = matmul_rope(xt, w_rope, cos, sa, sb, BF16).reshape(B, S, -1)
    vt = matmul_t_blocks(xt, jnp.concatenate([wv, wv], axis=1).T, B, tk, BF16).reshape(B, S // tk, LANES, tk)
    wit = matmul_t_rows(xt, _pad_cols(wwi * (IDX_HEADS ** -0.5 * IDX_DIM ** -0.5), LANES).T, B, F32)
    return dsa_attention(proj, vt, wit, tq, tk).reshape(B * S, D)


def _gla_layer(x, w_in, w_a2, b_a, norm_g):
    B, S, D = x.shape
    xt = x.reshape(B * S, D)
    W = 2 * GLA_HEADS * GLA_DK + 2 * GLA_HEADS * GLA_DV
    qkvg = matmul(xt, w_in[:, :W], F32).reshape(B, S, W)
    a1 = matmul(xt, _pad_cols(w_in[:, W:], LANES), F32).reshape(B, S, LANES)
    w_a2p = jnp.pad(w_a2, ((0, LANES - GLA_GATE_RANK), (0, 0)))
    return gla_attention(qkvg, a1, w_a2p, b_a, norm_g).reshape(B * S, GLA_HEADS * GLA_DV)


def _moe_layer(x, p_i, w_router, b_router, w_gu, w_down, ws_gu, ws_down, g, b, w_gate, b_gate, w_proj):
    T, D = x.shape
    E, M = N_EXPERTS, EXPERT_BLOCK
    idx, wts, rank, cnt = moe_router(x, w_router, b_router)
    counts = cnt[:, 0]
    padded = (counts + M - 1) // M * M
    pend = jnp.cumsum(padded)
    poff = pend - padded
    dest = poff[idx] + rank
    P = (T * TOP_K + M - 1) // M * M + E * M
    nb = P // M
    tok = jnp.broadcast_to(jnp.arange(T, dtype=I32)[None, :], (TOP_K, T))
    row_tok = jnp.zeros((P,), I32).at[dest.reshape(-1)].set(tok.reshape(-1))
    block_exp = jnp.clip(jnp.searchsorted(pend, jnp.arange(nb, dtype=I32) * M, side="right"), 0, E - 1).astype(I32)
    n_real = (pend[-1] // M).astype(I32).reshape(1)
    xs = x.astype(BF16)[row_tok]
    ys = moe_experts(xs, block_exp, n_real, w_gu, w_down)
    yg = ys[dest]
    return ffn_tail(x, yg, wts.T, p_i, ws_gu, ws_down, g, b, w_gate, b_gate, w_proj)


def kernel(x, p, positions, ln1_g, ln1_b, ln2_g, ln2_b, fox_w_in, fox_b_f, fox_w_out, sb_w_in, sb_w_out, dsa_w_in, dsa_w_out, gla_w_in, gla_w_a2, gla_b_a, gla_norm_g, gla_w_out, moe_w_router, moe_b_router, moe_w_gu, moe_w_down, shared_w_gu, shared_w_down, ple_w_proj, ple_w_gate, ple_b_gate):
    B, S, D = x.shape
    T = B * S
    depth = p.shape[0]
    t_attn = min(512, S)
    w_gu_bf = moe_w_gu.astype(BF16)
    w_down_bf = moe_w_down.astype(BF16)
    xt = x.reshape(T, D)
    for i in range(depth):
        m, j = i % 4, i // 4
        xb = xt.reshape(B, S, D)
        if m == 0:
            a, w_out = _fox_layer(xb, fox_w_in[j], fox_b_f[j], t_attn), fox_w_out[j]
        elif m == 1:
            a, w_out = _sb_layer(xb, sb_w_in[j], t_attn), sb_w_out[j]
        elif m == 2:
            a, w_out = _dsa_layer(xb, positions, dsa_w_in[j], min(256, S), min(512, S)), dsa_w_out[j]
        else:
            a, w_out = _gla_layer(xb, gla_w_in[j], gla_w_a2[j], gla_b_a[j], gla_norm_g[j]), gla_w_out[j]
        xt = matmul_residual_ln(a, w_out, xt, ln1_g[i], ln1_b[i])
        xt = _moe_layer(xt, p[i].reshape(T, -1), moe_w_router[i], moe_b_router[i], w_gu_bf[i], w_down_bf[i],
                        shared_w_gu[i], shared_w_down[i], ln2_g[i], ln2_b[i],
                        ple_w_gate[i], ple_b_gate[i], ple_w_proj[i])
    return xt.reshape(B, S, D)
```

```python
import functools
import math

import jax
import jax.numpy as jnp
from jax import lax
from jax.experimental import pallas as pl
from jax.experimental.pallas import tpu as pltpu

F32 = jnp.float32
BF16 = jnp.bfloat16
I32 = jnp.int32

D_MODEL = 1024
HEAD_DIM = 64
N_HEADS = D_MODEL // HEAD_DIM
N_PAIRS = N_HEADS // 2
LANES = 128
ROPE_DIM = HEAD_DIM // 4
ROPE_THETA = 500000.0
IDX_HEADS = 8
IDX_DIM = HEAD_DIM
IDX_TOPK_MAX = 256
GLA_HEADS = 4
GLA_DK = D_MODEL // 2 // GLA_HEADS
GLA_DV = D_MODEL // GLA_HEADS
GLA_GATE_RANK = 16
GLA_TAU = 16.0
GLA_CHUNK = 64
N_EXPERTS = 64
TOP_K = 8
N_GROUPS = 8
GROUP_SIZE = N_EXPERTS // N_GROUPS
TOPK_GROUPS = 4
D_EXPERT = 256
D_SHARED = 256
ROUTED_SCALE = 2.5
EXPERT_BLOCK = 512
PLE_DIM = 256
DEPTH = 4
DEEPNORM_ALPHA = (2 * DEPTH) ** 0.25
LN_EPS = 1e-5
RMS_EPS = 1e-6
INT_MIN = -(2 ** 31)
VMEM_LIMIT = 56 * 1024 * 1024


def _cparams(sem):
    return pltpu.CompilerParams(dimension_semantics=sem, vmem_limit_bytes=VMEM_LIMIT)


def _pick(n, cands):
    for c in cands:
        if n % c == 0:
            return c
    raise ValueError(f"no tile for {n}")


def _dot(a, b):
    return jnp.dot(a, b, preferred_element_type=F32)


def _dot_nt(a, b):
    return lax.dot_general(a, b, (((1,), (1,)), ((), ())), preferred_element_type=F32)


def _dot_tn(a, b):
    return lax.dot_general(a, b, (((0,), (0,)), ((), ())), preferred_element_type=F32)


def _split3(x):
    h1 = x.astype(BF16)
    r1 = x - h1.astype(F32)
    h2 = r1.astype(BF16)
    h3 = (r1 - h2.astype(F32)).astype(BF16)
    return h1, h2, h3


def _log_sigmoid(x):
    return jnp.minimum(x, 0.0) - jnp.log1p(jnp.exp(-jnp.abs(x)))


def _mm_body(x_ref, w_ref, o_ref):
    o_ref[...] = _dot(x_ref[...].astype(BF16), w_ref[...].astype(BF16)).astype(o_ref.dtype)


def matmul(x, w, out_dtype, tm=512):
    M, K = x.shape
    N = w.shape[1]
    tm = min(tm, M)
    tn = _pick(N, (512, 384, 256, 128))
    return pl.pallas_call(
        _mm_body,
        grid=(M // tm, N // tn),
        in_specs=[pl.BlockSpec((tm, K), lambda i, j: (i, 0)),
                  pl.BlockSpec((K, tn), lambda i, j: (0, j))],
        out_specs=pl.BlockSpec((tm, tn), lambda i, j: (i, j)),
        out_shape=jax.ShapeDtypeStruct((M, N), out_dtype),
        compiler_params=_cparams(("parallel", "parallel")),
        name="matmul",
    )(x, w)


def _mm_rope_body(x_ref, w_ref, cos_ref, sa_ref, sb_ref, o_ref):
    y = _dot(x_ref[...].astype(BF16), w_ref[...].astype(BF16))
    cos, sa, sb = cos_ref[...], sa_ref[...], sb_ref[...]
    for c in range(y.shape[1] // LANES):
        yc = y[:, c * LANES:(c + 1) * LANES]
        oc = yc * cos + pltpu.roll(yc, LANES - ROPE_DIM // 2, 1) * sa + pltpu.roll(yc, ROPE_DIM // 2, 1) * sb
        o_ref[:, c * LANES:(c + 1) * LANES] = oc.astype(o_ref.dtype)


def matmul_rope(x, w, cos, sa, sb, out_dtype, tm=512):
    M, K = x.shape
    N = w.shape[1]
    tm = min(tm, M)
    tn = _pick(N, (256, 128))
    tab = pl.BlockSpec((tm, LANES), lambda i, j: (i, 0))
    return pl.pallas_call(
        _mm_rope_body,
        grid=(M // tm, N // tn),
        in_specs=[pl.BlockSpec((tm, K), lambda i, j: (i, 0)),
                  pl.BlockSpec((K, tn), lambda i, j: (0, j)), tab, tab, tab],
        out_specs=pl.BlockSpec((tm, tn), lambda i, j: (i, j)),
        out_shape=jax.ShapeDtypeStruct((M, N), out_dtype),
        compiler_params=_cparams(("parallel", "parallel")),
        name="matmul_rope",
    )(x, w, cos, sa, sb)


def _layer_norm(y, g, b):
    mu = jnp.mean(y, axis=-1, keepdims=True)
    d = y - mu
    var = jnp.mean(d * d, axis=-1, keepdims=True)
    return d * lax.rsqrt(var + LN_EPS) * g + b


def _mm_res_ln_body(a_ref, w_ref, x_ref, g_ref, b_ref, o_ref):
    h = _dot(a_ref[...].astype(BF16), w_ref[...].astype(BF16))
    o_ref[...] = _layer_norm(DEEPNORM_ALPHA * x_ref[...] + h, g_ref[...], b_ref[...])


def matmul_residual_ln(a, w, x, g, b, tm=512):
    M, K = a.shape
    N = w.shape[1]
    tm = min(tm, M)
    return pl.pallas_call(
        _mm_res_ln_body,
        grid=(M // tm,),
        in_specs=[pl.BlockSpec((tm, K), lambda i: (i, 0)),
                  pl.BlockSpec((K, N), lambda i: (0, 0)),
                  pl.BlockSpec((tm, N), lambda i: (i, 0)),
                  pl.BlockSpec((1, N), lambda i: (0, 0)),
                  pl.BlockSpec((1, N), lambda i: (0, 0))],
        out_specs=pl.BlockSpec((tm, N), lambda i: (i, 0)),
        out_shape=jax.ShapeDtypeStruct((M, N), F32),
        compiler_params=_cparams(("parallel",)),
        name="matmul_residual_ln",
    )(a, w, x, g.reshape(1, N), b.reshape(1, N))


def _mm_t_blocks_body(wt_ref, x_ref, o_ref):
    o_ref[0, 0, 0] = _dot_nt(wt_ref[...].astype(BF16), x_ref[...].astype(BF16)).astype(o_ref.dtype)


def _mm_t_rows_body(wt_ref, x_ref, o_ref):
    o_ref[0] = _dot_nt(wt_ref[...].astype(BF16), x_ref[...].astype(BF16)).astype(o_ref.dtype)


def matmul_t_blocks(xt, wt, B, t, out_dtype):
    T, K = xt.shape
    nk = T // B // t
    nj = wt.shape[0] // LANES
    return pl.pallas_call(
        _mm_t_blocks_body,
        grid=(B, nk, nj),
        in_specs=[pl.BlockSpec((LANES, K), lambda b, i, j: (j, 0)),
                  pl.BlockSpec((t, K), lambda b, i, j: (b * nk + i, 0))],
        out_specs=pl.BlockSpec((1, 1, 1, LANES, t), lambda b, i, j: (b, j, i, 0, 0)),
        out_shape=jax.ShapeDtypeStruct((B, nj, nk, LANES, t), out_dtype),
        compiler_params=_cparams(("parallel", "parallel", "parallel")),
        name="matmul_t_blocks",
    )(wt, xt)


def matmul_t_rows(xt, wt, B, out_dtype, tm=512):
    T, K = xt.shape
    S = T // B
    tm = min(tm, S)
    ns = S // tm
    return pl.pallas_call(
        _mm_t_rows_body,
        grid=(B, ns),
        in_specs=[pl.BlockSpec((LANES, K), lambda b, i: (0, 0)),
                  pl.BlockSpec((tm, K), lambda b, i: (b * ns + i, 0))],
        out_specs=pl.BlockSpec((1, LANES, tm), lambda b, i: (b, 0, i)),
        out_shape=jax.ShapeDtypeStruct((B, LANES, S), out_dtype),
        compiler_params=_cparams(("parallel", "parallel")),
        name="matmul_t_rows",
    )(wt, xt)


def _fox_gate_body(f_ref, bf_ref, tril_ref, place_ref, caug_ref, cb_ref, carry_sc, *, t):
    @pl.when(pl.program_id(1) == 0)
    def _reset():
        carry_sc[...] = jnp.zeros(carry_sc.shape, F32)

    log_f = _log_sigmoid(f_ref[0] + bf_ref[...])
    tril = tril_ref[...]
    h1, h2, h3 = _split3(log_f)
    c = _dot(tril, h1) + _dot(tril, h2) + _dot(tril, h3) + carry_sc[...]
    c0 = c[0:1, :]
    d1, d2, d3 = _split3(c0 - c)
    caug = _dot(d1, place_ref[0]) + _dot(d2, place_ref[1]) + _dot(d3, place_ref[2])
    caug_ref[0] = caug.astype(BF16)
    cb_ref[0, 0] = c0
    carry_sc[...] = c[t - 1:t, :]


def fox_gates(f, b_f, t):
    B, S, _ = f.shape
    nk = S // t
    ar = jnp.arange(t)
    tril = (ar[None, :] <= ar[:, None]).astype(BF16)
    hh = jnp.arange(LANES)
    place = jnp.stack([((hh[None, :] == 3 * hh[:, None] + j) & (hh[:, None] < N_HEADS)).astype(BF16)
                       for j in range(3)])
    return pl.pallas_call(
        functools.partial(_fox_gate_body, t=t),
        grid=(B, nk),
        in_specs=[pl.BlockSpec((1, t, LANES), lambda b, i: (b, i, 0)),
                  pl.BlockSpec((1, LANES), lambda b, i: (0, 0)),
                  pl.BlockSpec((t, t), lambda b, i: (0, 0)),
                  pl.BlockSpec((3, LANES, LANES), lambda b, i: (0, 0, 0))],
        out_specs=[pl.BlockSpec((1, t, LANES), lambda b, i: (b, i, 0)),
                   pl.BlockSpec((1, 1, 1, LANES), lambda b, i: (b, i, 0, 0))],
        out_shape=[jax.ShapeDtypeStruct((B, S, LANES), BF16),
                   jax.ShapeDtypeStruct((B, nk, 1, LANES), F32)],
        scratch_shapes=[pltpu.VMEM((1, LANES), F32)],
        compiler_params=_cparams(("parallel", "arbitrary")),
        name="fox_gates",
    )(f, _pad_cols(b_f.reshape(1, -1), LANES), tril, place)


def _pair_halves(q2):
    lane = lax.broadcasted_iota(I32, q2.shape, 1)
    lo = lane < HEAD_DIM
    zero = jnp.zeros_like(q2)
    return lane, (jnp.where(lo, q2, zero), jnp.where(lo, zero, q2))


def _merge_heads_t(o_a, o_b):
    sub = lax.broadcasted_iota(I32, o_a.shape, 0)
    return jnp.transpose(jnp.where(sub < HEAD_DIM, o_a, o_b))


def _fox_body(cb_ref, q_ref, k_ref, vt_ref, caug_ref, o_ref, m_sc, l_sc, acc_sc, *, t, nk):
    b, p, qb = pl.program_id(0), pl.program_id(1), pl.program_id(2)
    q2 = q_ref[0]
    lane, halves = _pair_halves(q2)
    rows = []
    for h in range(2):
        first = 3 * (2 * p + h)
        pick = jnp.logical_and(lane >= first, lane < first + 3)
        ones = jnp.where(pick, 1.0, 0.0).astype(BF16)
        rows.append(jnp.concatenate([halves[h], ones], axis=1))
    q_both = jnp.concatenate(rows, axis=0)
    m_sc[...] = jnp.full(m_sc.shape, -jnp.inf, F32)
    l_sc[...] = jnp.zeros(l_sc.shape, F32)
    acc_sc[...] = jnp.zeros(acc_sc.shape, F32)
    bases = [(b * N_HEADS + 2 * p + h) * nk for h in range(2)]

    def block(kb, masked):
        start = pl.multiple_of(kb * t, t)
        k_aug = jnp.concatenate([k_ref[0, pl.ds(start, t), :], caug_ref[0, pl.ds(start, t), :]], axis=1)
        s = _dot_nt(k_aug, q_both)
        if masked:
            kid = lax.broadcasted_iota(I32, (t, 2 * t), 0)
            qid = lax.broadcasted_iota(I32, (t, 2 * t), 1)
            qid = jnp.where(qid >= t, qid - t, qid)
            s = jnp.where(kid <= qid, s, -jnp.inf)
        off = jnp.concatenate([jnp.full((1, t), cb_ref[bs + qb] - cb_ref[bs + kb], F32) for bs in bases], axis=1)
        m_old = m_sc[...]
        m_new = jnp.maximum(m_old, jnp.max(s, axis=0, keepdims=True) + off)
        alpha = jnp.exp(m_old - m_new)
        pr = jnp.exp(s - (m_new - off))
        l_sc[...] = alpha * l_sc[...] + jnp.sum(pr, axis=0, keepdims=True)
        acc_sc[...] = alpha * acc_sc[...] + _dot(vt_ref[0, 0, kb], pr.astype(BF16))
        m_sc[...] = m_new

    def full_block(kb, carry):
        block(kb, False)
        return carry

    lax.fori_loop(0, qb, full_block, 0)
    block(qb, True)
    o = acc_sc[...] / l_sc[...]
    o_ref[0] = _merge_heads_t(o[:, :t], o[:, t:]).astype(o_ref.dtype)


def fox_attention(qk, vt, caug, cb, t):
    B, S, _ = qk.shape
    nk = S // t
    grid_spec = pltpu.PrefetchScalarGridSpec(
        num_scalar_prefetch=1,
        grid=(B, N_PAIRS, nk),
        in_specs=[pl.BlockSpec((1, t, LANES), lambda b, p, i, cb: (b, i, p)),
                  pl.BlockSpec((1, S, LANES), lambda b, p, i, cb: (b, 0, N_PAIRS + p)),
                  pl.BlockSpec((1, 1, nk, LANES, t), lambda b, p, i, cb: (b, p, 0, 0, 0)),
                  pl.BlockSpec((1, S, LANES), lambda b, p, i, cb: (b, 0, 0))],
        out_specs=pl.BlockSpec((1, t, LANES), lambda b, p, i, cb: (b, i, p)),
        scratch_shapes=[pltpu.VMEM((1, 2 * t), F32), pltpu.VMEM((1, 2 * t), F32),
                        pltpu.VMEM((LANES, 2 * t), F32)],
    )
    return pl.pallas_call(
        functools.partial(_fox_body, t=t, nk=nk),
        grid_spec=grid_spec,
        out_shape=jax.ShapeDtypeStruct((B, S, D_MODEL), BF16),
        compiler_params=_cparams(("parallel", "parallel", "arbitrary")),
        name="fox_attention",
    )(cb, qk, qk, vt, caug)


def _sb_body(q_ref, k_ref, vt_ref, upper_ref, o_ref, after_sc, acc_sc, *, t):
    qb = pl.program_id(2)
    _, halves = _pair_halves(q_ref[0])
    q_both = jnp.concatenate(halves, axis=0)
    after_sc[...] = jnp.zeros(after_sc.shape, F32)
    acc_sc[...] = jnp.zeros(acc_sc.shape, F32)
    upper = upper_ref[...]

    def block(kb, masked):
        start = pl.multiple_of(kb * t, t)
        z = _dot_nt(k_ref[0, pl.ds(start, t), :], q_both)
        lg = jnp.log(1.0 + jnp.exp(-jnp.abs(z)))
        u = jnp.minimum(-z, 0.0) - lg
        log_beta = jnp.minimum(z, 0.0) - lg
        if masked:
            kid = lax.broadcasted_iota(I32, (t, 2 * t), 0)
            qid = lax.broadcasted_iota(I32, (t, 2 * t), 1)
            strict = kid < jnp.where(qid >= t, qid - t, qid)
            u = jnp.where(strict, u, 0.0)
        rest = _dot(upper, u.astype(BF16)) + after_sc[...]
        a = jnp.exp(log_beta + rest)
        if masked:
            a = jnp.where(strict, a, 0.0)
        acc_sc[...] = acc_sc[...] + _dot(vt_ref[0, 0, kb], a.astype(BF16))
        after_sc[...] = after_sc[...] + jnp.sum(u, axis=0, keepdims=True)

    block(qb, True)

    def full_block(j, carry):
        block(qb - 1 - j, False)
        return carry

    lax.fori_loop(0, qb, full_block, 0)
    o = acc_sc[...]
    o_ref[0] = _merge_heads_t(o[:, :t], o[:, t:]).astype(o_ref.dtype)


def sb_attention(qk, vt, t):
    B, S, _ = qk.shape
    nk = S // t
    ar = jnp.arange(t)
    upper = (ar[None, :] > ar[:, None]).astype(BF16)
    return pl.pallas_call(
        functools.partial(_sb_body, t=t),
        grid=(B, N_PAIRS, nk),
        in_specs=[pl.BlockSpec((1, t, LANES), lambda b, p, i: (b, i, p)),
                  pl.BlockSpec((1, S, LANES), lambda b, p, i: (b, 0, N_PAIRS + p)),
                  pl.BlockSpec((1, 1, nk, LANES, t), lambda b, p, i: (b, p, 0, 0, 0)),
                  pl.BlockSpec((t, t), lambda b, p, i: (0, 0))],
        out_specs=pl.BlockSpec((1, t, LANES), lambda b, p, i: (b, i, p)),
        out_shape=jax.ShapeDtypeStruct((B, S, D_MODEL), BF16),
        scratch_shapes=[pltpu.VMEM((1, 2 * t), F32), pltpu.VMEM((LANES, 2 * t), F32)],
        compiler_params=_cparams(("parallel", "parallel", "arbitrary")),
        name="sb_attention",
    )(qk, qk, vt, upper)


def _sortable_key(x):
    bits = pltpu.bitcast(x, I32)
    return bits ^ ((bits >> 31) & jnp.int32(0x7FFFFFFF))


def _dsa_body(q_ref, qi_ref, k_ref, ki_ref, vt_ref, wi_ref, lower_ref, o_ref,
              keys_sc, thr_sc, need_sc, flag_sc, m_sc, l_sc, acc_sc, *, tq, tk, topk):
    qb = pl.program_id(1)
    p = pl.program_id(2)
    nch = (qb * tq) // tk + 1
    kid = lax.broadcasted_iota(I32, (tk, tq), 0)
    qpos = qb * tq + lax.broadcasted_iota(I32, (tk, tq), 1)

    @pl.when(p == 0)
    def _select():
        def score_chunk(c, carry):
            start = pl.multiple_of(c * tk, tk)
            kidup = ki_ref[0, pl.ds(start, tk), :]
            score = jnp.zeros((tk, tq), F32)
            for hp in range(IDX_HEADS // 2):
                _, halves = _pair_halves(qi_ref[0, :, hp * LANES:(hp + 1) * LANES])
                for j in range(2):
                    h = 2 * hp + j
                    score = score + jnp.maximum(_dot_nt(kidup, halves[j]), 0.0) * wi_ref[0, h:h + 1, :]
            keys_sc[c] = jnp.where(c * tk + kid <= qpos, _sortable_key(score), jnp.int32(INT_MIN))
            return carry

        lax.fori_loop(0, nch, score_chunk, 0)

        def count_ge(thr):
            def body(c, cnt):
                kc = keys_sc[c]
                for j in range(tk // 8):
                    cnt = cnt + jnp.where(kc[j * 8:(j + 1) * 8, :] >= thr, 1, 0)
                return cnt
            cnt = lax.fori_loop(0, nch, body, jnp.zeros((8, tq), I32))
            return jnp.sum(cnt, axis=0, keepdims=True)

        def bit_step(i, thr):
            cand = thr + jnp.left_shift(jnp.int32(1), 31 - i)
            return jnp.where(count_ge(cand) >= topk, cand, thr)

        thr = lax.fori_loop(0, 32, bit_step, jnp.full((1, tq), INT_MIN, I32))
        n_ge = count_ge(thr)
        n_gt = count_ge(thr + 1)
        need = topk - n_gt
        thr_sc[...] = thr
        need_sc[...] = need
        tie = jnp.logical_and(n_ge - n_gt > need, thr > INT_MIN)
        flag_sc[0] = jnp.max(jnp.where(tie, 1, 0))

    _, halves = _pair_halves(q_ref[0])
    q_both = jnp.concatenate(halves, axis=0)
    m_sc[...] = jnp.full(m_sc.shape, -1e30, F32)
    l_sc[...] = jnp.zeros(l_sc.shape, F32)
    acc_sc[...] = jnp.zeros(acc_sc.shape, F32)
    thr = thr_sc[...]

    def attend(c, sel):
        start = pl.multiple_of(c * tk, tk)
        s = _dot_nt(k_ref[0, pl.ds(start, tk), :], q_both)
        s = jnp.where(jnp.concatenate([sel, sel], axis=1), s, -1e30)
        m_old = m_sc[...]
        m_new = jnp.maximum(m_old, jnp.max(s, axis=0, keepdims=True))
        alpha = jnp.exp(m_old - m_new)
        pr = jnp.exp(s - m_new)
        l_sc[...] = alpha * l_sc[...] + jnp.sum(pr, axis=0, keepdims=True)
        acc_sc[...] = alpha * acc_sc[...] + _dot(vt_ref[0, c], pr.astype(BF16))
        m_sc[...] = m_new

    @pl.when(flag_sc[0] == 0)
    def _no_ties():
        def body(c, carry):
            kc = keys_sc[c]
            attend(c, jnp.logical_and(kc >= thr, kc > INT_MIN))
            return carry
        lax.fori_loop(0, nch, body, 0)

    @pl.when(flag_sc[0] != 0)
    def _ties():
        need = need_sc[...].astype(F32)

        def body(c, seen):
            kc = keys_sc[c]
            eq = kc == thr
            eqf = jnp.where(eq, 1.0, 0.0)
            rank = _dot(lower_ref[...], eqf.astype(BF16)) + seen
            sel = jnp.logical_or(kc > thr, jnp.logical_and(eq, rank < need))
            attend(c, jnp.logical_and(sel, kc > INT_MIN))
            return seen + jnp.sum(eqf, axis=0, keepdims=True)
        lax.fori_loop(0, nch, body, jnp.zeros((1, tq), F32))

    o = acc_sc[...] / l_sc[...]
    o_ref[0] = _merge_heads_t(o[:, :tq], o[:, tq:]).astype(o_ref.dtype)


def dsa_attention(proj, vt, wit, tq, tk):
    B, S, _ = proj.shape
    topk = min(IDX_TOPK_MAX, S // 4)
    ar = jnp.arange(tk)
    lower = (ar[None, :] < ar[:, None]).astype(BF16)
    kern = functools.partial(_dsa_body, tq=tq, tk=tk, topk=topk)
    qi_blk = D_MODEL // (IDX_HEADS * IDX_DIM)
    k_blk = (D_MODEL + IDX_HEADS * IDX_DIM) // LANES
    nk = S // tk
    return pl.pallas_call(
        kern,
        grid=(B, S // tq, N_PAIRS),
        in_specs=[pl.BlockSpec((1, tq, LANES), lambda b, i, p: (b, i, p)),
                  pl.BlockSpec((1, tq, IDX_HEADS * IDX_DIM), lambda b, i, p: (b, i, qi_blk)),
                  pl.BlockSpec((1, S, LANES), lambda b, i, p: (b, 0, k_blk)),
                  pl.BlockSpec((1, S, LANES), lambda b, i, p: (b, 0, k_blk + 1)),
                  pl.BlockSpec((1, nk, LANES, tk), lambda b, i, p: (b, 0, 0, 0)),
                  pl.BlockSpec((1, LANES, tq), lambda b, i, p: (b, 0, i)),
                  pl.BlockSpec((tk, tk), lambda b, i, p: (0, 0))],
        out_specs=pl.BlockSpec((1, tq, LANES), lambda b, i, p: (b, i, p)),
        out_shape=jax.ShapeDtypeStruct((B, S, D_MODEL), BF16),
        scratch_shapes=[pltpu.VMEM((nk, tk, tq), I32),
                        pltpu.VMEM((1, tq), I32), pltpu.VMEM((1, tq), I32),
                        pltpu.SMEM((1,), I32),
                        pltpu.VMEM((1, 2 * tq), F32), pltpu.VMEM((1, 2 * tq), F32),
                        pltpu.VMEM((LANES, 2 * tq), F32)],
        compiler_params=_cparams(("parallel", "arbitrary", "arbitrary")),
        name="dsa_attention",
    )(proj, proj, proj, proj, vt, wit, lower)


def _gla_body(qkvg_ref, a1_ref, wa2_ref, ba_ref, ng_ref, o_ref, state_sc, *, ts):
    C, H, dk, dv = GLA_CHUNK, GLA_HEADS, GLA_DK, GLA_DV

    @pl.when(pl.program_id(1) == 0)
    def _reset():
        state_sc[...] = jnp.zeros(state_sc.shape, F32)

    row = lax.broadcasted_iota(I32, (C, C), 0)
    colc = lax.broadcasted_iota(I32, (C, C), 1)
    causal = colc <= row
    tril = jnp.where(causal, 1.0, 0.0).astype(BF16)
    wa2 = wa2_ref[...].astype(BF16)
    ng = ng_ref[...]
    v_off, g_off = 2 * H * dk, 2 * H * dk + H * dv
    for n in range(ts // C):
        rows = slice(n * C, (n + 1) * C)
        za = _dot(a1_ref[0, rows, :].astype(BF16), wa2) + ba_ref[...]
        log_a = _log_sigmoid(za) / GLA_TAU
        h1, h2, h3 = _split3(log_a)
        bcum = _dot(tril, h1) + _dot(tril, h2) + _dot(tril, h3)
        eb = jnp.exp(bcum)
        enb = jnp.exp(-bcum)
        b_last = bcum[C - 1:C, :]
        eout = jnp.exp(b_last - bcum)
        dec = jnp.exp(b_last)
        for h in range(H):
            ks = slice(h * dk, (h + 1) * dk)
            q = qkvg_ref[0, rows, h * dk:(h + 1) * dk] * (dk ** -0.5)
            k = qkvg_ref[0, rows, H * dk + h * dk:H * dk + (h + 1) * dk]
            v = qkvg_ref[0, rows, v_off + h * dv:v_off + (h + 1) * dv].astype(BF16)
            q_in = (q * eb[:, ks]).astype(BF16)
            k_in = (k * enb[:, ks]).astype(BF16)
            k_out = (k * eout[:, ks]).astype(BF16)
            att = jnp.where(causal, _dot_nt(q_in, k_in), 0.0)
            state = state_sc[h]
            o = _dot(att.astype(BF16), v) + _dot(q_in, state.astype(BF16))
            dec_col = jnp.transpose(jnp.broadcast_to(dec[:, ks], (dk, dk)))[:, 0:1]
            state_sc[h] = state * dec_col + _dot_tn(k_out, v)
            o = o * lax.rsqrt(jnp.mean(o * o, axis=-1, keepdims=True) + RMS_EPS) * ng
            g = qkvg_ref[0, rows, g_off + h * dv:g_off + (h + 1) * dv]
            o = o * (g * jax.nn.sigmoid(g))
            o_ref[0, rows, h * dv:(h + 1) * dv] = o.astype(o_ref.dtype)


def gla_attention(qkvg, a1, w_a2p, b_a, norm_g, ts=512):
    B, S, W = qkvg.shape
    ts = min(ts, S)
    HK = GLA_HEADS * GLA_DK
    kern = functools.partial(_gla_body, ts=ts)
    return pl.pallas_call(
        kern,
        grid=(B, S // ts),
        in_specs=[pl.BlockSpec((1, ts, W), lambda b, i: (b, i, 0)),
                  pl.BlockSpec((1, ts, LANES), lambda b, i: (b, i, 0)),
                  pl.BlockSpec((LANES, HK), lambda b, i: (0, 0)),
                  pl.BlockSpec((1, HK), lambda b, i: (0, 0)),
                  pl.BlockSpec((1, GLA_DV), lambda b, i: (0, 0))],
        out_specs=pl.BlockSpec((1, ts, GLA_HEADS * GLA_DV), lambda b, i: (b, i, 0)),
        out_shape=jax.ShapeDtypeStruct((B, S, GLA_HEADS * GLA_DV), BF16),
        scratch_shapes=[pltpu.VMEM((GLA_HEADS, GLA_DK, GLA_DV), F32)],
        compiler_params=_cparams(("parallel", "arbitrary")),
        name="gla_attention",
    )(qkvg, a1, w_a2p, b_a.reshape(1, HK), norm_g.reshape(1, GLA_DV))


def _first_max(vals, ids, big):
    m = jnp.max(vals, axis=0, keepdims=True)
    first = jnp.min(jnp.where(vals == m, ids, big), axis=0, keepdims=True)
    return m, first


def _router_body(x_ref, wr_ref, br_ref, lower_ref, idx_ref, w_ref, rank_ref, cnt_ref, run_sc, *, tm):
    E, G, GS = N_EXPERTS, N_GROUPS, GROUP_SIZE

    @pl.when(pl.program_id(0) == 0)
    def _reset():
        run_sc[...] = jnp.zeros(run_sc.shape, F32)

    x = x_ref[...]
    x1 = x.astype(BF16)
    x2 = (x - x1.astype(F32)).astype(BF16)
    wr = wr_ref[...]
    w1 = wr.astype(BF16)
    w2 = (wr - w1.astype(F32)).astype(BF16)
    logits = _dot_nt(w1, x1) + _dot_nt(w1, x2) + _dot_nt(w2, x1)
    scores = jax.nn.sigmoid(logits)
    sel = scores + br_ref[...]
    neg = -jnp.inf
    eid = lax.broadcasted_iota(I32, (E, tm), 0)
    lid = lax.broadcasted_iota(I32, (GS, tm), 0)

    grp = []
    for g in range(G):
        sg = sel[g * GS:(g + 1) * GS, :]
        m1, f1 = _first_max(sg, lid, GS)
        m2 = jnp.max(jnp.where(lid == f1, neg, sg), axis=0, keepdims=True)
        grp.append(m1 + m2)
    grp = jnp.concatenate(grp, axis=0)
    gmask = jnp.zeros((G, tm), jnp.bool_)
    for _ in range(TOPK_GROUPS):
        _, f = _first_max(grp, lid, G)
        hit = lid == f
        gmask = jnp.logical_or(gmask, hit)
        grp = jnp.where(hit, neg, grp)
    emask = jnp.concatenate(
        [jnp.broadcast_to(gmask[g:g + 1, :], (GS, tm)) for g in range(G)], axis=0)
    sel = jnp.where(emask, sel, neg)

    chosen = jnp.zeros((E, tm), jnp.bool_)
    ids, ws = [], []
    for _ in range(TOP_K):
        _, f = _first_max(sel, eid, E)
        hit = eid == f
        ids.append(f)
        ws.append(jnp.sum(jnp.where(hit, scores, 0.0), axis=0, keepdims=True))
        chosen = jnp.logical_or(chosen, hit)
        sel = jnp.where(hit, neg, sel)
    ids = jnp.concatenate(ids, axis=0)
    ws = jnp.concatenate(ws, axis=0)
    ws = ws / jnp.sum(ws, axis=0, keepdims=True) * ROUTED_SCALE

    cf = jnp.where(chosen, 1.0, 0.0)
    before = _dot(cf.astype(BF16), lower_ref[...]) + run_sc[...]
    ranks = [jnp.sum(jnp.where(eid == ids[k:k + 1, :], before, 0.0), axis=0, keepdims=True)
             for k in range(TOP_K)]
    run_sc[...] = run_sc[...] + jnp.sum(cf, axis=1, keepdims=True)
    idx_ref[...] = ids
    w_ref[...] = ws
    rank_ref[...] = jnp.concatenate(ranks, axis=0).astype(I32)
    cnt_ref[...] = jnp.broadcast_to(run_sc[...], cnt_ref.shape).astype(I32)


def moe_router(x, w_router, b_router, tm=512):
    T, D = x.shape
    tm = min(tm, T)
    ar = jnp.arange(tm)
    lower = (ar[:, None] < ar[None, :]).astype(BF16)
    out = pl.BlockSpec((TOP_K, tm), lambda i: (0, i))
    return pl.pallas_call(
        functools.partial(_router_body, tm=tm),
        grid=(T // tm,),
        in_specs=[pl.BlockSpec((tm, D), lambda i: (i, 0)),
                  pl.BlockSpec((N_EXPERTS, D), lambda i: (0, 0)),
                  pl.BlockSpec((N_EXPERTS, 1), lambda i: (0, 0)),
                  pl.BlockSpec((tm, tm), lambda i: (0, 0))],
        out_specs=[out, out, out, pl.BlockSpec((N_EXPERTS, LANES), lambda i: (0, 0))],
        out_shape=[jax.ShapeDtypeStruct((TOP_K, T), I32), jax.ShapeDtypeStruct((TOP_K, T), F32),
                   jax.ShapeDtypeStruct((TOP_K, T), I32),
                   jax.ShapeDtypeStruct((N_EXPERTS, LANES), I32)],
        scratch_shapes=[pltpu.VMEM((N_EXPERTS, 1), F32)],
        compiler_params=_cparams(("arbitrary",)),
        name="moe_router",
    )(x, w_router.T, b_router.reshape(N_EXPERTS, 1), lower)


HALF = D_MODEL // 2


def _pack_rows(x):
    lo = pltpu.bitcast(x[:, :HALF].astype(BF16).astype(F32), I32)
    hi = pltpu.bitcast(x[:, HALF:].astype(BF16).astype(F32), I32)
    return lax.shift_right_logical(lo, 16) | (hi & jnp.int32(-65536))


def _unpack_rows(w):
    return pltpu.bitcast(w << 16, F32), pltpu.bitcast(w & jnp.int32(-65536), F32)


def _dispatch_body(x_ref, dest_hbm, xs_in, xs_hbm, idx_sm, pack_sc, sem_idx, sem_row, *, tm, nt):
    del xs_in
    i = pl.program_id(0)
    slot = i % 2
    n_rows = tm * TOP_K

    def idx_copy(tile, s):
        return pltpu.make_async_copy(dest_hbm.at[tile], idx_sm.at[s], sem_idx.at[s])

    def row_copy(s, t, row):
        return pltpu.make_async_copy(pack_sc.at[s, pl.ds(t, 1)], xs_hbm.at[pl.ds(row, 1)], sem_row.at[s])

    @pl.when(i == 0)
    def _first():
        idx_copy(0, 0).start()

    @pl.when(i + 1 < nt)
    def _prefetch():
        idx_copy(i + 1, 1 - slot).start()

    idx_copy(i, slot).wait()
    pack_sc[slot] = _pack_rows(x_ref[...])

    def issue(t, carry):
        for k in range(TOP_K):
            row_copy(slot, t, idx_sm[slot, t * TOP_K + k]).start()
        return carry

    lax.fori_loop(0, tm, issue, 0, unroll=4)

    def drain(s):
        pltpu.make_async_copy(xs_hbm.at[pl.ds(0, n_rows)], xs_hbm.at[pl.ds(n_rows, n_rows)], sem_row.at[s]).wait()

    @pl.when(i > 0)
    def _drain_prev():
        drain(1 - slot)

    @pl.when(i == nt - 1)
    def _drain_last():
        drain(slot)


def moe_dispatch(x, dest_tiles, n_rows_out, tm):
    T, D = x.shape
    nt = T // tm
    xs0 = jnp.zeros((n_rows_out, HALF), I32)
    return pl.pallas_call(
        functools.partial(_dispatch_body, tm=tm, nt=nt),
        grid=(nt,),
        in_specs=[pl.BlockSpec((tm, D), lambda i: (i, 0)),
                  pl.BlockSpec(memory_space=pl.ANY),
                  pl.BlockSpec(memory_space=pl.ANY)],
        out_specs=pl.BlockSpec(memory_space=pl.ANY),
        out_shape=jax.ShapeDtypeStruct((n_rows_out, HALF), I32),
        scratch_shapes=[pltpu.SMEM((2, tm * TOP_K), I32), pltpu.VMEM((2, tm, HALF), I32),
                        pltpu.SemaphoreType.DMA((2,)), pltpu.SemaphoreType.DMA((2,))],
        input_output_aliases={2: 0},
        compiler_params=_cparams(("arbitrary",)),
        name="moe_dispatch",
    )(x, dest_tiles, xs0)


def _expert_body(be_ref, nreal_ref, x_ref, wgu_ref, wd_ref, o_ref):
    @pl.when(pl.program_id(0) < nreal_ref[0])
    def _():
        lo, hi = _unpack_rows(x_ref[...])
        gu = _dot(lo.astype(BF16), wgu_ref[0, :HALF, :]) + _dot(hi.astype(BF16), wgu_ref[0, HALF:, :])
        gt, up = gu[:, :D_EXPERT], gu[:, D_EXPERT:]
        h = gt * jax.nn.sigmoid(gt) * up
        o_ref[...] = _pack_rows(_dot(h.astype(BF16), wd_ref[0]))

    @pl.when(pl.program_id(0) >= nreal_ref[0])
    def _():
        o_ref[...] = jnp.zeros(o_ref.shape, I32)


def moe_experts(xs, block_exp, n_real, w_gu, w_down):
    P = xs.shape[0]
    D, M = D_MODEL, EXPERT_BLOCK
    grid_spec = pltpu.PrefetchScalarGridSpec(
        num_scalar_prefetch=2,
        grid=(P // M,),
        in_specs=[pl.BlockSpec((M, HALF), lambda i, be, nr: (i, 0)),
                  pl.BlockSpec((1, D, 2 * D_EXPERT), lambda i, be, nr: (be[i], 0, 0)),
                  pl.BlockSpec((1, D_EXPERT, D), lambda i, be, nr: (be[i], 0, 0))],
        out_specs=pl.BlockSpec((M, HALF), lambda i, be, nr: (i, 0)),
    )
    return pl.pallas_call(
        _expert_body,
        grid_spec=grid_spec,
        out_shape=jax.ShapeDtypeStruct((P, HALF), I32),
        compiler_params=_cparams(("arbitrary",)),
        name="moe_experts",
    )(block_exp, n_real, xs, w_gu, w_down)


def _ffn_tail_body(x_ref, dest_hbm, ys_hbm, w_ref, p_ref, wsg_ref, wsd_ref, g_ref, b_ref, wg_ref, bg_ref, wp_ref,
                   o_ref, idx_sm, y_sc, sem_idx, sem_row, *, tm, nt):
    i = pl.program_id(0)
    slot = i % 2
    n_rows = tm * TOP_K

    def idx_copy(tile, s):
        return pltpu.make_async_copy(dest_hbm.at[tile], idx_sm.at[s], sem_idx.at[s])

    def row_copy(s, k, t, row):
        return pltpu.make_async_copy(ys_hbm.at[pl.ds(row, 1)], y_sc.at[s, pl.ds(k * tm + t, 1)], sem_row.at[s])

    def gather(tile, s):
        idx_copy(tile, s).start()
        idx_copy(tile, s).wait()

        def issue(t, carry):
            for k in range(TOP_K):
                row_copy(s, k, t, idx_sm[s, t * TOP_K + k]).start()
            return carry
        lax.fori_loop(0, tm, issue, 0, unroll=4)

    @pl.when(i == 0)
    def _first():
        gather(0, 0)

    @pl.when(i + 1 < nt)
    def _next():
        gather(i + 1, 1 - slot)

    x = x_ref[...]
    gu = _dot(x.astype(BF16), wsg_ref[...].astype(BF16))
    gs, us = gu[:, :D_SHARED], gu[:, D_SHARED:]
    shared = _dot((gs * jax.nn.sigmoid(gs) * us).astype(BF16), wsd_ref[...].astype(BF16))
    pe = _dot(p_ref[...].astype(BF16), wp_ref[...].astype(BF16))

    pltpu.make_async_copy(ys_hbm.at[pl.ds(0, n_rows)], y_sc.at[slot], sem_row.at[slot]).wait()

    r_lo = jnp.zeros((tm, HALF), F32)
    r_hi = jnp.zeros((tm, HALF), F32)
    for k in range(TOP_K):
        lo, hi = _unpack_rows(y_sc[slot, k * tm:(k + 1) * tm, :])
        wk = w_ref[:, k:k + 1]
        r_lo = r_lo + lo * wk
        r_hi = r_hi + hi * wk
    routed = jnp.concatenate([r_lo, r_hi], axis=1)
    x2 = _layer_norm(DEEPNORM_ALPHA * x + (routed + shared), g_ref[...], b_ref[...])
    gate = jax.nn.sigmoid(_dot(x2.astype(BF16), wg_ref[...].astype(BF16)) + bg_ref[...])
    o_ref[...] = x2 + gate * pe


def ffn_tail(x, dest_tiles, ys, wt, p, ws_gu, ws_down, g, b, w_gate, b_gate, w_proj, tm):
    T, D = x.shape
    nt = T // tm
    full = lambda shape: pl.BlockSpec(shape, lambda i: tuple(0 for _ in shape))
    return pl.pallas_call(
        functools.partial(_ffn_tail_body, tm=tm, nt=nt),
        grid=(nt,),
        in_specs=[pl.BlockSpec((tm, D), lambda i: (i, 0)),
                  pl.BlockSpec(memory_space=pl.ANY),
                  pl.BlockSpec(memory_space=pl.ANY),
                  pl.BlockSpec((tm, TOP_K), lambda i: (i, 0)),
                  pl.BlockSpec((tm, PLE_DIM), lambda i: (i, 0)),
                  full((D, 2 * D_SHARED)), full((D_SHARED, D)),
                  full((1, D)), full((1, D)), full((D, D)), full((1, D)), full((PLE_DIM, D))],
        out_specs=pl.BlockSpec((tm, D), lambda i: (i, 0)),
        out_shape=jax.ShapeDtypeStruct((T, D), F32),
        scratch_shapes=[pltpu.SMEM((2, tm * TOP_K), I32), pltpu.VMEM((2, TOP_K * tm, HALF), I32),
                        pltpu.SemaphoreType.DMA((2,)), pltpu.SemaphoreType.DMA((2,))],
        compiler_params=_cparams(("arbitrary",)),
        name="ffn_tail",
    )(x, dest_tiles, ys, wt, p, ws_gu, ws_down, g.reshape(1, D), b.reshape(1, D), w_gate,
      b_gate.reshape(1, D), w_proj)


def _pad_cols(w, n):
    return jnp.pad(w, ((0, 0), (0, n - w.shape[1])))


def _qk_weights(wq, wk):
    return jnp.concatenate([wq * HEAD_DIM ** -0.5, wk], axis=1)


def _fox_layer(x, w_in, b_f, t):
    B, S, D = x.shape
    xt = x.reshape(B * S, D)
    qk = matmul(xt, _qk_weights(w_in[:, :D], w_in[:, D:2 * D]), BF16).reshape(B, S, 2 * D)
    vt = matmul_t_blocks(xt, w_in[:, 2 * D:3 * D].T, B, t, BF16)
    f = matmul(xt, _pad_cols(w_in[:, 3 * D:], LANES), F32).reshape(B, S, LANES)
    caug, cbs = fox_gates(f, b_f, t)
    cb = cbs[:, :, 0, :N_HEADS].transpose(0, 2, 1).reshape(-1)
    return fox_attention(qk, vt, caug, cb, t).reshape(B * S, D)


def _sb_layer(x, w_in, t):
    B, S, D = x.shape
    xt = x.reshape(B * S, D)
    qk = matmul(xt, _qk_weights(w_in[:, :D], w_in[:, D:2 * D]), BF16).reshape(B, S, 2 * D)
    vt = matmul_t_blocks(xt, w_in[:, 2 * D:].T, B, t, BF16)
    return sb_attention(qk, vt, t).reshape(B * S, D)


def _rope_tables(positions):
    half = ROPE_DIM // 2
    inv_freq = jnp.exp(-math.log(ROPE_THETA) * 2.0 * jnp.arange(half, dtype=F32) / ROPE_DIM)
    ang = positions.astype(F32).reshape(-1, 1) * inv_freq
    cos, sin = jnp.cos(ang), jnp.sin(ang)
    T = ang.shape[0]
    rest = HEAD_DIM - ROPE_DIM
    cos64 = jnp.concatenate([cos, cos, jnp.ones((T, rest), F32)], axis=1)
    sa64 = jnp.concatenate([-sin, jnp.zeros((T, HEAD_DIM - half), F32)], axis=1)
    sb64 = jnp.concatenate([jnp.zeros((T, half), F32), sin, jnp.zeros((T, rest), F32)], axis=1)
    return tuple(jnp.tile(a, (1, LANES // HEAD_DIM)) for a in (cos64, sa64, sb64))


def _dsa_layer(x, positions, w_in, tq, tk):
    B, S, D = x.shape
    xt = x.reshape(B * S, D)
    dh = HEAD_DIM
    o_k, o_v, o_qi = D, D + dh, D + 2 * dh
    o_ki = o_qi + IDX_HEADS * IDX_DIM
    o_wi = o_ki + IDX_DIM
    wq, wk, wv = w_in[:, :o_k], w_in[:, o_k:o_v], w_in[:, o_v:o_qi]
    wqi, wki, wwi = w_in[:, o_qi:o_ki], w_in[:, o_ki:o_wi], w_in[:, o_wi:]
    w_rope = jnp.concatenate([wq * dh ** -0.5, wqi, wk, wk, wki, wki], axis=1)
    cos, sa, sb = _rope_tables(positions)
    proj = matmul_rope(xt, w_rope, cos, sa, sb, BF16).reshape(B, S, -1)
    vt = matmul_t_blocks(xt, jnp.concatenate([wv, wv], axis=1).T, B, tk, BF16).reshape(B, S // tk, LANES, tk)
    wit = matmul_t_rows(xt, _pad_cols(wwi * (IDX_HEADS ** -0.5 * IDX_DIM ** -0.5), LANES).T, B, F32)
    return dsa_attention(proj, vt, wit, tq, tk).reshape(B * S, D)


def _gla_layer(x, w_in, w_a2, b_a, norm_g):
    B, S, D = x.shape
    xt = x.reshape(B * S, D)
    W = 2 * GLA_HEADS * GLA_DK + 2 * GLA_HEADS * GLA_DV
    qkvg = matmul(xt, w_in[:, :W], F32).reshape(B, S, W)
    a1 = matmul(xt, _pad_cols(w_in[:, W:], LANES), F32).reshape(B, S, LANES)
    w_a2p = jnp.pad(w_a2, ((0, LANES - GLA_GATE_RANK), (0, 0)))
    return gla_attention(qkvg, a1, w_a2p, b_a, norm_g).reshape(B * S, GLA_HEADS * GLA_DV)


MOE_TOKEN_TILE = 256


def _moe_layer(x, p_i, w_router, b_router, w_gu, w_down, ws_gu, ws_down, g, b, w_gate, b_gate, w_proj):
    T, D = x.shape
    E, M = N_EXPERTS, EXPERT_BLOCK
    tm = min(MOE_TOKEN_TILE, T)
    idx, wts, rank, cnt = moe_router(x, w_router, b_router)
    counts = cnt[:, 0]
    padded = (counts + M - 1) // M * M
    pend = jnp.cumsum(padded)
    poff = pend - padded
    dest = poff[idx] + rank
    dest_tiles = dest.T.reshape(T // tm, tm * TOP_K)
    P = (T * TOP_K + M - 1) // M * M + E * M
    nb = P // M
    block_exp = jnp.clip(jnp.searchsorted(pend, jnp.arange(nb, dtype=I32) * M, side="right"), 0, E - 1).astype(I32)
    n_real = (pend[-1] // M).astype(I32).reshape(1)
    xs = moe_dispatch(x, dest_tiles, P, tm)
    ys = moe_experts(xs, block_exp, n_real, w_gu, w_down)
    return ffn_tail(x, dest_tiles, ys, wts.T, p_i, ws_gu, ws_down, g, b, w_gate, b_gate, w_proj, tm)


def kernel(x, p, positions, ln1_g, ln1_b, ln2_g, ln2_b, fox_w_in, fox_b_f, fox_w_out, sb_w_in, sb_w_out, dsa_w_in, dsa_w_out, gla_w_in, gla_w_a2, gla_b_a, gla_norm_g, gla_w_out, moe_w_router, moe_b_router, moe_w_gu, moe_w_down, shared_w_gu, shared_w_down, ple_w_proj, ple_w_gate, ple_b_gate):
    B, S, D = x.shape
    T = B * S
    depth = p.shape[0]
    t_attn = min(512, S)
    w_gu_bf = moe_w_gu.astype(BF16)
    w_down_bf = moe_w_down.astype(BF16)
    xt = x.reshape(T, D)
    for i in range(depth):
        m, j = i % 4, i // 4
        xb = xt.reshape(B, S, D)
        if m == 0:
            a, w_out = _fox_layer(xb, fox_w_in[j], fox_b_f[j], t_attn), fox_w_out[j]
        elif m == 1:
            a, w_out = _sb_layer(xb, sb_w_in[j], t_attn), sb_w_out[j]
        elif m == 2:
            a, w_out = _dsa_layer(xb, positions, dsa_w_in[j], min(256, S), min(512, S)), dsa_w_out[j]
        else:
            a, w_out = _gla_layer(xb, gla_w_in[j], gla_w_a2[j], gla_b_a[j], gla_norm_g[j]), gla_w_out[j]
        xt = matmul_residual_ln(a, w_out, xt, ln1_g[i], ln1_b[i])
        xt = _moe_layer(xt, p[i].reshape(T, -1), moe_w_router[i], moe_b_router[i], w_gu_bf[i], w_down_bf[i],
                        shared_w_gu[i], shared_w_down[i], ln2_g[i], ln2_b[i],
                        ple_w_gate[i], ple_b_gate[i], ple_w_proj[i])
    return xt.reshape(B, S, D)
```

```python
import functools
import math

import jax
import jax.numpy as jnp
from jax import lax
from jax.experimental import pallas as pl
from jax.experimental.pallas import tpu as pltpu

F32 = jnp.float32
BF16 = jnp.bfloat16
I32 = jnp.int32

D_MODEL = 1024
HEAD_DIM = 64
N_HEADS = D_MODEL // HEAD_DIM
N_PAIRS = N_HEADS // 2
LANES = 128
ROPE_DIM = HEAD_DIM // 4
ROPE_THETA = 500000.0
IDX_HEADS = 8
IDX_DIM = HEAD_DIM
IDX_TOPK_MAX = 256
GLA_HEADS = 4
GLA_DK = D_MODEL // 2 // GLA_HEADS
GLA_DV = D_MODEL // GLA_HEADS
GLA_GATE_RANK = 16
GLA_TAU = 16.0
GLA_CHUNK = 64
N_EXPERTS = 64
TOP_K = 8
N_GROUPS = 8
GROUP_SIZE = N_EXPERTS // N_GROUPS
TOPK_GROUPS = 4
D_EXPERT = 256
D_SHARED = 256
ROUTED_SCALE = 2.5
EXPERT_BLOCK = 512
PLE_DIM = 256
DEPTH = 4
DEEPNORM_ALPHA = (2 * DEPTH) ** 0.25
LN_EPS = 1e-5
RMS_EPS = 1e-6
INT_MIN = -(2 ** 31)
VMEM_LIMIT = 56 * 1024 * 1024


def _cparams(sem):
    return pltpu.CompilerParams(dimension_semantics=sem, vmem_limit_bytes=VMEM_LIMIT)


def _pick(n, cands):
    for c in cands:
        if n % c == 0:
            return c
    raise ValueError(f"no tile for {n}")


def _dot(a, b):
    return jnp.dot(a, b, preferred_element_type=F32)


def _dot_nt(a, b):
    return lax.dot_general(a, b, (((1,), (1,)), ((), ())), preferred_element_type=F32)


def _dot_tn(a, b):
    return lax.dot_general(a, b, (((0,), (0,)), ((), ())), preferred_element_type=F32)


def _split3(x):
    h1 = x.astype(BF16)
    r1 = x - h1.astype(F32)
    h2 = r1.astype(BF16)
    h3 = (r1 - h2.astype(F32)).astype(BF16)
    return h1, h2, h3


def _log_sigmoid(x):
    return jnp.minimum(x, 0.0) - jnp.log1p(jnp.exp(-jnp.abs(x)))


def _mm_body(x_ref, w_ref, o_ref):
    o_ref[...] = _dot(x_ref[...].astype(BF16), w_ref[...].astype(BF16)).astype(o_ref.dtype)


def matmul(x, w, out_dtype, tm=512):
    M, K = x.shape
    N = w.shape[1]
    tm = min(tm, M)
    tn = _pick(N, (512, 384, 256, 128))
    return pl.pallas_call(
        _mm_body,
        grid=(M // tm, N // tn),
        in_specs=[pl.BlockSpec((tm, K), lambda i, j: (i, 0)),
                  pl.BlockSpec((K, tn), lambda i, j: (0, j))],
        out_specs=pl.BlockSpec((tm, tn), lambda i, j: (i, j)),
        out_shape=jax.ShapeDtypeStruct((M, N), out_dtype),
        compiler_params=_cparams(("parallel", "parallel")),
        name="matmul",
    )(x, w)


def _mm_rope_body(x_ref, w_ref, cos_ref, sa_ref, sb_ref, o_ref):
    y = _dot(x_ref[...].astype(BF16), w_ref[...].astype(BF16))
    cos, sa, sb = cos_ref[...], sa_ref[...], sb_ref[...]
    for c in range(y.shape[1] // LANES):
        yc = y[:, c * LANES:(c + 1) * LANES]
        oc = yc * cos + pltpu.roll(yc, LANES - ROPE_DIM // 2, 1) * sa + pltpu.roll(yc, ROPE_DIM // 2, 1) * sb
        o_ref[:, c * LANES:(c + 1) * LANES] = oc.astype(o_ref.dtype)


def matmul_rope(x, w, cos, sa, sb, out_dtype, tm=512):
    M, K = x.shape
    N = w.shape[1]
    tm = min(tm, M)
    tn = _pick(N, (256, 128))
    tab = pl.BlockSpec((tm, LANES), lambda i, j: (i, 0))
    return pl.pallas_call(
        _mm_rope_body,
        grid=(M // tm, N // tn),
        in_specs=[pl.BlockSpec((tm, K), lambda i, j: (i, 0)),
                  pl.BlockSpec((K, tn), lambda i, j: (0, j)), tab, tab, tab],
        out_specs=pl.BlockSpec((tm, tn), lambda i, j: (i, j)),
        out_shape=jax.ShapeDtypeStruct((M, N), out_dtype),
        compiler_params=_cparams(("parallel", "parallel")),
        name="matmul_rope",
    )(x, w, cos, sa, sb)


def _layer_norm(y, g, b):
    mu = jnp.mean(y, axis=-1, keepdims=True)
    d = y - mu
    var = jnp.mean(d * d, axis=-1, keepdims=True)
    return d * lax.rsqrt(var + LN_EPS) * g + b


def _mm_res_ln_body(a_ref, w_ref, x_ref, g_ref, b_ref, o_ref):
    h = _dot(a_ref[...].astype(BF16), w_ref[...].astype(BF16))
    o_ref[...] = _layer_norm(DEEPNORM_ALPHA * x_ref[...] + h, g_ref[...], b_ref[...])


def matmul_residual_ln(a, w, x, g, b, tm=512):
    M, K = a.shape
    N = w.shape[1]
    tm = min(tm, M)
    return pl.pallas_call(
        _mm_res_ln_body,
        grid=(M // tm,),
        in_specs=[pl.BlockSpec((tm, K), lambda i: (i, 0)),
                  pl.BlockSpec((K, N), lambda i: (0, 0)),
                  pl.BlockSpec((tm, N), lambda i: (i, 0)),
                  pl.BlockSpec((1, N), lambda i: (0, 0)),
                  pl.BlockSpec((1, N), lambda i: (0, 0))],
        out_specs=pl.BlockSpec((tm, N), lambda i: (i, 0)),
        out_shape=jax.ShapeDtypeStruct((M, N), F32),
        compiler_params=_cparams(("parallel",)),
        name="matmul_residual_ln",
    )(a, w, x, g.reshape(1, N), b.reshape(1, N))


def _mm_t_blocks_body(wt_ref, x_ref, o_ref):
    o_ref[0, 0, 0] = _dot_nt(wt_ref[...].astype(BF16), x_ref[...].astype(BF16)).astype(o_ref.dtype)


def _mm_t_rows_body(wt_ref, x_ref, o_ref):
    o_ref[0] = _dot_nt(wt_ref[...].astype(BF16), x_ref[...].astype(BF16)).astype(o_ref.dtype)


def matmul_t_blocks(xt, wt, B, t, out_dtype):
    T, K = xt.shape
    nk = T // B // t
    nj = wt.shape[0] // LANES
    return pl.pallas_call(
        _mm_t_blocks_body,
        grid=(B, nk, nj),
        in_specs=[pl.BlockSpec((LANES, K), lambda b, i, j: (j, 0)),
                  pl.BlockSpec((t, K), lambda b, i, j: (b * nk + i, 0))],
        out_specs=pl.BlockSpec((1, 1, 1, LANES, t), lambda b, i, j: (b, j, i, 0, 0)),
        out_shape=jax.ShapeDtypeStruct((B, nj, nk, LANES, t), out_dtype),
        compiler_params=_cparams(("parallel", "parallel", "parallel")),
        name="matmul_t_blocks",
    )(wt, xt)


def matmul_t_rows(xt, wt, B, out_dtype, tm=512):
    T, K = xt.shape
    S = T // B
    tm = min(tm, S)
    ns = S // tm
    return pl.pallas_call(
        _mm_t_rows_body,
        grid=(B, ns),
        in_specs=[pl.BlockSpec((LANES, K), lambda b, i: (0, 0)),
                  pl.BlockSpec((tm, K), lambda b, i: (b * ns + i, 0))],
        out_specs=pl.BlockSpec((1, LANES, tm), lambda b, i: (b, 0, i)),
        out_shape=jax.ShapeDtypeStruct((B, LANES, S), out_dtype),
        compiler_params=_cparams(("parallel", "parallel")),
        name="matmul_t_rows",
    )(wt, xt)


def _fox_gate_body(f_ref, bf_ref, tril_ref, place_ref, caug_ref, cb_ref, carry_sc, *, t):
    @pl.when(pl.program_id(1) == 0)
    def _reset():
        carry_sc[...] = jnp.zeros(carry_sc.shape, F32)

    log_f = _log_sigmoid(f_ref[0] + bf_ref[...])
    tril = tril_ref[...]
    h1, h2, h3 = _split3(log_f)
    c = _dot(tril, h1) + _dot(tril, h2) + _dot(tril, h3) + carry_sc[...]
    c0 = c[0:1, :]
    d1, d2, d3 = _split3(c0 - c)
    caug = _dot(d1, place_ref[0]) + _dot(d2, place_ref[1]) + _dot(d3, place_ref[2])
    caug_ref[0] = caug.astype(BF16)
    cb_ref[0, 0] = c0
    carry_sc[...] = c[t - 1:t, :]


def fox_gates(f, b_f, t):
    B, S, _ = f.shape
    nk = S // t
    ar = jnp.arange(t)
    tril = (ar[None, :] <= ar[:, None]).astype(BF16)
    hh = jnp.arange(LANES)
    place = jnp.stack([((hh[None, :] == 3 * hh[:, None] + j) & (hh[:, None] < N_HEADS)).astype(BF16)
                       for j in range(3)])
    return pl.pallas_call(
        functools.partial(_fox_gate_body, t=t),
        grid=(B, nk),
        in_specs=[pl.BlockSpec((1, t, LANES), lambda b, i: (b, i, 0)),
                  pl.BlockSpec((1, LANES), lambda b, i: (0, 0)),
                  pl.BlockSpec((t, t), lambda b, i: (0, 0)),
                  pl.BlockSpec((3, LANES, LANES), lambda b, i: (0, 0, 0))],
        out_specs=[pl.BlockSpec((1, t, LANES), lambda b, i: (b, i, 0)),
                   pl.BlockSpec((1, 1, 1, LANES), lambda b, i: (b, i, 0, 0))],
        out_shape=[jax.ShapeDtypeStruct((B, S, LANES), BF16),
                   jax.ShapeDtypeStruct((B, nk, 1, LANES), F32)],
        scratch_shapes=[pltpu.VMEM((1, LANES), F32)],
        compiler_params=_cparams(("parallel", "arbitrary")),
        name="fox_gates",
    )(f, _pad_cols(b_f.reshape(1, -1), LANES), tril, place)


def _pair_halves(q2):
    lane = lax.broadcasted_iota(I32, q2.shape, 1)
    lo = lane < HEAD_DIM
    zero = jnp.zeros_like(q2)
    return lane, (jnp.where(lo, q2, zero), jnp.where(lo, zero, q2))


def _merge_heads_t(o_a, o_b):
    sub = lax.broadcasted_iota(I32, o_a.shape, 0)
    return jnp.transpose(jnp.where(sub < HEAD_DIM, o_a, o_b))


def _fox_body(cb_ref, q_ref, k_ref, vt_ref, caug_ref, o_ref, m_sc, l_sc, acc_sc, *, t, nk):
    b, p, qb = pl.program_id(0), pl.program_id(1), pl.program_id(2)
    q2 = q_ref[0]
    lane, halves = _pair_halves(q2)
    rows = []
    for h in range(2):
        first = 3 * (2 * p + h)
        pick = jnp.logical_and(lane >= first, lane < first + 3)
        ones = jnp.where(pick, 1.0, 0.0).astype(BF16)
        rows.append(jnp.concatenate([halves[h], ones], axis=1))
    q_both = jnp.concatenate(rows, axis=0)
    m_sc[...] = jnp.full(m_sc.shape, -jnp.inf, F32)
    l_sc[...] = jnp.zeros(l_sc.shape, F32)
    acc_sc[...] = jnp.zeros(acc_sc.shape, F32)
    bases = [(b * N_HEADS + 2 * p + h) * nk for h in range(2)]

    def block(kb, masked):
        start = pl.multiple_of(kb * t, t)
        k_aug = jnp.concatenate([k_ref[0, pl.ds(start, t), :], caug_ref[0, pl.ds(start, t), :]], axis=1)
        s = _dot_nt(k_aug, q_both)
        if masked:
            kid = lax.broadcasted_iota(I32, (t, 2 * t), 0)
            qid = lax.broadcasted_iota(I32, (t, 2 * t), 1)
            qid = jnp.where(qid >= t, qid - t, qid)
            s = jnp.where(kid <= qid, s, -jnp.inf)
        off = jnp.concatenate([jnp.full((1, t), cb_ref[bs + qb] - cb_ref[bs + kb], F32) for bs in bases], axis=1)
        m_old = m_sc[...]
        m_new = jnp.maximum(m_old, jnp.max(s, axis=0, keepdims=True) + off)
        alpha = jnp.exp(m_old - m_new)
        pr = jnp.exp(s - (m_new - off))
        l_sc[...] = alpha * l_sc[...] + jnp.sum(pr, axis=0, keepdims=True)
        acc_sc[...] = alpha * acc_sc[...] + _dot(vt_ref[0, 0, kb], pr.astype(BF16))
        m_sc[...] = m_new

    def full_block(kb, carry):
        block(kb, False)
        return carry

    lax.fori_loop(0, qb, full_block, 0)
    block(qb, True)
    o = acc_sc[...] / l_sc[...]
    o_ref[0] = _merge_heads_t(o[:, :t], o[:, t:]).astype(o_ref.dtype)


def fox_attention(qk, vt, caug, cb, t):
    B, S, _ = qk.shape
    nk = S // t
    grid_spec = pltpu.PrefetchScalarGridSpec(
        num_scalar_prefetch=1,
        grid=(B, N_PAIRS, nk),
        in_specs=[pl.BlockSpec((1, t, LANES), lambda b, p, i, cb: (b, i, p)),
                  pl.BlockSpec((1, S, LANES), lambda b, p, i, cb: (b, 0, N_PAIRS + p)),
                  pl.BlockSpec((1, 1, nk, LANES, t), lambda b, p, i, cb: (b, p, 0, 0, 0)),
                  pl.BlockSpec((1, S, LANES), lambda b, p, i, cb: (b, 0, 0))],
        out_specs=pl.BlockSpec((1, t, LANES), lambda b, p, i, cb: (b, i, p)),
        scratch_shapes=[pltpu.VMEM((1, 2 * t), F32), pltpu.VMEM((1, 2 * t), F32),
                        pltpu.VMEM((LANES, 2 * t), F32)],
    )
    return pl.pallas_call(
        functools.partial(_fox_body, t=t, nk=nk),
        grid_spec=grid_spec,
        out_shape=jax.ShapeDtypeStruct((B, S, D_MODEL), BF16),
        compiler_params=_cparams(("parallel", "parallel", "arbitrary")),
        name="fox_attention",
    )(cb, qk, qk, vt, caug)


def _sb_body(q_ref, k_ref, vt_ref, upper_ref, o_ref, after_sc, acc_sc, *, t):
    qb = pl.program_id(2)
    _, halves = _pair_halves(q_ref[0])
    q_both = jnp.concatenate(halves, axis=0)
    after_sc[...] = jnp.zeros(after_sc.shape, F32)
    acc_sc[...] = jnp.zeros(acc_sc.shape, F32)
    upper = upper_ref[...]

    def block(kb, masked):
        start = pl.multiple_of(kb * t, t)
        z = _dot_nt(k_ref[0, pl.ds(start, t), :], q_both)
        lg = jnp.log(1.0 + jnp.exp(-jnp.abs(z)))
        u = jnp.minimum(-z, 0.0) - lg
        log_beta = jnp.minimum(z, 0.0) - lg
        if masked:
            kid = lax.broadcasted_iota(I32, (t, 2 * t), 0)
            qid = lax.broadcasted_iota(I32, (t, 2 * t), 1)
            strict = kid < jnp.where(qid >= t, qid - t, qid)
            u = jnp.where(strict, u, 0.0)
        rest = _dot(upper, u.astype(BF16)) + after_sc[...]
        a = jnp.exp(log_beta + rest)
        if masked:
            a = jnp.where(strict, a, 0.0)
        acc_sc[...] = acc_sc[...] + _dot(vt_ref[0, 0, kb], a.astype(BF16))
        after_sc[...] = after_sc[...] + jnp.sum(u, axis=0, keepdims=True)

    block(qb, True)

    def full_block(j, carry):
        block(qb - 1 - j, False)
        return carry

    lax.fori_loop(0, qb, full_block, 0)
    o = acc_sc[...]
    o_ref[0] = _merge_heads_t(o[:, :t], o[:, t:]).astype(o_ref.dtype)


def sb_attention(qk, vt, t):
    B, S, _ = qk.shape
    nk = S // t
    ar = jnp.arange(t)
    upper = (ar[None, :] > ar[:, None]).astype(BF16)
    return pl.pallas_call(
        functools.partial(_sb_body, t=t),
        grid=(B, N_PAIRS, nk),
        in_specs=[pl.BlockSpec((1, t, LANES), lambda b, p, i: (b, i, p)),
                  pl.BlockSpec((1, S, LANES), lambda b, p, i: (b, 0, N_PAIRS + p)),
                  pl.BlockSpec((1, 1, nk, LANES, t), lambda b, p, i: (b, p, 0, 0, 0)),
                  pl.BlockSpec((t, t), lambda b, p, i: (0, 0))],
        out_specs=pl.BlockSpec((1, t, LANES), lambda b, p, i: (b, i, p)),
        out_shape=jax.ShapeDtypeStruct((B, S, D_MODEL), BF16),
        scratch_shapes=[pltpu.VMEM((1, 2 * t), F32), pltpu.VMEM((LANES, 2 * t), F32)],
        compiler_params=_cparams(("parallel", "parallel", "arbitrary")),
        name="sb_attention",
    )(qk, qk, vt, upper)


def _sortable_key(x):
    bits = pltpu.bitcast(x, I32)
    return bits ^ ((bits >> 31) & jnp.int32(0x7FFFFFFF))


def _dsa_body(q_ref, qi_ref, k_ref, ki_ref, vt_ref, wi_ref, lower_ref, o_ref,
              keys_sc, thr_sc, need_sc, flag_sc, m_sc, l_sc, acc_sc, *, tq, tk, topk):
    qb = pl.program_id(1)
    p = pl.program_id(2)
    nch = (qb * tq) // tk + 1
    kid = lax.broadcasted_iota(I32, (tk, tq), 0)
    qpos = qb * tq + lax.broadcasted_iota(I32, (tk, tq), 1)

    @pl.when(p == 0)
    def _select():
        def score_chunk(c, carry):
            start = pl.multiple_of(c * tk, tk)
            kidup = ki_ref[0, pl.ds(start, tk), :]
            score = jnp.zeros((tk, tq), F32)
            for hp in range(IDX_HEADS // 2):
                _, halves = _pair_halves(qi_ref[0, :, hp * LANES:(hp + 1) * LANES])
                for j in range(2):
                    h = 2 * hp + j
                    score = score + jnp.maximum(_dot_nt(kidup, halves[j]), 0.0) * wi_ref[0, h:h + 1, :]
            keys_sc[c] = jnp.where(c * tk + kid <= qpos, _sortable_key(score), jnp.int32(INT_MIN))
            return carry

        lax.fori_loop(0, nch, score_chunk, 0)

        def count_ge(thr):
            def body(c, cnt):
                kc = keys_sc[c]
                for j in range(tk // 8):
                    cnt = cnt + jnp.where(kc[j * 8:(j + 1) * 8, :] >= thr, 1, 0)
                return cnt
            cnt = lax.fori_loop(0, nch, body, jnp.zeros((8, tq), I32))
            return jnp.sum(cnt, axis=0, keepdims=True)

        def bit_step(i, thr):
            cand = thr + jnp.left_shift(jnp.int32(1), 31 - i)
            return jnp.where(count_ge(cand) >= topk, cand, thr)

        thr = lax.fori_loop(0, 32, bit_step, jnp.full((1, tq), INT_MIN, I32))
        n_ge = count_ge(thr)
        n_gt = count_ge(thr + 1)
        need = topk - n_gt
        thr_sc[...] = thr
        need_sc[...] = need
        tie = jnp.logical_and(n_ge - n_gt > need, thr > INT_MIN)
        flag_sc[0] = jnp.max(jnp.where(tie, 1, 0))

    _, halves = _pair_halves(q_ref[0])
    q_both = jnp.concatenate(halves, axis=0)
    m_sc[...] = jnp.full(m_sc.shape, -1e30, F32)
    l_sc[...] = jnp.zeros(l_sc.shape, F32)
    acc_sc[...] = jnp.zeros(acc_sc.shape, F32)
    thr = thr_sc[...]

    def attend(c, sel):
        start = pl.multiple_of(c * tk, tk)
        s = _dot_nt(k_ref[0, pl.ds(start, tk), :], q_both)
        s = jnp.where(jnp.concatenate([sel, sel], axis=1), s, -1e30)
        m_old = m_sc[...]
        m_new = jnp.maximum(m_old, jnp.max(s, axis=0, keepdims=True))
        alpha = jnp.exp(m_old - m_new)
        pr = jnp.exp(s - m_new)
        l_sc[...] = alpha * l_sc[...] + jnp.sum(pr, axis=0, keepdims=True)
        acc_sc[...] = alpha * acc_sc[...] + _dot(vt_ref[0, c], pr.astype(BF16))
        m_sc[...] = m_new

    @pl.when(flag_sc[0] == 0)
    def _no_ties():
        def body(c, carry):
            kc = keys_sc[c]
            attend(c, jnp.logical_and(kc >= thr, kc > INT_MIN))
            return carry
        lax.fori_loop(0, nch, body, 0)

    @pl.when(flag_sc[0] != 0)
    def _ties():
        need = need_sc[...].astype(F32)

        def body(c, seen):
            kc = keys_sc[c]
            eq = kc == thr
            eqf = jnp.where(eq, 1.0, 0.0)
            rank = _dot(lower_ref[...], eqf.astype(BF16)) + seen
            sel = jnp.logical_or(kc > thr, jnp.logical_and(eq, rank < need))
            attend(c, jnp.logical_and(sel, kc > INT_MIN))
            return seen + jnp.sum(eqf, axis=0, keepdims=True)
        lax.fori_loop(0, nch, body, jnp.zeros((1, tq), F32))

    o = acc_sc[...] / l_sc[...]
    o_ref[0] = _merge_heads_t(o[:, :tq], o[:, tq:]).astype(o_ref.dtype)


def dsa_attention(proj, vt, wit, tq, tk):
    B, S, _ = proj.shape
    topk = min(IDX_TOPK_MAX, S // 4)
    ar = jnp.arange(tk)
    lower = (ar[None, :] < ar[:, None]).astype(BF16)
    kern = functools.partial(_dsa_body, tq=tq, tk=tk, topk=topk)
    qi_blk = D_MODEL // (IDX_HEADS * IDX_DIM)
    k_blk = (D_MODEL + IDX_HEADS * IDX_DIM) // LANES
    nk = S // tk
    return pl.pallas_call(
        kern,
        grid=(B, S // tq, N_PAIRS),
        in_specs=[pl.BlockSpec((1, tq, LANES), lambda b, i, p: (b, i, p)),
                  pl.BlockSpec((1, tq, IDX_HEADS * IDX_DIM), lambda b, i, p: (b, i, qi_blk)),
                  pl.BlockSpec((1, S, LANES), lambda b, i, p: (b, 0, k_blk)),
                  pl.BlockSpec((1, S, LANES), lambda b, i, p: (b, 0, k_blk + 1)),
                  pl.BlockSpec((1, nk, LANES, tk), lambda b, i, p: (b, 0, 0, 0)),
                  pl.BlockSpec((1, LANES, tq), lambda b, i, p: (b, 0, i)),
                  pl.BlockSpec((tk, tk), lambda b, i, p: (0, 0))],
        out_specs=pl.BlockSpec((1, tq, LANES), lambda b, i, p: (b, i, p)),
        out_shape=jax.ShapeDtypeStruct((B, S, D_MODEL), BF16),
        scratch_shapes=[pltpu.VMEM((nk, tk, tq), I32),
                        pltpu.VMEM((1, tq), I32), pltpu.VMEM((1, tq), I32),
                        pltpu.SMEM((1,), I32),
                        pltpu.VMEM((1, 2 * tq), F32), pltpu.VMEM((1, 2 * tq), F32),
                        pltpu.VMEM((LANES, 2 * tq), F32)],
        compiler_params=_cparams(("parallel", "arbitrary", "arbitrary")),
        name="dsa_attention",
    )(proj, proj, proj, proj, vt, wit, lower)


def _gla_body(qkvg_ref, a1_ref, wa2_ref, ba_ref, ng_ref, o_ref, state_sc, *, ts):
    C, H, dk, dv = GLA_CHUNK, GLA_HEADS, GLA_DK, GLA_DV

    @pl.when(pl.program_id(1) == 0)
    def _reset():
        state_sc[...] = jnp.zeros(state_sc.shape, F32)

    row = lax.broadcasted_iota(I32, (C, C), 0)
    colc = lax.broadcasted_iota(I32, (C, C), 1)
    causal = colc <= row
    tril = jnp.where(causal, 1.0, 0.0).astype(BF16)
    wa2 = wa2_ref[...].astype(BF16)
    ng = ng_ref[...]
    v_off, g_off = 2 * H * dk, 2 * H * dk + H * dv
    for n in range(ts // C):
        rows = slice(n * C, (n + 1) * C)
        za = _dot(a1_ref[0, rows, :].astype(BF16), wa2) + ba_ref[...]
        log_a = _log_sigmoid(za) / GLA_TAU
        h1, h2, h3 = _split3(log_a)
        bcum = _dot(tril, h1) + _dot(tril, h2) + _dot(tril, h3)
        eb = jnp.exp(bcum)
        enb = jnp.exp(-bcum)
        b_last = bcum[C - 1:C, :]
        eout = jnp.exp(b_last - bcum)
        dec = jnp.exp(b_last)
        for h in range(H):
            ks = slice(h * dk, (h + 1) * dk)
            q = qkvg_ref[0, rows, h * dk:(h + 1) * dk] * (dk ** -0.5)
            k = qkvg_ref[0, rows, H * dk + h * dk:H * dk + (h + 1) * dk]
            v = qkvg_ref[0, rows, v_off + h * dv:v_off + (h + 1) * dv].astype(BF16)
            q_in = (q * eb[:, ks]).astype(BF16)
            k_in = (k * enb[:, ks]).astype(BF16)
            k_out = (k * eout[:, ks]).astype(BF16)
            att = jnp.where(causal, _dot_nt(q_in, k_in), 0.0)
            state = state_sc[h]
            o = _dot(att.astype(BF16), v) + _dot(q_in, state.astype(BF16))
            dec_col = jnp.transpose(jnp.broadcast_to(dec[:, ks], (dk, dk)))[:, 0:1]
            state_sc[h] = state * dec_col + _dot_tn(k_out, v)
            o = o * lax.rsqrt(jnp.mean(o * o, axis=-1, keepdims=True) + RMS_EPS) * ng
            g = qkvg_ref[0, rows, g_off + h * dv:g_off + (h + 1) * dv]
            o = o * (g * jax.nn.sigmoid(g))
            o_ref[0, rows, h * dv:(h + 1) * dv] = o.astype(o_ref.dtype)


def gla_attention(qkvg, a1, w_a2p, b_a, norm_g, ts=512):
    B, S, W = qkvg.shape
    ts = min(ts, S)
    HK = GLA_HEADS * GLA_DK
    kern = functools.partial(_gla_body, ts=ts)
    return pl.pallas_call(
        kern,
        grid=(B, S // ts),
        in_specs=[pl.BlockSpec((1, ts, W), lambda b, i: (b, i, 0)),
                  pl.BlockSpec((1, ts, LANES), lambda b, i: (b, i, 0)),
                  pl.BlockSpec((LANES, HK), lambda b, i: (0, 0)),
                  pl.BlockSpec((1, HK), lambda b, i: (0, 0)),
                  pl.BlockSpec((1, GLA_DV), lambda b, i: (0, 0))],
        out_specs=pl.BlockSpec((1, ts, GLA_HEADS * GLA_DV), lambda b, i: (b, i, 0)),
        out_shape=jax.ShapeDtypeStruct((B, S, GLA_HEADS * GLA_DV), BF16),
        scratch_shapes=[pltpu.VMEM((GLA_HEADS, GLA_DK, GLA_DV), F32)],
        compiler_params=_cparams(("parallel", "arbitrary")),
        name="gla_attention",
    )(qkvg, a1, w_a2p, b_a.reshape(1, HK), norm_g.reshape(1, GLA_DV))


def _first_max(vals, ids, big):
    m = jnp.max(vals, axis=0, keepdims=True)
    first = jnp.min(jnp.where(vals == m, ids, big), axis=0, keepdims=True)
    return m, first


def _router_body(x_ref, wr_ref, br_ref, lower_ref, idx_ref, w_ref, rank_ref, cnt_ref, run_sc, *, tm):
    E, G, GS = N_EXPERTS, N_GROUPS, GROUP_SIZE

    @pl.when(pl.program_id(0) == 0)
    def _reset():
        run_sc[...] = jnp.zeros(run_sc.shape, F32)

    x = x_ref[...]
    x1 = x.astype(BF16)
    x2 = (x - x1.astype(F32)).astype(BF16)
    wr = wr_ref[...]
    w1 = wr.astype(BF16)
    w2 = (wr - w1.astype(F32)).astype(BF16)
    logits = _dot_nt(w1, x1) + _dot_nt(w1, x2) + _dot_nt(w2, x1)
    scores = jax.nn.sigmoid(logits)
    sel = scores + br_ref[...]
    neg = -jnp.inf
    eid = lax.broadcasted_iota(I32, (E, tm), 0)
    lid = lax.broadcasted_iota(I32, (GS, tm), 0)

    grp = []
    for g in range(G):
        sg = sel[g * GS:(g + 1) * GS, :]
        m1, f1 = _first_max(sg, lid, GS)
        m2 = jnp.max(jnp.where(lid == f1, neg, sg), axis=0, keepdims=True)
        grp.append(m1 + m2)
    grp = jnp.concatenate(grp, axis=0)
    gmask = jnp.zeros((G, tm), jnp.bool_)
    for _ in range(TOPK_GROUPS):
        _, f = _first_max(grp, lid, G)
        hit = lid == f
        gmask = jnp.logical_or(gmask, hit)
        grp = jnp.where(hit, neg, grp)
    emask = jnp.concatenate(
        [jnp.broadcast_to(gmask[g:g + 1, :], (GS, tm)) for g in range(G)], axis=0)
    sel = jnp.where(emask, sel, neg)

    chosen = jnp.zeros((E, tm), jnp.bool_)
    ids, ws = [], []
    for _ in range(TOP_K):
        _, f = _first_max(sel, eid, E)
        hit = eid == f
        ids.append(f)
        ws.append(jnp.sum(jnp.where(hit, scores, 0.0), axis=0, keepdims=True))
        chosen = jnp.logical_or(chosen, hit)
        sel = jnp.where(hit, neg, sel)
    ids = jnp.concatenate(ids, axis=0)
    ws = jnp.concatenate(ws, axis=0)
    ws = ws / jnp.sum(ws, axis=0, keepdims=True) * ROUTED_SCALE

    cf = jnp.where(chosen, 1.0, 0.0)
    before = _dot(cf.astype(BF16), lower_ref[...]) + run_sc[...]
    ranks = [jnp.sum(jnp.where(eid == ids[k:k + 1, :], before, 0.0), axis=0, keepdims=True)
             for k in range(TOP_K)]
    run_sc[...] = run_sc[...] + jnp.sum(cf, axis=1, keepdims=True)
    idx_ref[...] = ids
    w_ref[...] = ws
    rank_ref[...] = jnp.concatenate(ranks, axis=0).astype(I32)
    cnt_ref[...] = jnp.broadcast_to(run_sc[...], cnt_ref.shape).astype(I32)


def moe_router(x, w_router, b_router, tm=512):
    T, D = x.shape
    tm = min(tm, T)
    ar = jnp.arange(tm)
    lower = (ar[:, None] < ar[None, :]).astype(BF16)
    out = pl.BlockSpec((TOP_K, tm), lambda i: (0, i))
    return pl.pallas_call(
        functools.partial(_router_body, tm=tm),
        grid=(T // tm,),
        in_specs=[pl.BlockSpec((tm, D), lambda i: (i, 0)),
                  pl.BlockSpec((N_EXPERTS, D), lambda i: (0, 0)),
                  pl.BlockSpec((N_EXPERTS, 1), lambda i: (0, 0)),
                  pl.BlockSpec((tm, tm), lambda i: (0, 0))],
        out_specs=[out, out, out, pl.BlockSpec((N_EXPERTS, LANES), lambda i: (0, 0))],
        out_shape=[jax.ShapeDtypeStruct((TOP_K, T), I32), jax.ShapeDtypeStruct((TOP_K, T), F32),
                   jax.ShapeDtypeStruct((TOP_K, T), I32),
                   jax.ShapeDtypeStruct((N_EXPERTS, LANES), I32)],
        scratch_shapes=[pltpu.VMEM((N_EXPERTS, 1), F32)],
        compiler_params=_cparams(("arbitrary",)),
        name="moe_router",
    )(x, w_router.T, b_router.reshape(N_EXPERTS, 1), lower)


HALF = D_MODEL // 2


def _pack_rows(x):
    lo = pltpu.bitcast(x[:, :HALF].astype(BF16).astype(F32), I32)
    hi = pltpu.bitcast(x[:, HALF:].astype(BF16).astype(F32), I32)
    return lax.shift_right_logical(lo, 16) | (hi & jnp.int32(-65536))


def _unpack_rows(w):
    return pltpu.bitcast(w << 16, F32), pltpu.bitcast(w & jnp.int32(-65536), F32)


def _dispatch_body(x_ref, dest_hbm, xs_in, xs_hbm, idx_sm, pack_sc, sem_idx, sem_row, *, tm, nt):
    del xs_in
    i = pl.program_id(0)
    slot = i % 2
    n_rows = tm * TOP_K

    def idx_copy(tile, s):
        return pltpu.make_async_copy(dest_hbm.at[tile], idx_sm.at[s], sem_idx.at[s])

    def row_copy(s, t, row):
        return pltpu.make_async_copy(pack_sc.at[s, pl.ds(t, 1)], xs_hbm.at[pl.ds(row, 1)], sem_row.at[s])

    @pl.when(i == 0)
    def _first():
        idx_copy(0, 0).start()

    @pl.when(i + 1 < nt)
    def _prefetch():
        idx_copy(i + 1, 1 - slot).start()

    idx_copy(i, slot).wait()
    pack_sc[slot] = _pack_rows(x_ref[...])

    for t in range(tm):
        for k in range(TOP_K):
            row_copy(slot, t, idx_sm[slot, t * TOP_K + k]).start()

    def drain(s):
        pltpu.make_async_copy(xs_hbm.at[pl.ds(0, n_rows)], xs_hbm.at[pl.ds(n_rows, n_rows)], sem_row.at[s]).wait()

    @pl.when(i > 0)
    def _drain_prev():
        drain(1 - slot)

    @pl.when(i == nt - 1)
    def _drain_last():
        drain(slot)


def moe_dispatch(x, dest_tiles, n_rows_out, tm):
    T, D = x.shape
    nt = T // tm
    xs0 = jnp.zeros((n_rows_out, HALF), I32)
    return pl.pallas_call(
        functools.partial(_dispatch_body, tm=tm, nt=nt),
        grid=(nt,),
        in_specs=[pl.BlockSpec((tm, D), lambda i: (i, 0)),
                  pl.BlockSpec(memory_space=pl.ANY),
                  pl.BlockSpec(memory_space=pl.ANY)],
        out_specs=pl.BlockSpec(memory_space=pl.ANY),
        out_shape=jax.ShapeDtypeStruct((n_rows_out, HALF), I32),
        scratch_shapes=[pltpu.SMEM((2, tm * TOP_K), I32), pltpu.VMEM((2, tm, HALF), I32),
                        pltpu.SemaphoreType.DMA((2,)), pltpu.SemaphoreType.DMA((2,))],
        input_output_aliases={2: 0},
        compiler_params=_cparams(("arbitrary",)),
        name="moe_dispatch",
    )(x, dest_tiles, xs0)


def _expert_body(be_ref, nreal_ref, x_ref, wgu_ref, wd_ref, o_ref):
    @pl.when(pl.program_id(0) < nreal_ref[0])
    def _():
        lo, hi = _unpack_rows(x_ref[...])
        gu = _dot(lo.astype(BF16), wgu_ref[0, :HALF, :]) + _dot(hi.astype(BF16), wgu_ref[0, HALF:, :])
        gt, up = gu[:, :D_EXPERT], gu[:, D_EXPERT:]
        h = gt * jax.nn.sigmoid(gt) * up
        o_ref[...] = _pack_rows(_dot(h.astype(BF16), wd_ref[0]))

    @pl.when(pl.program_id(0) >= nreal_ref[0])
    def _():
        o_ref[...] = jnp.zeros(o_ref.shape, I32)


def moe_experts(xs, block_exp, n_real, w_gu, w_down):
    P = xs.shape[0]
    D, M = D_MODEL, EXPERT_BLOCK
    grid_spec = pltpu.PrefetchScalarGridSpec(
        num_scalar_prefetch=2,
        grid=(P // M,),
        in_specs=[pl.BlockSpec((M, HALF), lambda i, be, nr: (i, 0)),
                  pl.BlockSpec((1, D, 2 * D_EXPERT), lambda i, be, nr: (be[i], 0, 0)),
                  pl.BlockSpec((1, D_EXPERT, D), lambda i, be, nr: (be[i], 0, 0))],
        out_specs=pl.BlockSpec((M, HALF), lambda i, be, nr: (i, 0)),
    )
    return pl.pallas_call(
        _expert_body,
        grid_spec=grid_spec,
        out_shape=jax.ShapeDtypeStruct((P, HALF), I32),
        compiler_params=_cparams(("arbitrary",)),
        name="moe_experts",
    )(block_exp, n_real, xs, w_gu, w_down)


def _ffn_tail_body(x_ref, dest_hbm, ys_hbm, w_ref, p_ref, wsg_ref, wsd_ref, g_ref, b_ref, wg_ref, bg_ref, wp_ref,
                   o_ref, idx_sm, y_sc, sem_idx, sem_row, *, tm, nt):
    i = pl.program_id(0)
    slot = i % 2
    n_rows = tm * TOP_K

    def idx_copy(tile, s):
        return pltpu.make_async_copy(dest_hbm.at[tile], idx_sm.at[s], sem_idx.at[s])

    def row_copy(s, k, t, row):
        return pltpu.make_async_copy(ys_hbm.at[pl.ds(row, 1)], y_sc.at[s, pl.ds(k * tm + t, 1)], sem_row.at[s])

    def gather(s):
        for t in range(tm):
            for k in range(TOP_K):
                row_copy(s, k, t, idx_sm[s, t * TOP_K + k]).start()

    @pl.when(i == 0)
    def _first():
        idx_copy(0, 0).start()
        idx_copy(0, 0).wait()
        gather(0)

        @pl.when(nt > 1)
        def _():
            idx_copy(1, 1).start()

    @pl.when(i + 1 < nt)
    def _next():
        idx_copy(i + 1, 1 - slot).wait()
        gather(1 - slot)

        @pl.when(i + 2 < nt)
        def _():
            idx_copy(i + 2, slot).start()

    x = x_ref[...]
    gu = _dot(x.astype(BF16), wsg_ref[...].astype(BF16))
    gs, us = gu[:, :D_SHARED], gu[:, D_SHARED:]
    shared = _dot((gs * jax.nn.sigmoid(gs) * us).astype(BF16), wsd_ref[...].astype(BF16))
    pe = _dot(p_ref[...].astype(BF16), wp_ref[...].astype(BF16))

    pltpu.make_async_copy(ys_hbm.at[pl.ds(0, n_rows)], y_sc.at[slot], sem_row.at[slot]).wait()

    r_lo = jnp.zeros((tm, HALF), F32)
    r_hi = jnp.zeros((tm, HALF), F32)
    for k in range(TOP_K):
        lo, hi = _unpack_rows(y_sc[slot, k * tm:(k + 1) * tm, :])
        wk = w_ref[:, k:k + 1]
        r_lo = r_lo + lo * wk
        r_hi = r_hi + hi * wk
    routed = jnp.concatenate([r_lo, r_hi], axis=1)
    x2 = _layer_norm(DEEPNORM_ALPHA * x + (routed + shared), g_ref[...], b_ref[...])
    gate = jax.nn.sigmoid(_dot(x2.astype(BF16), wg_ref[...].astype(BF16)) + bg_ref[...])
    o_ref[...] = x2 + gate * pe


def ffn_tail(x, dest_tiles, ys, wt, p, ws_gu, ws_down, g, b, w_gate, b_gate, w_proj, tm):
    T, D = x.shape
    nt = T // tm
    full = lambda shape: pl.BlockSpec(shape, lambda i: tuple(0 for _ in shape))
    return pl.pallas_call(
        functools.partial(_ffn_tail_body, tm=tm, nt=nt),
        grid=(nt,),
        in_specs=[pl.BlockSpec((tm, D), lambda i: (i, 0)),
                  pl.BlockSpec(memory_space=pl.ANY),
                  pl.BlockSpec(memory_space=pl.ANY),
                  pl.BlockSpec((tm, TOP_K), lambda i: (i, 0)),
                  pl.BlockSpec((tm, PLE_DIM), lambda i: (i, 0)),
                  full((D, 2 * D_SHARED)), full((D_SHARED, D)),
                  full((1, D)), full((1, D)), full((D, D)), full((1, D)), full((PLE_DIM, D))],
        out_specs=pl.BlockSpec((tm, D), lambda i: (i, 0)),
        out_shape=jax.ShapeDtypeStruct((T, D), F32),
        scratch_shapes=[pltpu.SMEM((2, tm * TOP_K), I32), pltpu.VMEM((2, TOP_K * tm, HALF), I32),
                        pltpu.SemaphoreType.DMA((2,)), pltpu.SemaphoreType.DMA((2,))],
        compiler_params=_cparams(("arbitrary",)),
        name="ffn_tail",
    )(x, dest_tiles, ys, wt, p, ws_gu, ws_down, g.reshape(1, D), b.reshape(1, D), w_gate,
      b_gate.reshape(1, D), w_proj)


def _pad_cols(w, n):
    return jnp.pad(w, ((0, 0), (0, n - w.shape[1])))


def _qk_weights(wq, wk):
    return jnp.concatenate([wq * HEAD_DIM ** -0.5, wk], axis=1)


def _fox_layer(x, w_in, b_f, t):
    B, S, D = x.shape
    xt = x.reshape(B * S, D)
    qk = matmul(xt, _qk_weights(w_in[:, :D], w_in[:, D:2 * D]), BF16).reshape(B, S, 2 * D)
    vt = matmul_t_blocks(xt, w_in[:, 2 * D:3 * D].T, B, t, BF16)
    f = matmul(xt, _pad_cols(w_in[:, 3 * D:], LANES), F32).reshape(B, S, LANES)
    caug, cbs = fox_gates(f, b_f, t)
    cb = cbs[:, :, 0, :N_HEADS].transpose(0, 2, 1).reshape(-1)
    return fox_attention(qk, vt, caug, cb, t).reshape(B * S, D)


def _sb_layer(x, w_in, t):
    B, S, D = x.shape
    xt = x.reshape(B * S, D)
    qk = matmul(xt, _qk_weights(w_in[:, :D], w_in[:, D:2 * D]), BF16).reshape(B, S, 2 * D)
    vt = matmul_t_blocks(xt, w_in[:, 2 * D:].T, B, t, BF16)
    return sb_attention(qk, vt, t).reshape(B * S, D)


def _rope_tables(positions):
    half = ROPE_DIM // 2
    inv_freq = jnp.exp(-math.log(ROPE_THETA) * 2.0 * jnp.arange(half, dtype=F32) / ROPE_DIM)
    ang = positions.astype(F32).reshape(-1, 1) * inv_freq
    cos, sin = jnp.cos(ang), jnp.sin(ang)
    T = ang.shape[0]
    rest = HEAD_DIM - ROPE_DIM
    cos64 = jnp.concatenate([cos, cos, jnp.ones((T, rest), F32)], axis=1)
    sa64 = jnp.concatenate([-sin, jnp.zeros((T, HEAD_DIM - half), F32)], axis=1)
    sb64 = jnp.concatenate([jnp.zeros((T, half), F32), sin, jnp.zeros((T, rest), F32)], axis=1)
    return tuple(jnp.tile(a, (1, LANES // HEAD_DIM)) for a in (cos64, sa64, sb64))


def _dsa_layer(x, positions, w_in, tq, tk):
    B, S, D = x.shape
    xt = x.reshape(B * S, D)
    dh = HEAD_DIM
    o_k, o_v, o_qi = D, D + dh, D + 2 * dh
    o_ki = o_qi + IDX_HEADS * IDX_DIM
    o_wi = o_ki + IDX_DIM
    wq, wk, wv = w_in[:, :o_k], w_in[:, o_k:o_v], w_in[:, o_v:o_qi]
    wqi, wki, wwi = w_in[:, o_qi:o_ki], w_in[:, o_ki:o_wi], w_in[:, o_wi:]
    w_rope = jnp.concatenate([wq * dh ** -0.5, wqi, wk, wk, wki, wki], axis=1)
    cos, sa, sb = _rope_tables(positions)
    proj = matmul_rope(xt, w_rope, cos, sa, sb, BF16).reshape(B, S, -1)
    vt = matmul_t_blocks(xt, jnp.concatenate([wv, wv], axis=1).T, B, tk, BF16).reshape(B, S // tk, LANES, tk)
    wit = matmul_t_rows(xt, _pad_cols(wwi * (IDX_HEADS ** -0.5 * IDX_DIM ** -0.5), LANES).T, B, F32)
    return dsa_attention(proj, vt, wit, tq, tk).reshape(B * S, D)


def _gla_layer(x, w_in, w_a2, b_a, norm_g):
    B, S, D = x.shape
    xt = x.reshape(B * S, D)
    W = 2 * GLA_HEADS * GLA_DK + 2 * GLA_HEADS * GLA_DV
    qkvg = matmul(xt, w_in[:, :W], F32).reshape(B, S, W)
    a1 = matmul(xt, _pad_cols(w_in[:, W:], LANES), F32).reshape(B, S, LANES)
    w_a2p = jnp.pad(w_a2, ((0, LANES - GLA_GATE_RANK), (0, 0)))
    return gla_attention(qkvg, a1, w_a2p, b_a, norm_g).reshape(B * S, GLA_HEADS * GLA_DV)


MOE_TOKEN_TILE = 256


def _moe_layer(x, p_i, w_router, b_router, w_gu, w_down, ws_gu, ws_down, g, b, w_gate, b_gate, w_proj):
    T, D = x.shape
    E, M = N_EXPERTS, EXPERT_BLOCK
    tm = min(MOE_TOKEN_TILE, T)
    idx, wts, rank, cnt = moe_router(x, w_router, b_router)
    counts = cnt[:, 0]
    padded = (counts + M - 1) // M * M
    pend = jnp.cumsum(padded)
    poff = pend - padded
    eids = jnp.arange(E, dtype=I32)[:, None, None]
    dest = rank + jnp.sum(jnp.where(idx[None] == eids, poff.astype(I32)[:, None, None], 0), axis=0)
    dest_tiles = dest.T.reshape(T // tm, tm * TOP_K)
    P = (T * TOP_K + M - 1) // M * M + E * M
    nb = P // M
    block_exp = jnp.clip(jnp.searchsorted(pend, jnp.arange(nb, dtype=I32) * M, side="right"), 0, E - 1).astype(I32)
    n_real = (pend[-1] // M).astype(I32).reshape(1)
    xs = moe_dispatch(x, dest_tiles, P, tm)
    ys = moe_experts(xs, block_exp, n_real, w_gu, w_down)
    return ffn_tail(x, dest_tiles, ys, wts.T, p_i, ws_gu, ws_down, g, b, w_gate, b_gate, w_proj, tm)


def kernel(x, p, positions, ln1_g, ln1_b, ln2_g, ln2_b, fox_w_in, fox_b_f, fox_w_out, sb_w_in, sb_w_out, dsa_w_in, dsa_w_out, gla_w_in, gla_w_a2, gla_b_a, gla_norm_g, gla_w_out, moe_w_router, moe_b_router, moe_w_gu, moe_w_down, shared_w_gu, shared_w_down, ple_w_proj, ple_w_gate, ple_b_gate):
    B, S, D = x.shape
    T = B * S
    depth = p.shape[0]
    t_attn = min(512, S)
    w_gu_bf = moe_w_gu.astype(BF16)
    w_down_bf = moe_w_down.astype(BF16)
    xt = x.reshape(T, D)
    for i in range(depth):
        m, j = i % 4, i // 4
        xb = xt.reshape(B, S, D)
        if m == 0:
            a, w_out = _fox_layer(xb, fox_w_in[j], fox_b_f[j], t_attn), fox_w_out[j]
        elif m == 1:
            a, w_out = _sb_layer(xb, sb_w_in[j], t_attn), sb_w_out[j]
        elif m == 2:
            a, w_out = _dsa_layer(xb, positions, dsa_w_in[j], min(256, S), min(512, S)), dsa_w_out[j]
        else:
            a, w_out = _gla_layer(xb, gla_w_in[j], gla_w_a2[j], gla_b_a[j], gla_norm_g[j]), gla_w_out[j]
        xt = matmul_residual_ln(a, w_out, xt, ln1_g[i], ln1_b[i])
        xt = _moe_layer(xt, p[i].reshape(T, -1), moe_w_router[i], moe_b_router[i], w_gu_bf[i], w_down_bf[i],
                        shared_w_gu[i], shared_w_down[i], ln2_g[i], ln2_b[i],
                        ple_w_gate[i], ple_b_gate[i], ple_w_proj[i])
    return xt.reshape(B, S, D)
```

```python
import functools
import math

import jax
import jax.numpy as jnp
from jax import lax
from jax.experimental import pallas as pl
from jax.experimental.pallas import tpu as pltpu

F32 = jnp.float32
BF16 = jnp.bfloat16
I32 = jnp.int32

D_MODEL = 1024
HEAD_DIM = 64
N_HEADS = D_MODEL // HEAD_DIM
N_PAIRS = N_HEADS // 2
LANES = 128
ROPE_DIM = HEAD_DIM // 4
ROPE_THETA = 500000.0
IDX_HEADS = 8
IDX_DIM = HEAD_DIM
IDX_TOPK_MAX = 256
GLA_HEADS = 4
GLA_DK = D_MODEL // 2 // GLA_HEADS
GLA_DV = D_MODEL // GLA_HEADS
GLA_GATE_RANK = 16
GLA_TAU = 16.0
GLA_CHUNK = 64
N_EXPERTS = 64
TOP_K = 8
N_GROUPS = 8
GROUP_SIZE = N_EXPERTS // N_GROUPS
TOPK_GROUPS = 4
D_EXPERT = 256
D_SHARED = 256
ROUTED_SCALE = 2.5
EXPERT_BLOCK = 512
PLE_DIM = 256
DEPTH = 4
DEEPNORM_ALPHA = (2 * DEPTH) ** 0.25
LN_EPS = 1e-5
RMS_EPS = 1e-6
INT_MIN = -(2 ** 31)
LOG2E = 1.4426950408889634
SB_SUB = 256
VMEM_LIMIT = 56 * 1024 * 1024


def _cparams(sem):
    return pltpu.CompilerParams(dimension_semantics=sem, vmem_limit_bytes=VMEM_LIMIT)


def _pick(n, cands):
    for c in cands:
        if n % c == 0:
            return c
    raise ValueError(f"no tile for {n}")


def _dot(a, b):
    return jnp.dot(a, b, preferred_element_type=F32)


def _dot_nt(a, b):
    return lax.dot_general(a, b, (((1,), (1,)), ((), ())), preferred_element_type=F32)


def _dot_tn(a, b):
    return lax.dot_general(a, b, (((0,), (0,)), ((), ())), preferred_element_type=F32)


def _split3(x):
    h1 = x.astype(BF16)
    r1 = x - h1.astype(F32)
    h2 = r1.astype(BF16)
    h3 = (r1 - h2.astype(F32)).astype(BF16)
    return h1, h2, h3


def _log_sigmoid(x):
    return jnp.minimum(x, 0.0) - jnp.log1p(jnp.exp(-jnp.abs(x)))


def _mm_body(x_ref, w_ref, o_ref):
    o_ref[...] = _dot(x_ref[...].astype(BF16), w_ref[...].astype(BF16)).astype(o_ref.dtype)


def matmul(x, w, out_dtype, tm=512):
    M, K = x.shape
    N = w.shape[1]
    tm = min(tm, M)
    tn = _pick(N, (512, 384, 256, 128))
    return pl.pallas_call(
        _mm_body,
        grid=(M // tm, N // tn),
        in_specs=[pl.BlockSpec((tm, K), lambda i, j: (i, 0)),
                  pl.BlockSpec((K, tn), lambda i, j: (0, j))],
        out_specs=pl.BlockSpec((tm, tn), lambda i, j: (i, j)),
        out_shape=jax.ShapeDtypeStruct((M, N), out_dtype),
        compiler_params=_cparams(("parallel", "parallel")),
        name="matmul",
    )(x, w)


def _mm_rope_body(x_ref, w_ref, cos_ref, sa_ref, sb_ref, o_ref):
    y = _dot(x_ref[...].astype(BF16), w_ref[...].astype(BF16))
    cos, sa, sb = cos_ref[...], sa_ref[...], sb_ref[...]
    for c in range(y.shape[1] // LANES):
        yc = y[:, c * LANES:(c + 1) * LANES]
        oc = yc * cos + pltpu.roll(yc, LANES - ROPE_DIM // 2, 1) * sa + pltpu.roll(yc, ROPE_DIM // 2, 1) * sb
        o_ref[:, c * LANES:(c + 1) * LANES] = oc.astype(o_ref.dtype)


def matmul_rope(x, w, cos, sa, sb, out_dtype, tm=512):
    M, K = x.shape
    N = w.shape[1]
    tm = min(tm, M)
    tn = _pick(N, (256, 128))
    tab = pl.BlockSpec((tm, LANES), lambda i, j: (i, 0))
    return pl.pallas_call(
        _mm_rope_body,
        grid=(M // tm, N // tn),
        in_specs=[pl.BlockSpec((tm, K), lambda i, j: (i, 0)),
                  pl.BlockSpec((K, tn), lambda i, j: (0, j)), tab, tab, tab],
        out_specs=pl.BlockSpec((tm, tn), lambda i, j: (i, j)),
        out_shape=jax.ShapeDtypeStruct((M, N), out_dtype),
        compiler_params=_cparams(("parallel", "parallel")),
        name="matmul_rope",
    )(x, w, cos, sa, sb)


def _layer_norm(y, g, b):
    mu = jnp.mean(y, axis=-1, keepdims=True)
    d = y - mu
    var = jnp.mean(d * d, axis=-1, keepdims=True)
    return d * lax.rsqrt(var + LN_EPS) * g + b


def _mm_res_ln_body(a_ref, w_ref, x_ref, g_ref, b_ref, o_ref):
    h = _dot(a_ref[...].astype(BF16), w_ref[...].astype(BF16))
    o_ref[...] = _layer_norm(DEEPNORM_ALPHA * x_ref[...] + h, g_ref[...], b_ref[...])


def matmul_residual_ln(a, w, x, g, b, tm=512):
    M, K = a.shape
    N = w.shape[1]
    tm = min(tm, M)
    return pl.pallas_call(
        _mm_res_ln_body,
        grid=(M // tm,),
        in_specs=[pl.BlockSpec((tm, K), lambda i: (i, 0)),
                  pl.BlockSpec((K, N), lambda i: (0, 0)),
                  pl.BlockSpec((tm, N), lambda i: (i, 0)),
                  pl.BlockSpec((1, N), lambda i: (0, 0)),
                  pl.BlockSpec((1, N), lambda i: (0, 0))],
        out_specs=pl.BlockSpec((tm, N), lambda i: (i, 0)),
        out_shape=jax.ShapeDtypeStruct((M, N), F32),
        compiler_params=_cparams(("parallel",)),
        name="matmul_residual_ln",
    )(a, w, x, g.reshape(1, N), b.reshape(1, N))


def _mm_t_blocks_body(wt_ref, x_ref, o_ref):
    o_ref[0, 0, 0] = _dot_nt(wt_ref[...].astype(BF16), x_ref[...].astype(BF16)).astype(o_ref.dtype)


def _mm_t_rows_body(wt_ref, x_ref, o_ref):
    o_ref[0] = _dot_nt(wt_ref[...].astype(BF16), x_ref[...].astype(BF16)).astype(o_ref.dtype)


def matmul_t_blocks(xt, wt, B, t, out_dtype):
    T, K = xt.shape
    nk = T // B // t
    nj = wt.shape[0] // LANES
    return pl.pallas_call(
        _mm_t_blocks_body,
        grid=(B, nk, nj),
        in_specs=[pl.BlockSpec((LANES, K), lambda b, i, j: (j, 0)),
                  pl.BlockSpec((t, K), lambda b, i, j: (b * nk + i, 0))],
        out_specs=pl.BlockSpec((1, 1, 1, LANES, t), lambda b, i, j: (b, j, i, 0, 0)),
        out_shape=jax.ShapeDtypeStruct((B, nj, nk, LANES, t), out_dtype),
        compiler_params=_cparams(("parallel", "parallel", "parallel")),
        name="matmul_t_blocks",
    )(wt, xt)


def matmul_t_rows(xt, wt, B, out_dtype, tm=512):
    T, K = xt.shape
    S = T // B
    tm = min(tm, S)
    ns = S // tm
    return pl.pallas_call(
        _mm_t_rows_body,
        grid=(B, ns),
        in_specs=[pl.BlockSpec((LANES, K), lambda b, i: (0, 0)),
                  pl.BlockSpec((tm, K), lambda b, i: (b * ns + i, 0))],
        out_specs=pl.BlockSpec((1, LANES, tm), lambda b, i: (b, 0, i)),
        out_shape=jax.ShapeDtypeStruct((B, LANES, S), out_dtype),
        compiler_params=_cparams(("parallel", "parallel")),
        name="matmul_t_rows",
    )(wt, xt)


def _fox_gate_body(f_ref, bf_ref, tril_ref, place_ref, caug_ref, cb_ref, carry_sc, *, t):
    @pl.when(pl.program_id(1) == 0)
    def _reset():
        carry_sc[...] = jnp.zeros(carry_sc.shape, F32)

    log_f = _log_sigmoid(f_ref[0] + bf_ref[...])
    tril = tril_ref[...]
    h1, h2, h3 = _split3(log_f)
    c = _dot(tril, h1) + _dot(tril, h2) + _dot(tril, h3) + carry_sc[...]
    carry_sc[...] = c[t - 1:t, :]
    c = c * LOG2E
    c0 = c[0:1, :]
    d1, d2, d3 = _split3(c0 - c)
    caug = _dot(d1, place_ref[0]) + _dot(d2, place_ref[1]) + _dot(d3, place_ref[2])
    caug_ref[0] = caug.astype(BF16)
    cb_ref[0, 0] = c0


def fox_gates(f, b_f, t):
    B, S, _ = f.shape
    nk = S // t
    ar = jnp.arange(t)
    tril = (ar[None, :] <= ar[:, None]).astype(BF16)
    hh = jnp.arange(LANES)
    place = jnp.stack([((hh[None, :] == 3 * hh[:, None] + j) & (hh[:, None] < N_HEADS)).astype(BF16)
                       for j in range(3)])
    return pl.pallas_call(
        functools.partial(_fox_gate_body, t=t),
        grid=(B, nk),
        in_specs=[pl.BlockSpec((1, t, LANES), lambda b, i: (b, i, 0)),
                  pl.BlockSpec((1, LANES), lambda b, i: (0, 0)),
                  pl.BlockSpec((t, t), lambda b, i: (0, 0)),
                  pl.BlockSpec((3, LANES, LANES), lambda b, i: (0, 0, 0))],
        out_specs=[pl.BlockSpec((1, t, LANES), lambda b, i: (b, i, 0)),
                   pl.BlockSpec((1, 1, 1, LANES), lambda b, i: (b, i, 0, 0))],
        out_shape=[jax.ShapeDtypeStruct((B, S, LANES), BF16),
                   jax.ShapeDtypeStruct((B, nk, 1, LANES), F32)],
        scratch_shapes=[pltpu.VMEM((1, LANES), F32)],
        compiler_params=_cparams(("parallel", "arbitrary")),
        name="fox_gates",
    )(f, _pad_cols(b_f.reshape(1, -1), LANES), tril, place)


def _pair_halves(q2):
    lane = lax.broadcasted_iota(I32, q2.shape, 1)
    lo = lane < HEAD_DIM
    zero = jnp.zeros_like(q2)
    return lane, (jnp.where(lo, q2, zero), jnp.where(lo, zero, q2))


def _merge_heads_t(o_a, o_b):
    sub = lax.broadcasted_iota(I32, o_a.shape, 0)
    return jnp.transpose(jnp.where(sub < HEAD_DIM, o_a, o_b))


def _fox_body(cb_ref, q_ref, k_ref, vt_ref, caug_ref, o_ref, s_sc, m_sc, l_sc, acc_sc, *, t, nk):
    b, p, qb = pl.program_id(0), pl.program_id(1), pl.program_id(2)
    q2 = q_ref[0]
    lane, halves = _pair_halves(q2)
    rows = []
    for h in range(2):
        first = 3 * (2 * p + h)
        pick = jnp.logical_and(lane >= first, lane < first + 3)
        ones = jnp.where(pick, 1.0, 0.0).astype(BF16)
        rows.append(jnp.concatenate([halves[h], ones], axis=1))
    q_both = jnp.concatenate(rows, axis=0)
    m_sc[...] = jnp.full(m_sc.shape, -jnp.inf, F32)
    l_sc[...] = jnp.zeros(l_sc.shape, F32)
    acc_sc[...] = jnp.zeros(acc_sc.shape, F32)
    bases = [(b * N_HEADS + 2 * p + h) * nk for h in range(2)]

    def scores(kb):
        start = pl.multiple_of(kb * t, t)
        k_aug = jnp.concatenate([k_ref[0, pl.ds(start, t), :], caug_ref[0, pl.ds(start, t), :]], axis=1)
        return _dot_nt(k_aug, q_both)

    def softmax_step(kb, s, masked):
        if masked:
            kid = lax.broadcasted_iota(I32, (t, 2 * t), 0)
            qid = lax.broadcasted_iota(I32, (t, 2 * t), 1)
            qid = jnp.where(qid >= t, qid - t, qid)
            s = jnp.where(kid <= qid, s, -jnp.inf)
        off = jnp.concatenate([jnp.full((1, t), cb_ref[bs + qb] - cb_ref[bs + kb], F32) for bs in bases], axis=1)
        m_old = m_sc[...]
        m_new = jnp.maximum(m_old, jnp.max(s, axis=0, keepdims=True) + off)
        alpha = jnp.exp2(m_old - m_new)
        pr = jnp.exp2(s - (m_new - off))
        l_sc[...] = alpha * l_sc[...] + jnp.sum(pr, axis=0, keepdims=True)
        acc_sc[...] = alpha * acc_sc[...] + _dot(vt_ref[0, 0, kb], pr.astype(BF16))
        m_sc[...] = m_new

    s_sc[0] = scores(0)

    def full_block(kb, carry):
        s = s_sc[kb % 2]
        s_sc[(kb + 1) % 2] = scores(kb + 1)
        softmax_step(kb, s, False)
        return carry

    lax.fori_loop(0, qb, full_block, 0)
    softmax_step(qb, s_sc[qb % 2], True)
    o = acc_sc[...] / l_sc[...]
    o_ref[0] = _merge_heads_t(o[:, :t], o[:, t:]).astype(o_ref.dtype)


def fox_attention(qk, vt, caug, cb, t):
    B, S, _ = qk.shape
    nk = S // t
    grid_spec = pltpu.PrefetchScalarGridSpec(
        num_scalar_prefetch=1,
        grid=(B, N_PAIRS, nk),
        in_specs=[pl.BlockSpec((1, t, LANES), lambda b, p, i, cb: (b, i, p)),
                  pl.BlockSpec((1, S, LANES), lambda b, p, i, cb: (b, 0, N_PAIRS + p)),
                  pl.BlockSpec((1, 1, nk, LANES, t), lambda b, p, i, cb: (b, p, 0, 0, 0)),
                  pl.BlockSpec((1, S, LANES), lambda b, p, i, cb: (b, 0, 0))],
        out_specs=pl.BlockSpec((1, t, LANES), lambda b, p, i, cb: (b, i, p)),
        scratch_shapes=[pltpu.VMEM((2, t, 2 * t), F32), pltpu.VMEM((1, 2 * t), F32), pltpu.VMEM((1, 2 * t), F32),
                        pltpu.VMEM((LANES, 2 * t), F32)],
    )
    return pl.pallas_call(
        functools.partial(_fox_body, t=t, nk=nk),
        grid_spec=grid_spec,
        out_shape=jax.ShapeDtypeStruct((B, S, D_MODEL), BF16),
        compiler_params=_cparams(("parallel", "parallel", "arbitrary")),
        name="fox_attention",
    )(cb, qk, qk, vt, caug)


def _sb_body(q_ref, k_ref, vt_ref, upper_ref, o_ref, after_sc, acc_sc, *, t):
    qb = pl.program_id(2)
    _, halves = _pair_halves(q_ref[0])
    q_both = jnp.concatenate(halves, axis=0)
    after_sc[...] = jnp.zeros(after_sc.shape, F32)
    acc_sc[...] = jnp.zeros(acc_sc.shape, F32)
    neg_upper = upper_ref[...]

    def scores(kb):
        start = pl.multiple_of(kb * t, t)
        return _dot_nt(k_ref[0, pl.ds(start, t), :], q_both)

    def weigh(kb, z, masked):
        sp = jnp.maximum(z, 0.0) + jnp.log2(1.0 + jnp.exp2(-jnp.abs(z)))
        log_beta = z - sp
        if masked:
            kid = lax.broadcasted_iota(I32, (t, 2 * t), 0)
            qid = lax.broadcasted_iota(I32, (t, 2 * t), 1)
            strict = kid < jnp.where(qid >= t, qid - t, qid)
            sp = jnp.where(strict, sp, 0.0)
        spb = sp.astype(BF16)
        sub = neg_upper.shape[0]
        nsub = t // sub
        sums = [jnp.sum(sp[j * sub:(j + 1) * sub, :], axis=0, keepdims=True) for j in range(nsub)]
        later = after_sc[...]
        rests = [None] * nsub
        for j in reversed(range(nsub)):
            rests[j] = _dot(neg_upper, spb[j * sub:(j + 1) * sub, :]) + later
            later = later - sums[j]
        a = jnp.exp2(log_beta + jnp.concatenate(rests, axis=0))
        if masked:
            a = jnp.where(strict, a, 0.0)
        acc_sc[...] = acc_sc[...] + _dot(vt_ref[0, 0, kb], a.astype(BF16))
        after_sc[...] = later

    weigh(qb, scores(qb), True)

    def step(j, carry):
        kb = qb - 1 - j
        weigh(kb, scores(kb), False)
        return carry

    lax.fori_loop(0, qb, step, 0)
    o = acc_sc[...]
    o_ref[0] = _merge_heads_t(o[:, :t], o[:, t:]).astype(o_ref.dtype)


def sb_attention(qk, vt, t):
    B, S, _ = qk.shape
    nk = S // t
    ar = jnp.arange(min(SB_SUB, t))
    upper = -(ar[None, :] > ar[:, None]).astype(BF16)
    return pl.pallas_call(
        functools.partial(_sb_body, t=t),
        grid=(B, N_PAIRS, nk),
        in_specs=[pl.BlockSpec((1, t, LANES), lambda b, p, i: (b, i, p)),
                  pl.BlockSpec((1, S, LANES), lambda b, p, i: (b, 0, N_PAIRS + p)),
                  pl.BlockSpec((1, 1, nk, LANES, t), lambda b, p, i: (b, p, 0, 0, 0)),
                  pl.BlockSpec(upper.shape, lambda b, p, i: (0, 0))],
        out_specs=pl.BlockSpec((1, t, LANES), lambda b, p, i: (b, i, p)),
        out_shape=jax.ShapeDtypeStruct((B, S, D_MODEL), BF16),
        scratch_shapes=[pltpu.VMEM((1, 2 * t), F32), pltpu.VMEM((LANES, 2 * t), F32)],
        compiler_params=_cparams(("parallel", "parallel", "arbitrary")),
        name="sb_attention",
    )(qk, qk, vt, upper)


def _sortable_key(x):
    bits = pltpu.bitcast(x, I32)
    return bits ^ ((bits >> 31) & jnp.int32(0x7FFFFFFF))


def _dsa_body(q_ref, qi_ref, k_ref, ki_ref, vt_ref, wi_ref, lower_ref, o_ref,
              keys_sc, s_sc, m_sc, l_sc, acc_sc, *, tq, tk, topk):
    qb = pl.program_id(1)
    p = pl.program_id(2)
    nch = (qb * tq) // tk + 1
    kid = lax.broadcasted_iota(I32, (tk, tq), 0)
    qpos = qb * tq + lax.broadcasted_iota(I32, (tk, tq), 1)

    @pl.when(p == 0)
    def _select():
        def score_chunk(c, carry):
            start = pl.multiple_of(c * tk, tk)
            kidup = ki_ref[0, pl.ds(start, tk), :]
            score = jnp.zeros((tk, tq), F32)
            for hp in range(IDX_HEADS // 2):
                _, halves = _pair_halves(qi_ref[0, :, hp * LANES:(hp + 1) * LANES])
                for j in range(2):
                    h = 2 * hp + j
                    score = score + jnp.maximum(_dot_nt(kidup, halves[j]), 0.0) * wi_ref[0, h:h + 1, :]
            keys_sc[c] = jnp.where(c * tk + kid <= qpos, _sortable_key(score), jnp.int32(INT_MIN))
            return carry

        lax.fori_loop(0, nch, score_chunk, 0)

        def count_ge(thr):
            def body(c, cnt):
                kc = keys_sc[c]
                for j in range(tk // 8):
                    cnt = cnt + jnp.where(kc[j * 8:(j + 1) * 8, :] >= thr, 1, 0)
                return cnt
            cnt = lax.fori_loop(0, nch, body, jnp.zeros((8, tq), I32))
            return jnp.sum(cnt, axis=0, keepdims=True)

        def bit_step(i, thr):
            cand = thr + jnp.left_shift(jnp.int32(1), 31 - i)
            return jnp.where(count_ge(cand) >= topk, cand, thr)

        thr = lax.fori_loop(0, 32, bit_step, jnp.full((1, tq), INT_MIN, I32))
        n_ge = count_ge(thr)
        n_gt = count_ge(thr + 1)
        need = topk - n_gt
        tie = jnp.logical_and(n_ge - n_gt > need, thr > INT_MIN)
        any_tie = jnp.max(jnp.where(tie, 1, 0))

        def store_bias(c, sel):
            keys_sc[c] = pltpu.bitcast(jnp.where(sel, 0.0, -1e30).astype(F32), I32)

        @pl.when(any_tie == 0)
        def _no_ties():
            def body(c, carry):
                kc = keys_sc[c]
                store_bias(c, jnp.logical_and(kc >= thr, kc > INT_MIN))
                return carry
            lax.fori_loop(0, nch, body, 0)

        @pl.when(any_tie != 0)
        def _ties():
            needf = need.astype(F32)

            def body(c, seen):
                kc = keys_sc[c]
                eq = kc == thr
                eqf = jnp.where(eq, 1.0, 0.0)
                rank = _dot(lower_ref[...], eqf.astype(BF16)) + seen
                sel = jnp.logical_or(kc > thr, jnp.logical_and(eq, rank < needf))
                store_bias(c, jnp.logical_and(sel, kc > INT_MIN))
                return seen + jnp.sum(eqf, axis=0, keepdims=True)
            lax.fori_loop(0, nch, body, jnp.zeros((1, tq), F32))

    _, halves = _pair_halves(q_ref[0])
    q_both = jnp.concatenate(halves, axis=0)
    m_sc[...] = jnp.full(m_sc.shape, -1e30, F32)
    l_sc[...] = jnp.zeros(l_sc.shape, F32)
    acc_sc[...] = jnp.zeros(acc_sc.shape, F32)

    def scores(c):
        start = pl.multiple_of(c * tk, tk)
        return _dot_nt(k_ref[0, pl.ds(start, tk), :], q_both)

    def softmax_step(c, s_raw):
        bias = pltpu.bitcast(keys_sc[c], F32)
        s = s_raw + jnp.concatenate([bias, bias], axis=1)
        m_old = m_sc[...]
        m_new = jnp.maximum(m_old, jnp.max(s, axis=0, keepdims=True))
        alpha = jnp.exp2(m_old - m_new)
        pr = jnp.exp2(s - m_new)
        l_sc[...] = alpha * l_sc[...] + jnp.sum(pr, axis=0, keepdims=True)
        acc_sc[...] = alpha * acc_sc[...] + _dot(vt_ref[0, c], pr.astype(BF16))
        m_sc[...] = m_new

    s_sc[0] = scores(0)

    def attend(c, carry):
        s_raw = s_sc[c % 2]
        s_sc[(c + 1) % 2] = scores(c + 1)
        softmax_step(c, s_raw)
        return carry

    lax.fori_loop(0, nch - 1, attend, 0)
    softmax_step(nch - 1, s_sc[(nch - 1) % 2])
    o = acc_sc[...] / l_sc[...]
    o_ref[0] = _merge_heads_t(o[:, :tq], o[:, tq:]).astype(o_ref.dtype)


def dsa_attention(proj, vt, wit, tq, tk):
    B, S, _ = proj.shape
    topk = min(IDX_TOPK_MAX, S // 4)
    ar = jnp.arange(tk)
    lower = (ar[None, :] < ar[:, None]).astype(BF16)
    kern = functools.partial(_dsa_body, tq=tq, tk=tk, topk=topk)
    qi_blk = D_MODEL // (IDX_HEADS * IDX_DIM)
    k_blk = (D_MODEL + IDX_HEADS * IDX_DIM) // LANES
    nk = S // tk
    return pl.pallas_call(
        kern,
        grid=(B, S // tq, N_PAIRS),
        in_specs=[pl.BlockSpec((1, tq, LANES), lambda b, i, p: (b, i, p)),
                  pl.BlockSpec((1, tq, IDX_HEADS * IDX_DIM), lambda b, i, p: (b, i, qi_blk)),
                  pl.BlockSpec((1, S, LANES), lambda b, i, p: (b, 0, k_blk)),
                  pl.BlockSpec((1, S, LANES), lambda b, i, p: (b, 0, k_blk + 1)),
                  pl.BlockSpec((1, nk, LANES, tk), lambda b, i, p: (b, 0, 0, 0)),
                  pl.BlockSpec((1, LANES, tq), lambda b, i, p: (b, 0, i)),
                  pl.BlockSpec((tk, tk), lambda b, i, p: (0, 0))],
        out_specs=pl.BlockSpec((1, tq, LANES), lambda b, i, p: (b, i, p)),
        out_shape=jax.ShapeDtypeStruct((B, S, D_MODEL), BF16),
        scratch_shapes=[pltpu.VMEM((nk, tk, tq), I32), pltpu.VMEM((2, tk, 2 * tq), F32),
                        pltpu.VMEM((1, 2 * tq), F32), pltpu.VMEM((1, 2 * tq), F32),
                        pltpu.VMEM((LANES, 2 * tq), F32)],
        compiler_params=_cparams(("parallel", "arbitrary", "arbitrary")),
        name="dsa_attention",
    )(proj, proj, proj, proj, vt, wit, lower)


def _gla_body(qkvg_ref, a1_ref, wa2_ref, ba_ref, ng_ref, o_ref, state_sc, *, ts):
    C, H, dk, dv = GLA_CHUNK, GLA_HEADS, GLA_DK, GLA_DV

    @pl.when(pl.program_id(1) == 0)
    def _reset():
        state_sc[...] = jnp.zeros(state_sc.shape, F32)

    row = lax.broadcasted_iota(I32, (C, C), 0)
    colc = lax.broadcasted_iota(I32, (C, C), 1)
    causal = colc <= row
    tril = jnp.where(causal, 1.0, 0.0).astype(BF16)
    wa2 = wa2_ref[...].astype(BF16)
    ng = ng_ref[...]
    v_off, g_off = 2 * H * dk, 2 * H * dk + H * dv
    for n in range(ts // C):
        rows = slice(n * C, (n + 1) * C)
        za = _dot(a1_ref[0, rows, :].astype(BF16), wa2) + ba_ref[...]
        log_a = _log_sigmoid(za) / GLA_TAU
        h1, h2, h3 = _split3(log_a)
        bcum = _dot(tril, h1) + _dot(tril, h2) + _dot(tril, h3)
        eb = jnp.exp(bcum)
        enb = jnp.exp(-bcum)
        b_last = bcum[C - 1:C, :]
        eout = jnp.exp(b_last - bcum)
        dec = jnp.exp(b_last)
        for h in range(H):
            ks = slice(h * dk, (h + 1) * dk)
            q = qkvg_ref[0, rows, h * dk:(h + 1) * dk] * (dk ** -0.5)
            k = qkvg_ref[0, rows, H * dk + h * dk:H * dk + (h + 1) * dk]
            v = qkvg_ref[0, rows, v_off + h * dv:v_off + (h + 1) * dv].astype(BF16)
            q_in = (q * eb[:, ks]).astype(BF16)
            k_in = (k * enb[:, ks]).astype(BF16)
            k_out = (k * eout[:, ks]).astype(BF16)
            att = jnp.where(causal, _dot_nt(q_in, k_in), 0.0)
            state = state_sc[h]
            o = _dot(att.astype(BF16), v) + _dot(q_in, state.astype(BF16))
            dec_col = jnp.transpose(jnp.broadcast_to(dec[:, ks], (dk, dk)))[:, 0:1]
            state_sc[h] = state * dec_col + _dot_tn(k_out, v)
            o = o * lax.rsqrt(jnp.mean(o * o, axis=-1, keepdims=True) + RMS_EPS) * ng
            g = qkvg_ref[0, rows, g_off + h * dv:g_off + (h + 1) * dv]
            o = o * (g * jax.nn.sigmoid(g))
            o_ref[0, rows, h * dv:(h + 1) * dv] = o.astype(o_ref.dtype)


def gla_attention(qkvg, a1, w_a2p, b_a, norm_g, ts=512):
    B, S, W = qkvg.shape
    ts = min(ts, S)
    HK = GLA_HEADS * GLA_DK
    kern = functools.partial(_gla_body, ts=ts)
    return pl.pallas_call(
        kern,
        grid=(B, S // ts),
        in_specs=[pl.BlockSpec((1, ts, W), lambda b, i: (b, i, 0)),
                  pl.BlockSpec((1, ts, LANES), lambda b, i: (b, i, 0)),
                  pl.BlockSpec((LANES, HK), lambda b, i: (0, 0)),
                  pl.BlockSpec((1, HK), lambda b, i: (0, 0)),
                  pl.BlockSpec((1, GLA_DV), lambda b, i: (0, 0))],
        out_specs=pl.BlockSpec((1, ts, GLA_HEADS * GLA_DV), lambda b, i: (b, i, 0)),
        out_shape=jax.ShapeDtypeStruct((B, S, GLA_HEADS * GLA_DV), BF16),
        scratch_shapes=[pltpu.VMEM((GLA_HEADS, GLA_DK, GLA_DV), F32)],
        compiler_params=_cparams(("parallel", "arbitrary")),
        name="gla_attention",
    )(qkvg, a1, w_a2p, b_a.reshape(1, HK), norm_g.reshape(1, GLA_DV))


def _first_max(vals, ids, big):
    m = jnp.max(vals, axis=0, keepdims=True)
    first = jnp.min(jnp.where(vals == m, ids, big), axis=0, keepdims=True)
    return m, first


def _router_body(x_ref, wr_ref, br_ref, lower_ref, idx_ref, w_ref, rank_ref, cnt_ref, run_sc, *, tm):
    E, G, GS = N_EXPERTS, N_GROUPS, GROUP_SIZE

    @pl.when(pl.program_id(0) == 0)
    def _reset():
        run_sc[...] = jnp.zeros(run_sc.shape, F32)

    x = x_ref[...]
    x1 = x.astype(BF16)
    x2 = (x - x1.astype(F32)).astype(BF16)
    wr = wr_ref[...]
    w1 = wr.astype(BF16)
    w2 = (wr - w1.astype(F32)).astype(BF16)
    logits = _dot_nt(w1, x1) + _dot_nt(w1, x2) + _dot_nt(w2, x1)
    scores = jax.nn.sigmoid(logits)
    sel = scores + br_ref[...]
    neg = -jnp.inf
    eid = lax.broadcasted_iota(I32, (E, tm), 0)
    lid = lax.broadcasted_iota(I32, (GS, tm), 0)

    grp = []
    for g in range(G):
        sg = sel[g * GS:(g + 1) * GS, :]
        m1, f1 = _first_max(sg, lid, GS)
        m2 = jnp.max(jnp.where(lid == f1, neg, sg), axis=0, keepdims=True)
        grp.append(m1 + m2)
    grp = jnp.concatenate(grp, axis=0)
    gmask = jnp.zeros((G, tm), jnp.bool_)
    for _ in range(TOPK_GROUPS):
        _, f = _first_max(grp, lid, G)
        hit = lid == f
        gmask = jnp.logical_or(gmask, hit)
        grp = jnp.where(hit, neg, grp)
    emask = jnp.concatenate(
        [jnp.broadcast_to(gmask[g:g + 1, :], (GS, tm)) for g in range(G)], axis=0)
    sel = jnp.where(emask, sel, neg)

    chosen = jnp.zeros((E, tm), jnp.bool_)
    ids, ws = [], []
    for _ in range(TOP_K):
        _, f = _first_max(sel, eid, E)
        hit = eid == f
        ids.append(f)
        ws.append(jnp.sum(jnp.where(hit, scores, 0.0), axis=0, keepdims=True))
        chosen = jnp.logical_or(chosen, hit)
        sel = jnp.where(hit, neg, sel)
    ids = jnp.concatenate(ids, axis=0)
    ws = jnp.concatenate(ws, axis=0)
    ws = ws / jnp.sum(ws, axis=0, keepdims=True) * ROUTED_SCALE

    cf = jnp.where(chosen, 1.0, 0.0)
    before = _dot(cf.astype(BF16), lower_ref[...]) + run_sc[...]
    ranks = [jnp.sum(jnp.where(eid == ids[k:k + 1, :], before, 0.0), axis=0, keepdims=True)
             for k in range(TOP_K)]
    run_sc[...] = run_sc[...] + jnp.sum(cf, axis=1, keepdims=True)
    idx_ref[...] = ids
    w_ref[...] = ws
    rank_ref[...] = jnp.concatenate(ranks, axis=0).astype(I32)
    cnt_ref[...] = jnp.broadcast_to(run_sc[...], cnt_ref.shape).astype(I32)


def moe_router(x, w_router, b_router, tm=512):
    T, D = x.shape
    tm = min(tm, T)
    ar = jnp.arange(tm)
    lower = (ar[:, None] < ar[None, :]).astype(BF16)
    out = pl.BlockSpec((TOP_K, tm), lambda i: (0, i))
    return pl.pallas_call(
        functools.partial(_router_body, tm=tm),
        grid=(T // tm,),
        in_specs=[pl.BlockSpec((tm, D), lambda i: (i, 0)),
                  pl.BlockSpec((N_EXPERTS, D), lambda i: (0, 0)),
                  pl.BlockSpec((N_EXPERTS, 1), lambda i: (0, 0)),
                  pl.BlockSpec((tm, tm), lambda i: (0, 0))],
        out_specs=[out, out, out, pl.BlockSpec((N_EXPERTS, LANES), lambda i: (0, 0))],
        out_shape=[jax.ShapeDtypeStruct((TOP_K, T), I32), jax.ShapeDtypeStruct((TOP_K, T), F32),
                   jax.ShapeDtypeStruct((TOP_K, T), I32),
                   jax.ShapeDtypeStruct((N_EXPERTS, LANES), I32)],
        scratch_shapes=[pltpu.VMEM((N_EXPERTS, 1), F32)],
        compiler_params=_cparams(("arbitrary",)),
        name="moe_router",
    )(x, w_router.T, b_router.reshape(N_EXPERTS, 1), lower)


HALF = D_MODEL // 2


def _pack_rows(x):
    lo = pltpu.bitcast(x[:, :HALF].astype(BF16).astype(F32), I32)
    hi = pltpu.bitcast(x[:, HALF:].astype(BF16).astype(F32), I32)
    return lax.shift_right_logical(lo, 16) | (hi & jnp.int32(-65536))


def _unpack_rows(w):
    return pltpu.bitcast(w << 16, F32), pltpu.bitcast(w & jnp.int32(-65536), F32)


def _dispatch_body(x_ref, dest_hbm, xs_in, xs_hbm, idx_sm, pack_sc, sem_idx, sem_row, *, tm, nt):
    del xs_in
    i = pl.program_id(0)
    slot = i % 2
    n_rows = tm * TOP_K

    def idx_copy(tile, s):
        return pltpu.make_async_copy(dest_hbm.at[tile], idx_sm.at[s], sem_idx.at[s])

    def row_copy(s, t, row):
        return pltpu.make_async_copy(pack_sc.at[s, pl.ds(t, 1)], xs_hbm.at[pl.ds(row, 1)], sem_row.at[s])

    @pl.when(i == 0)
    def _first():
        idx_copy(0, 0).start()

    @pl.when(i + 1 < nt)
    def _prefetch():
        idx_copy(i + 1, 1 - slot).start()

    idx_copy(i, slot).wait()
    pack_sc[slot] = _pack_rows(x_ref[...])

    for t in range(tm):
        for k in range(TOP_K):
            row_copy(slot, t, idx_sm[slot, t * TOP_K + k]).start()

    def drain(s):
        pltpu.make_async_copy(xs_hbm.at[pl.ds(0, n_rows)], xs_hbm.at[pl.ds(n_rows, n_rows)], sem_row.at[s]).wait()

    @pl.when(i > 0)
    def _drain_prev():
        drain(1 - slot)

    @pl.when(i == nt - 1)
    def _drain_last():
        drain(slot)


def moe_dispatch(x, dest_tiles, n_rows_out, tm):
    T, D = x.shape
    nt = T // tm
    xs0 = jnp.zeros((n_rows_out, HALF), I32)
    return pl.pallas_call(
        functools.partial(_dispatch_body, tm=tm, nt=nt),
        grid=(nt,),
        in_specs=[pl.BlockSpec((tm, D), lambda i: (i, 0)),
                  pl.BlockSpec(memory_space=pl.ANY),
                  pl.BlockSpec(memory_space=pl.ANY)],
        out_specs=pl.BlockSpec(memory_space=pl.ANY),
        out_shape=jax.ShapeDtypeStruct((n_rows_out, HALF), I32),
        scratch_shapes=[pltpu.SMEM((2, tm * TOP_K), I32), pltpu.VMEM((2, tm, HALF), I32),
                        pltpu.SemaphoreType.DMA((2,)), pltpu.SemaphoreType.DMA((2,))],
        input_output_aliases={2: 0},
        compiler_params=_cparams(("arbitrary",)),
        name="moe_dispatch",
    )(x, dest_tiles, xs0)


def _expert_body(be_ref, nreal_ref, x_ref, wgu_ref, wd_ref, o_ref):
    @pl.when(pl.program_id(0) < nreal_ref[0])
    def _():
        lo, hi = _unpack_rows(x_ref[...])
        gu = _dot(lo.astype(BF16), wgu_ref[0, :HALF, :]) + _dot(hi.astype(BF16), wgu_ref[0, HALF:, :])
        gt, up = gu[:, :D_EXPERT], gu[:, D_EXPERT:]
        h = gt * jax.nn.sigmoid(gt) * up
        o_ref[...] = _pack_rows(_dot(h.astype(BF16), wd_ref[0]))

    @pl.when(pl.program_id(0) >= nreal_ref[0])
    def _():
        o_ref[...] = jnp.zeros(o_ref.shape, I32)


def moe_experts(xs, block_exp, n_real, w_gu, w_down):
    P = xs.shape[0]
    D, M = D_MODEL, EXPERT_BLOCK
    grid_spec = pltpu.PrefetchScalarGridSpec(
        num_scalar_prefetch=2,
        grid=(P // M,),
        in_specs=[pl.BlockSpec((M, HALF), lambda i, be, nr: (i, 0)),
                  pl.BlockSpec((1, D, 2 * D_EXPERT), lambda i, be, nr: (be[i], 0, 0)),
                  pl.BlockSpec((1, D_EXPERT, D), lambda i, be, nr: (be[i], 0, 0))],
        out_specs=pl.BlockSpec((M, HALF), lambda i, be, nr: (i, 0)),
    )
    return pl.pallas_call(
        _expert_body,
        grid_spec=grid_spec,
        out_shape=jax.ShapeDtypeStruct((P, HALF), I32),
        compiler_params=_cparams(("arbitrary",)),
        name="moe_experts",
    )(block_exp, n_real, xs, w_gu, w_down)


def _ffn_tail_body(x_ref, dest_hbm, ys_hbm, w_ref, p_ref, wsg_ref, wsd_ref, g_ref, b_ref, wg_ref, bg_ref, wp_ref,
                   o_ref, idx_sm, y_sc, sem_idx, sem_row, *, tm, nt):
    i = pl.program_id(0)
    slot = i % 2
    n_rows = tm * TOP_K

    def idx_copy(tile, s):
        return pltpu.make_async_copy(dest_hbm.at[tile], idx_sm.at[s], sem_idx.at[s])

    def row_copy(s, k, t, row):
        return pltpu.make_async_copy(ys_hbm.at[pl.ds(row, 1)], y_sc.at[s, pl.ds(k * tm + t, 1)], sem_row.at[s])

    def gather(s):
        for t in range(tm):
            for k in range(TOP_K):
                row_copy(s, k, t, idx_sm[s, t * TOP_K + k]).start()

    @pl.when(i == 0)
    def _first():
        idx_copy(0, 0).start()
        idx_copy(0, 0).wait()
        gather(0)

        @pl.when(nt > 1)
        def _():
            idx_copy(1, 1).start()

    @pl.when(i + 1 < nt)
    def _next():
        idx_copy(i + 1, 1 - slot).wait()
        gather(1 - slot)

        @pl.when(i + 2 < nt)
        def _():
            idx_copy(i + 2, slot).start()

    x = x_ref[...]
    gu = _dot(x.astype(BF16), wsg_ref[...].astype(BF16))
    gs, us = gu[:, :D_SHARED], gu[:, D_SHARED:]
    shared = _dot((gs * jax.nn.sigmoid(gs) * us).astype(BF16), wsd_ref[...].astype(BF16))
    pe = _dot(p_ref[...].astype(BF16), wp_ref[...].astype(BF16))

    pltpu.make_async_copy(ys_hbm.at[pl.ds(0, n_rows)], y_sc.at[slot], sem_row.at[slot]).wait()

    r_lo = jnp.zeros((tm, HALF), F32)
    r_hi = jnp.zeros((tm, HALF), F32)
    for k in range(TOP_K):
        lo, hi = _unpack_rows(y_sc[slot, k * tm:(k + 1) * tm, :])
        wk = w_ref[:, k:k + 1]
        r_lo = r_lo + lo * wk
        r_hi = r_hi + hi * wk
    routed = jnp.concatenate([r_lo, r_hi], axis=1)
    x2 = _layer_norm(DEEPNORM_ALPHA * x + (routed + shared), g_ref[...], b_ref[...])
    gate = jax.nn.sigmoid(_dot(x2.astype(BF16), wg_ref[...].astype(BF16)) + bg_ref[...])
    o_ref[...] = x2 + gate * pe


def ffn_tail(x, dest_tiles, ys, wt, p, ws_gu, ws_down, g, b, w_gate, b_gate, w_proj, tm):
    T, D = x.shape
    nt = T // tm
    full = lambda shape: pl.BlockSpec(shape, lambda i: tuple(0 for _ in shape))
    return pl.pallas_call(
        functools.partial(_ffn_tail_body, tm=tm, nt=nt),
        grid=(nt,),
        in_specs=[pl.BlockSpec((tm, D), lambda i: (i, 0)),
                  pl.BlockSpec(memory_space=pl.ANY),
                  pl.BlockSpec(memory_space=pl.ANY),
                  pl.BlockSpec((tm, TOP_K), lambda i: (i, 0)),
                  pl.BlockSpec((tm, PLE_DIM), lambda i: (i, 0)),
                  full((D, 2 * D_SHARED)), full((D_SHARED, D)),
                  full((1, D)), full((1, D)), full((D, D)), full((1, D)), full((PLE_DIM, D))],
        out_specs=pl.BlockSpec((tm, D), lambda i: (i, 0)),
        out_shape=jax.ShapeDtypeStruct((T, D), F32),
        scratch_shapes=[pltpu.SMEM((2, tm * TOP_K), I32), pltpu.VMEM((2, TOP_K * tm, HALF), I32),
                        pltpu.SemaphoreType.DMA((2,)), pltpu.SemaphoreType.DMA((2,))],
        compiler_params=_cparams(("arbitrary",)),
        name="ffn_tail",
    )(x, dest_tiles, ys, wt, p, ws_gu, ws_down, g.reshape(1, D), b.reshape(1, D), w_gate,
      b_gate.reshape(1, D), w_proj)


def _pad_cols(w, n):
    return jnp.pad(w, ((0, 0), (0, n - w.shape[1])))


def _qk_weights(wq, wk, q_scale=HEAD_DIM ** -0.5):
    return jnp.concatenate([wq * q_scale, wk], axis=1)


def _fox_layer(x, w_in, b_f, t):
    B, S, D = x.shape
    xt = x.reshape(B * S, D)
    qk = matmul(xt, _qk_weights(w_in[:, :D], w_in[:, D:2 * D], LOG2E * HEAD_DIM ** -0.5), BF16).reshape(B, S, 2 * D)
    vt = matmul_t_blocks(xt, w_in[:, 2 * D:3 * D].T, B, t, BF16)
    f = matmul(xt, _pad_cols(w_in[:, 3 * D:], LANES), F32).reshape(B, S, LANES)
    caug, cbs = fox_gates(f, b_f, t)
    cb = cbs[:, :, 0, :N_HEADS].transpose(0, 2, 1).reshape(-1)
    return fox_attention(qk, vt, caug, cb, t).reshape(B * S, D)


def _sb_layer(x, w_in, t):
    B, S, D = x.shape
    xt = x.reshape(B * S, D)
    qk = matmul(xt, _qk_weights(w_in[:, :D], w_in[:, D:2 * D], LOG2E * HEAD_DIM ** -0.5), BF16).reshape(B, S, 2 * D)
    vt = matmul_t_blocks(xt, w_in[:, 2 * D:].T, B, t, BF16)
    return sb_attention(qk, vt, t).reshape(B * S, D)


def _rope_tables(positions):
    half = ROPE_DIM // 2
    inv_freq = jnp.exp(-math.log(ROPE_THETA) * 2.0 * jnp.arange(half, dtype=F32) / ROPE_DIM)
    ang = positions.astype(F32).reshape(-1, 1) * inv_freq
    cos, sin = jnp.cos(ang), jnp.sin(ang)
    T = ang.shape[0]
    rest = HEAD_DIM - ROPE_DIM
    cos64 = jnp.concatenate([cos, cos, jnp.ones((T, rest), F32)], axis=1)
    sa64 = jnp.concatenate([-sin, jnp.zeros((T, HEAD_DIM - half), F32)], axis=1)
    sb64 = jnp.concatenate([jnp.zeros((T, half), F32), sin, jnp.zeros((T, rest), F32)], axis=1)
    return tuple(jnp.tile(a, (1, LANES // HEAD_DIM)) for a in (cos64, sa64, sb64))


def _dsa_layer(x, positions, w_in, tq, tk):
    B, S, D = x.shape
    xt = x.reshape(B * S, D)
    dh = HEAD_DIM
    o_k, o_v, o_qi = D, D + dh, D + 2 * dh
    o_ki = o_qi + IDX_HEADS * IDX_DIM
    o_wi = o_ki + IDX_DIM
    wq, wk, wv = w_in[:, :o_k], w_in[:, o_k:o_v], w_in[:, o_v:o_qi]
    wqi, wki, wwi = w_in[:, o_qi:o_ki], w_in[:, o_ki:o_wi], w_in[:, o_wi:]
    w_rope = jnp.concatenate([wq * (LOG2E * dh ** -0.5), wqi, wk, wk, wki, wki], axis=1)
    cos, sa, sb = _rope_tables(positions)
    proj = matmul_rope(xt, w_rope, cos, sa, sb, BF16).reshape(B, S, -1)
    vt = matmul_t_blocks(xt, jnp.concatenate([wv, wv], axis=1).T, B, tk, BF16).reshape(B, S // tk, LANES, tk)
    wit = matmul_t_rows(xt, _pad_cols(wwi * (IDX_HEADS ** -0.5 * IDX_DIM ** -0.5), LANES).T, B, F32)
    return dsa_attention(proj, vt, wit, tq, tk).reshape(B * S, D)


def _gla_layer(x, w_in, w_a2, b_a, norm_g):
    B, S, D = x.shape
    xt = x.reshape(B * S, D)
    W = 2 * GLA_HEADS * GLA_DK + 2 * GLA_HEADS * GLA_DV
    qkvg = matmul(xt, w_in[:, :W], F32).reshape(B, S, W)
    a1 = matmul(xt, _pad_cols(w_in[:, W:], LANES), F32).reshape(B, S, LANES)
    w_a2p = jnp.pad(w_a2, ((0, LANES - GLA_GATE_RANK), (0, 0)))
    return gla_attention(qkvg, a1, w_a2p, b_a, norm_g).reshape(B * S, GLA_HEADS * GLA_DV)


MOE_TOKEN_TILE = 256


def _moe_layer(x, p_i, w_router, b_router, w_gu, w_down, ws_gu, ws_down, g, b, w_gate, b_gate, w_proj):
    T, D = x.shape
    E, M = N_EXPERTS, EXPERT_BLOCK
    tm = min(MOE_TOKEN_TILE, T)
    idx, wts, rank, cnt = moe_router(x, w_router, b_router)
    counts = cnt[:, 0]
    padded = (counts + M - 1) // M * M
    pend = jnp.cumsum(padded)
    poff = pend - padded
    eids = jnp.arange(E, dtype=I32)[:, None, None]
    dest = rank + jnp.sum(jnp.where(idx[None] == eids, poff.astype(I32)[:, None, None], 0), axis=0)
    dest_tiles = dest.T.reshape(T // tm, tm * TOP_K)
    P = (T * TOP_K + M - 1) // M * M + E * M
    nb = P // M
    starts = jnp.arange(nb, dtype=I32) * M
    block_exp = jnp.minimum(jnp.sum((pend.astype(I32)[None, :] <= starts[:, None]).astype(I32), axis=1), E - 1)
    n_real = (pend[-1] // M).astype(I32).reshape(1)
    xs = moe_dispatch(x, dest_tiles, P, tm)
    ys = moe_experts(xs, block_exp, n_real, w_gu, w_down)
    return ffn_tail(x, dest_tiles, ys, wts.T, p_i, ws_gu, ws_down, g, b, w_gate, b_gate, w_proj, tm)


def kernel(x, p, positions, ln1_g, ln1_b, ln2_g, ln2_b, fox_w_in, fox_b_f, fox_w_out, sb_w_in, sb_w_out, dsa_w_in, dsa_w_out, gla_w_in, gla_w_a2, gla_b_a, gla_norm_g, gla_w_out, moe_w_router, moe_b_router, moe_w_gu, moe_w_down, shared_w_gu, shared_w_down, ple_w_proj, ple_w_gate, ple_b_gate):
    B, S, D = x.shape
    T = B * S
    depth = p.shape[0]
    t_attn = min(512, S)
    w_gu_bf = moe_w_gu.astype(BF16)
    w_down_bf = moe_w_down.astype(BF16)
    xt = x.reshape(T, D)
    for i in range(depth):
        m, j = i % 4, i // 4
        xb = xt.reshape(B, S, D)
        if m == 0:
            a, w_out = _fox_layer(xb, fox_w_in[j], fox_b_f[j], t_attn), fox_w_out[j]
        elif m == 1:
            a, w_out = _sb_layer(xb, sb_w_in[j], t_attn), sb_w_out[j]
        elif m == 2:
            a, w_out = _dsa_layer(xb, positions, dsa_w_in[j], min(256, S), min(512, S)), dsa_w_out[j]
        else:
            a, w_out = _gla_layer(xb, gla_w_in[j], gla_w_a2[j], gla_b_a[j], gla_norm_g[j]), gla_w_out[j]
        xt = matmul_residual_ln(a, w_out, xt, ln1_g[i], ln1_b[i])
        xt = _moe_layer(xt, p[i].reshape(T, -1), moe_w_router[i], moe_b_router[i], w_gu_bf[i], w_down_bf[i],
                        shared_w_gu[i], shared_w_down[i], ln2_g[i], ln2_b[i],
                        ple_w_gate[i], ple_b_gate[i], ple_w_proj[i])
    return xt.reshape(B, S, D)
```

```python
import functools
import math

import jax
import jax.numpy as jnp
from jax import lax
from jax.experimental import pallas as pl
from jax.experimental.pallas import tpu as pltpu

F32 = jnp.float32
BF16 = jnp.bfloat16
I32 = jnp.int32

D_MODEL = 1024
HEAD_DIM = 64
N_HEADS = D_MODEL // HEAD_DIM
N_PAIRS = N_HEADS // 2
LANES = 128
ROPE_DIM = HEAD_DIM // 4
ROPE_THETA = 500000.0
IDX_HEADS = 8
IDX_DIM = HEAD_DIM
IDX_TOPK_MAX = 256
GLA_HEADS = 4
GLA_DK = D_MODEL // 2 // GLA_HEADS
GLA_DV = D_MODEL // GLA_HEADS
GLA_GATE_RANK = 16
GLA_TAU = 16.0
GLA_CHUNK = 64
N_EXPERTS = 64
TOP_K = 8
N_GROUPS = 8
GROUP_SIZE = N_EXPERTS // N_GROUPS
TOPK_GROUPS = 4
D_EXPERT = 256
D_SHARED = 256
ROUTED_SCALE = 2.5
EXPERT_BLOCK = 512
PLE_DIM = 256
DEPTH = 4
DEEPNORM_ALPHA = (2 * DEPTH) ** 0.25
LN_EPS = 1e-5
RMS_EPS = 1e-6
INT_MIN = -(2 ** 31)
LOG2E = 1.4426950408889634
COUNT_CHAINS = 4
SB_SUB = 256
VMEM_LIMIT = 56 * 1024 * 1024


def _cparams(sem):
    return pltpu.CompilerParams(dimension_semantics=sem, vmem_limit_bytes=VMEM_LIMIT)


def _pick(n, cands):
    for c in cands:
        if n % c == 0:
            return c
    raise ValueError(f"no tile for {n}")


def _dot(a, b):
    return jnp.dot(a, b, preferred_element_type=F32)


def _dot_nt(a, b):
    return lax.dot_general(a, b, (((1,), (1,)), ((), ())), preferred_element_type=F32)


def _dot_tn(a, b):
    return lax.dot_general(a, b, (((0,), (0,)), ((), ())), preferred_element_type=F32)


def _split3(x):
    h1 = x.astype(BF16)
    r1 = x - h1.astype(F32)
    h2 = r1.astype(BF16)
    h3 = (r1 - h2.astype(F32)).astype(BF16)
    return h1, h2, h3


def _log_sigmoid(x):
    return jnp.minimum(x, 0.0) - jnp.log1p(jnp.exp(-jnp.abs(x)))


def _mm_body(x_ref, w_ref, o_ref):
    o_ref[...] = _dot(x_ref[...].astype(BF16), w_ref[...].astype(BF16)).astype(o_ref.dtype)


def matmul(x, w, out_dtype, tm=512):
    M, K = x.shape
    N = w.shape[1]
    tm = min(tm, M)
    tn = _pick(N, (512, 384, 256, 128))
    return pl.pallas_call(
        _mm_body,
        grid=(M // tm, N // tn),
        in_specs=[pl.BlockSpec((tm, K), lambda i, j: (i, 0)),
                  pl.BlockSpec((K, tn), lambda i, j: (0, j))],
        out_specs=pl.BlockSpec((tm, tn), lambda i, j: (i, j)),
        out_shape=jax.ShapeDtypeStruct((M, N), out_dtype),
        compiler_params=_cparams(("parallel", "parallel")),
        name="matmul",
    )(x, w)


def _mm_rope_body(x_ref, w_ref, cos_ref, sa_ref, sb_ref, o_ref):
    y = _dot(x_ref[...].astype(BF16), w_ref[...].astype(BF16))
    cos, sa, sb = cos_ref[...], sa_ref[...], sb_ref[...]
    for c in range(y.shape[1] // LANES):
        yc = y[:, c * LANES:(c + 1) * LANES]
        oc = yc * cos + pltpu.roll(yc, LANES - ROPE_DIM // 2, 1) * sa + pltpu.roll(yc, ROPE_DIM // 2, 1) * sb
        o_ref[:, c * LANES:(c + 1) * LANES] = oc.astype(o_ref.dtype)


def matmul_rope(x, w, cos, sa, sb, out_dtype, tm=512):
    M, K = x.shape
    N = w.shape[1]
    tm = min(tm, M)
    tn = _pick(N, (256, 128))
    tab = pl.BlockSpec((tm, LANES), lambda i, j: (i, 0))
    return pl.pallas_call(
        _mm_rope_body,
        grid=(M // tm, N // tn),
        in_specs=[pl.BlockSpec((tm, K), lambda i, j: (i, 0)),
                  pl.BlockSpec((K, tn), lambda i, j: (0, j)), tab, tab, tab],
        out_specs=pl.BlockSpec((tm, tn), lambda i, j: (i, j)),
        out_shape=jax.ShapeDtypeStruct((M, N), out_dtype),
        compiler_params=_cparams(("parallel", "parallel")),
        name="matmul_rope",
    )(x, w, cos, sa, sb)


def _layer_norm(y, g, b):
    mu = jnp.mean(y, axis=-1, keepdims=True)
    d = y - mu
    var = jnp.mean(d * d, axis=-1, keepdims=True)
    return d * lax.rsqrt(var + LN_EPS) * g + b


def _mm_res_ln_body(a_ref, w_ref, x_ref, g_ref, b_ref, o_ref):
    h = _dot(a_ref[...].astype(BF16), w_ref[...].astype(BF16))
    o_ref[...] = _layer_norm(DEEPNORM_ALPHA * x_ref[...] + h, g_ref[...], b_ref[...])


def matmul_residual_ln(a, w, x, g, b, tm=512):
    M, K = a.shape
    N = w.shape[1]
    tm = min(tm, M)
    return pl.pallas_call(
        _mm_res_ln_body,
        grid=(M // tm,),
        in_specs=[pl.BlockSpec((tm, K), lambda i: (i, 0)),
                  pl.BlockSpec((K, N), lambda i: (0, 0)),
                  pl.BlockSpec((tm, N), lambda i: (i, 0)),
                  pl.BlockSpec((1, N), lambda i: (0, 0)),
                  pl.BlockSpec((1, N), lambda i: (0, 0))],
        out_specs=pl.BlockSpec((tm, N), lambda i: (i, 0)),
        out_shape=jax.ShapeDtypeStruct((M, N), F32),
        compiler_params=_cparams(("parallel",)),
        name="matmul_residual_ln",
    )(a, w, x, g.reshape(1, N), b.reshape(1, N))


def _mm_t_blocks_body(wt_ref, x_ref, o_ref):
    xb = x_ref[...].astype(BF16)
    for j in range(o_ref.shape[1]):
        wj = wt_ref[j * LANES:(j + 1) * LANES, :].astype(BF16)
        o_ref[0, j, 0] = _dot_nt(wj, xb).astype(o_ref.dtype)


def _mm_t_rows_body(wt_ref, x_ref, o_ref):
    o_ref[0] = _dot_nt(wt_ref[...].astype(BF16), x_ref[...].astype(BF16)).astype(o_ref.dtype)


def matmul_t_blocks(xt, wt, B, t, out_dtype):
    T, K = xt.shape
    nk = T // B // t
    nj = wt.shape[0] // LANES
    return pl.pallas_call(
        _mm_t_blocks_body,
        grid=(B, nk),
        in_specs=[pl.BlockSpec((nj * LANES, K), lambda b, i: (0, 0)),
                  pl.BlockSpec((t, K), lambda b, i: (b * nk + i, 0))],
        out_specs=pl.BlockSpec((1, nj, 1, LANES, t), lambda b, i: (b, 0, i, 0, 0)),
        out_shape=jax.ShapeDtypeStruct((B, nj, nk, LANES, t), out_dtype),
        compiler_params=_cparams(("parallel", "parallel")),
        name="matmul_t_blocks",
    )(wt, xt)


def matmul_t_rows(xt, wt, B, out_dtype, tm=512):
    T, K = xt.shape
    S = T // B
    tm = min(tm, S)
    ns = S // tm
    return pl.pallas_call(
        _mm_t_rows_body,
        grid=(B, ns),
        in_specs=[pl.BlockSpec((LANES, K), lambda b, i: (0, 0)),
                  pl.BlockSpec((tm, K), lambda b, i: (b * ns + i, 0))],
        out_specs=pl.BlockSpec((1, LANES, tm), lambda b, i: (b, 0, i)),
        out_shape=jax.ShapeDtypeStruct((B, LANES, S), out_dtype),
        compiler_params=_cparams(("parallel", "parallel")),
        name="matmul_t_rows",
    )(wt, xt)


def _fox_gate_body(f_ref, bf_ref, tril_ref, place_ref, caug_ref, cb_ref, carry_sc, *, t):
    @pl.when(pl.program_id(1) == 0)
    def _reset():
        carry_sc[...] = jnp.zeros(carry_sc.shape, F32)

    log_f = _log_sigmoid(f_ref[0] + bf_ref[...])
    tril = tril_ref[...]
    h1, h2, h3 = _split3(log_f)
    c = _dot(tril, h1) + _dot(tril, h2) + _dot(tril, h3) + carry_sc[...]
    carry_sc[...] = c[t - 1:t, :]
    c = c * LOG2E
    c0 = c[0:1, :]
    d1, d2, d3 = _split3(c0 - c)
    caug = _dot(d1, place_ref[0]) + _dot(d2, place_ref[1]) + _dot(d3, place_ref[2])
    caug_ref[0] = caug.astype(BF16)
    cb_ref[0, 0] = c0


def fox_gates(f, b_f, t):
    B, S, _ = f.shape
    nk = S // t
    ar = jnp.arange(t)
    tril = (ar[None, :] <= ar[:, None]).astype(BF16)
    hh = jnp.arange(LANES)
    place = jnp.stack([((hh[None, :] == 3 * hh[:, None] + j) & (hh[:, None] < N_HEADS)).astype(BF16)
                       for j in range(3)])
    return pl.pallas_call(
        functools.partial(_fox_gate_body, t=t),
        grid=(B, nk),
        in_specs=[pl.BlockSpec((1, t, LANES), lambda b, i: (b, i, 0)),
                  pl.BlockSpec((1, LANES), lambda b, i: (0, 0)),
                  pl.BlockSpec((t, t), lambda b, i: (0, 0)),
                  pl.BlockSpec((3, LANES, LANES), lambda b, i: (0, 0, 0))],
        out_specs=[pl.BlockSpec((1, t, LANES), lambda b, i: (b, i, 0)),
                   pl.BlockSpec((1, 1, 1, LANES), lambda b, i: (b, i, 0, 0))],
        out_shape=[jax.ShapeDtypeStruct((B, S, LANES), BF16),
                   jax.ShapeDtypeStruct((B, nk, 1, LANES), F32)],
        scratch_shapes=[pltpu.VMEM((1, LANES), F32)],
        compiler_params=_cparams(("parallel", "arbitrary")),
        name="fox_gates",
    )(f, _pad_cols(b_f.reshape(1, -1), LANES), tril, place)


def _pair_halves(q2):
    lane = lax.broadcasted_iota(I32, q2.shape, 1)
    lo = lane < HEAD_DIM
    zero = jnp.zeros_like(q2)
    return lane, (jnp.where(lo, q2, zero), jnp.where(lo, zero, q2))


def _merge_heads_t(o_a, o_b):
    sub = lax.broadcasted_iota(I32, o_a.shape, 0)
    return jnp.transpose(jnp.where(sub < HEAD_DIM, o_a, o_b))


def _fox_body(cb_ref, q_ref, k_ref, vt_ref, caug_ref, o_ref, s_sc, m_sc, l_sc, acc_sc, *, t, nk):
    b, p, qb = pl.program_id(0), pl.program_id(1), pl.program_id(2)
    q2 = q_ref[0]
    lane, halves = _pair_halves(q2)
    rows = []
    for h in range(2):
        first = 3 * (2 * p + h)
        pick = jnp.logical_and(lane >= first, lane < first + 3)
        ones = jnp.where(pick, 1.0, 0.0).astype(BF16)
        rows.append(jnp.concatenate([halves[h], ones], axis=1))
    q_both = jnp.concatenate(rows, axis=0)
    m_sc[...] = jnp.full(m_sc.shape, -jnp.inf, F32)
    l_sc[...] = jnp.zeros(l_sc.shape, F32)
    acc_sc[...] = jnp.zeros(acc_sc.shape, F32)
    bases = [(b * N_HEADS + 2 * p + h) * nk for h in range(2)]

    def scores(kb):
        start = pl.multiple_of(kb * t, t)
        k_aug = jnp.concatenate([k_ref[0, pl.ds(start, t), :], caug_ref[0, pl.ds(start, t), :]], axis=1)
        return _dot_nt(k_aug, q_both)

    def softmax_step(kb, s, masked):
        if masked:
            kid = lax.broadcasted_iota(I32, (t, 2 * t), 0)
            qid = lax.broadcasted_iota(I32, (t, 2 * t), 1)
            qid = jnp.where(qid >= t, qid - t, qid)
            s = jnp.where(kid <= qid, s, -jnp.inf)
        off = jnp.concatenate([jnp.full((1, t), cb_ref[bs + qb] - cb_ref[bs + kb], F32) for bs in bases], axis=1)
        m_old = m_sc[...]
        m_new = jnp.maximum(m_old, jnp.max(s, axis=0, keepdims=True) + off)
        alpha = jnp.exp2(m_old - m_new)
        pr = jnp.exp2(s - (m_new - off))
        l_sc[...] = alpha * l_sc[...] + jnp.sum(pr, axis=0, keepdims=True)
        acc_sc[...] = alpha * acc_sc[...] + _dot(vt_ref[0, 0, kb], pr.astype(BF16))
        m_sc[...] = m_new

    s_sc[0] = scores(0)

    def full_block(kb, carry):
        s = s_sc[kb % 2]
        s_sc[(kb + 1) % 2] = scores(kb + 1)
        softmax_step(kb, s, False)
        return carry

    lax.fori_loop(0, qb, full_block, 0)
    softmax_step(qb, s_sc[qb % 2], True)
    o = acc_sc[...] / l_sc[...]
    o_ref[0] = _merge_heads_t(o[:, :t], o[:, t:]).astype(o_ref.dtype)


def fox_attention(qk, vt, caug, cb, t):
    B, S, _ = qk.shape
    nk = S // t
    grid_spec = pltpu.PrefetchScalarGridSpec(
        num_scalar_prefetch=1,
        grid=(B, N_PAIRS, nk),
        in_specs=[pl.BlockSpec((1, t, LANES), lambda b, p, i, cb: (b, i, p)),
                  pl.BlockSpec((1, S, LANES), lambda b, p, i, cb: (b, 0, N_PAIRS + p)),
                  pl.BlockSpec((1, 1, nk, LANES, t), lambda b, p, i, cb: (b, p, 0, 0, 0)),
                  pl.BlockSpec((1, S, LANES), lambda b, p, i, cb: (b, 0, 0))],
        out_specs=pl.BlockSpec((1, t, LANES), lambda b, p, i, cb: (b, i, p)),
        scratch_shapes=[pltpu.VMEM((2, t, 2 * t), F32), pltpu.VMEM((1, 2 * t), F32), pltpu.VMEM((1, 2 * t), F32),
                        pltpu.VMEM((LANES, 2 * t), F32)],
    )
    return pl.pallas_call(
        functools.partial(_fox_body, t=t, nk=nk),
        grid_spec=grid_spec,
        out_shape=jax.ShapeDtypeStruct((B, S, D_MODEL), BF16),
        compiler_params=_cparams(("parallel", "parallel", "arbitrary")),
        name="fox_attention",
    )(cb, qk, qk, vt, caug)


def _sb_body(q_ref, k_ref, vt_ref, upper_ref, o_ref, after_sc, acc_sc, *, t):
    qb = pl.program_id(2)
    _, halves = _pair_halves(q_ref[0])
    q_both = jnp.concatenate(halves, axis=0)
    after_sc[...] = jnp.zeros(after_sc.shape, F32)
    acc_sc[...] = jnp.zeros(acc_sc.shape, F32)
    neg_upper = upper_ref[...]

    def scores(kb):
        start = pl.multiple_of(kb * t, t)
        return _dot_nt(k_ref[0, pl.ds(start, t), :], q_both)

    def weigh(kb, z, masked):
        sp = jnp.maximum(z, 0.0) + jnp.log2(1.0 + jnp.exp2(-jnp.abs(z)))
        log_beta = z - sp
        if masked:
            kid = lax.broadcasted_iota(I32, (t, 2 * t), 0)
            qid = lax.broadcasted_iota(I32, (t, 2 * t), 1)
            strict = kid < jnp.where(qid >= t, qid - t, qid)
            sp = jnp.where(strict, sp, 0.0)
        spb = sp.astype(BF16)
        sub = neg_upper.shape[0]
        nsub = t // sub
        sums = [jnp.sum(sp[j * sub:(j + 1) * sub, :], axis=0, keepdims=True) for j in range(nsub)]
        later = after_sc[...]
        rests = [None] * nsub
        for j in reversed(range(nsub)):
            rests[j] = _dot(neg_upper, spb[j * sub:(j + 1) * sub, :]) + later
            later = later - sums[j]
        a = jnp.exp2(log_beta + jnp.concatenate(rests, axis=0))
        if masked:
            a = jnp.where(strict, a, 0.0)
        acc_sc[...] = acc_sc[...] + _dot(vt_ref[0, 0, kb], a.astype(BF16))
        after_sc[...] = later

    weigh(qb, scores(qb), True)

    def step(j, carry):
        kb = qb - 1 - j
        weigh(kb, scores(kb), False)
        return carry

    lax.fori_loop(0, qb, step, 0)
    o = acc_sc[...]
    o_ref[0] = _merge_heads_t(o[:, :t], o[:, t:]).astype(o_ref.dtype)


def sb_attention(qk, vt, t):
    B, S, _ = qk.shape
    nk = S // t
    ar = jnp.arange(min(SB_SUB, t))
    upper = -(ar[None, :] > ar[:, None]).astype(BF16)
    return pl.pallas_call(
        functools.partial(_sb_body, t=t),
        grid=(B, N_PAIRS, nk),
        in_specs=[pl.BlockSpec((1, t, LANES), lambda b, p, i: (b, i, p)),
                  pl.BlockSpec((1, S, LANES), lambda b, p, i: (b, 0, N_PAIRS + p)),
                  pl.BlockSpec((1, 1, nk, LANES, t), lambda b, p, i: (b, p, 0, 0, 0)),
                  pl.BlockSpec(upper.shape, lambda b, p, i: (0, 0))],
        out_specs=pl.BlockSpec((1, t, LANES), lambda b, p, i: (b, i, p)),
        out_shape=jax.ShapeDtypeStruct((B, S, D_MODEL), BF16),
        scratch_shapes=[pltpu.VMEM((1, 2 * t), F32), pltpu.VMEM((LANES, 2 * t), F32)],
        compiler_params=_cparams(("parallel", "parallel", "arbitrary")),
        name="sb_attention",
    )(qk, qk, vt, upper)


def _sortable_key(x):
    bits = pltpu.bitcast(x, I32)
    return bits ^ ((bits >> 31) & jnp.int32(0x7FFFFFFF))


def _dsa_body(q_ref, qi_ref, k_ref, ki_ref, vt_ref, wi_ref, lower_ref, o_ref,
              keys_sc, s_sc, m_sc, l_sc, acc_sc, *, tq, tk, topk):
    qb = pl.program_id(1)
    p = pl.program_id(2)
    nch = (qb * tq) // tk + 1
    kid = lax.broadcasted_iota(I32, (tk, tq), 0)
    qpos = qb * tq + lax.broadcasted_iota(I32, (tk, tq), 1)

    @pl.when(p == 0)
    def _select():
        def score_chunk(c, carry):
            start = pl.multiple_of(c * tk, tk)
            kidup = ki_ref[0, pl.ds(start, tk), :]
            score = jnp.zeros((tk, tq), F32)
            for hp in range(IDX_HEADS // 2):
                _, halves = _pair_halves(qi_ref[0, :, hp * LANES:(hp + 1) * LANES])
                for j in range(2):
                    h = 2 * hp + j
                    score = score + jnp.maximum(_dot_nt(kidup, halves[j]), 0.0) * wi_ref[0, h:h + 1, :]
            keys_sc[c] = jnp.where(c * tk + kid <= qpos, _sortable_key(score), jnp.int32(INT_MIN))
            return carry

        lax.fori_loop(0, nch, score_chunk, 0)

        def count_ge(thr):
            def body(c, cnts):
                kc = keys_sc[c]
                cnts = list(cnts)
                for j in range(tk // 8):
                    cnts[j % COUNT_CHAINS] = cnts[j % COUNT_CHAINS] + jnp.where(kc[j * 8:(j + 1) * 8, :] >= thr, 1, 0)
                return tuple(cnts)
            zero = jnp.zeros((8, tq), I32)
            cnts = lax.fori_loop(0, nch, body, (zero,) * COUNT_CHAINS)
            return jnp.sum(sum(cnts[1:], cnts[0]), axis=0, keepdims=True)

        def bit_step(i, thr):
            cand = thr + jnp.left_shift(jnp.int32(1), 31 - i)
            return jnp.where(count_ge(cand) >= topk, cand, thr)

        thr = lax.fori_loop(0, 32, bit_step, jnp.full((1, tq), INT_MIN, I32))
        n_ge = count_ge(thr)
        n_gt = count_ge(thr + 1)
        need = topk - n_gt
        tie = jnp.logical_and(n_ge - n_gt > need, thr > INT_MIN)
        any_tie = jnp.max(jnp.where(tie, 1, 0))

        def store_bias(c, sel):
            keys_sc[c] = pltpu.bitcast(jnp.where(sel, 0.0, -1e30).astype(F32), I32)

        @pl.when(any_tie == 0)
        def _no_ties():
            def body(c, carry):
                kc = keys_sc[c]
                store_bias(c, jnp.logical_and(kc >= thr, kc > INT_MIN))
                return carry
            lax.fori_loop(0, nch, body, 0)

        @pl.when(any_tie != 0)
        def _ties():
            needf = need.astype(F32)

            def body(c, seen):
                kc = keys_sc[c]
                eq = kc == thr
                eqf = jnp.where(eq, 1.0, 0.0)
                rank = _dot(lower_ref[...], eqf.astype(BF16)) + seen
                sel = jnp.logical_or(kc > thr, jnp.logical_and(eq, rank < needf))
                store_bias(c, jnp.logical_and(sel, kc > INT_MIN))
                return seen + jnp.sum(eqf, axis=0, keepdims=True)
            lax.fori_loop(0, nch, body, jnp.zeros((1, tq), F32))

    _, halves = _pair_halves(q_ref[0])
    q_both = jnp.concatenate(halves, axis=0)
    m_sc[...] = jnp.full(m_sc.shape, -1e30, F32)
    l_sc[...] = jnp.zeros(l_sc.shape, F32)
    acc_sc[...] = jnp.zeros(acc_sc.shape, F32)

    def scores(c):
        start = pl.multiple_of(c * tk, tk)
        return _dot_nt(k_ref[0, pl.ds(start, tk), :], q_both)

    def softmax_step(c, s_raw):
        bias = pltpu.bitcast(keys_sc[c], F32)
        s = s_raw + jnp.concatenate([bias, bias], axis=1)
        m_old = m_sc[...]
        m_new = jnp.maximum(m_old, jnp.max(s, axis=0, keepdims=True))
        alpha = jnp.exp2(m_old - m_new)
        pr = jnp.exp2(s - m_new)
        l_sc[...] = alpha * l_sc[...] + jnp.sum(pr, axis=0, keepdims=True)
        acc_sc[...] = alpha * acc_sc[...] + _dot(vt_ref[0, c], pr.astype(BF16))
        m_sc[...] = m_new

    s_sc[0] = scores(0)

    def attend(c, carry):
        s_raw = s_sc[c % 2]
        s_sc[(c + 1) % 2] = scores(c + 1)
        softmax_step(c, s_raw)
        return carry

    lax.fori_loop(0, nch - 1, attend, 0)
    softmax_step(nch - 1, s_sc[(nch - 1) % 2])
    o = acc_sc[...] / l_sc[...]
    o_ref[0] = _merge_heads_t(o[:, :tq], o[:, tq:]).astype(o_ref.dtype)


def dsa_attention(proj, vt, wit, tq, tk):
    B, S, _ = proj.shape
    topk = min(IDX_TOPK_MAX, S // 4)
    ar = jnp.arange(tk)
    lower = (ar[None, :] < ar[:, None]).astype(BF16)
    kern = functools.partial(_dsa_body, tq=tq, tk=tk, topk=topk)
    qi_blk = D_MODEL // (IDX_HEADS * IDX_DIM)
    k_blk = (D_MODEL + IDX_HEADS * IDX_DIM) // LANES
    nk = S // tk
    return pl.pallas_call(
        kern,
        grid=(B, S // tq, N_PAIRS),
        in_specs=[pl.BlockSpec((1, tq, LANES), lambda b, i, p: (b, i, p)),
                  pl.BlockSpec((1, tq, IDX_HEADS * IDX_DIM), lambda b, i, p: (b, i, qi_blk)),
                  pl.BlockSpec((1, S, LANES), lambda b, i, p: (b, 0, k_blk)),
                  pl.BlockSpec((1, S, LANES), lambda b, i, p: (b, 0, k_blk + 1)),
                  pl.BlockSpec((1, nk, LANES, tk), lambda b, i, p: (b, 0, 0, 0)),
                  pl.BlockSpec((1, LANES, tq), lambda b, i, p: (b, 0, i)),
                  pl.BlockSpec((tk, tk), lambda b, i, p: (0, 0))],
        out_specs=pl.BlockSpec((1, tq, LANES), lambda b, i, p: (b, i, p)),
        out_shape=jax.ShapeDtypeStruct((B, S, D_MODEL), BF16),
        scratch_shapes=[pltpu.VMEM((nk, tk, tq), I32), pltpu.VMEM((2, tk, 2 * tq), F32),
                        pltpu.VMEM((1, 2 * tq), F32), pltpu.VMEM((1, 2 * tq), F32),
                        pltpu.VMEM((LANES, 2 * tq), F32)],
        compiler_params=_cparams(("parallel", "arbitrary", "arbitrary")),
        name="dsa_attention",
    )(proj, proj, proj, proj, vt, wit, lower)


def _gla_body(qkvg_ref, a1_ref, wa2_ref, ba_ref, ng_ref, o_ref, state_sc, *, ts):
    C, H, dk, dv = GLA_CHUNK, GLA_HEADS, GLA_DK, GLA_DV

    @pl.when(pl.program_id(1) == 0)
    def _reset():
        state_sc[...] = jnp.zeros(state_sc.shape, F32)

    row = lax.broadcasted_iota(I32, (C, C), 0)
    colc = lax.broadcasted_iota(I32, (C, C), 1)
    causal = colc <= row
    tril = jnp.where(causal, 1.0, 0.0).astype(BF16)
    wa2 = wa2_ref[...].astype(BF16)
    ng = ng_ref[...]
    v_off, g_off = 2 * H * dk, 2 * H * dk + H * dv
    for n in range(ts // C):
        rows = slice(n * C, (n + 1) * C)
        za = _dot(a1_ref[0, rows, :].astype(BF16), wa2) + ba_ref[...]
        log_a = _log_sigmoid(za) / GLA_TAU
        h1, h2, h3 = _split3(log_a)
        bcum = _dot(tril, h1) + _dot(tril, h2) + _dot(tril, h3)
        eb = jnp.exp(bcum)
        enb = jnp.exp(-bcum)
        b_last = bcum[C - 1:C, :]
        eout = jnp.exp(b_last - bcum)
        dec = jnp.exp(b_last)
        for h in range(H):
            ks = slice(h * dk, (h + 1) * dk)
            q = qkvg_ref[0, rows, h * dk:(h + 1) * dk] * (dk ** -0.5)
            k = qkvg_ref[0, rows, H * dk + h * dk:H * dk + (h + 1) * dk]
            v = qkvg_ref[0, rows, v_off + h * dv:v_off + (h + 1) * dv].astype(BF16)
            q_in = (q * eb[:, ks]).astype(BF16)
            k_in = (k * enb[:, ks]).astype(BF16)
            k_out = (k * eout[:, ks]).astype(BF16)
            att = jnp.where(causal, _dot_nt(q_in, k_in), 0.0)
            state = state_sc[h]
            o = _dot(att.astype(BF16), v) + _dot(q_in, state.astype(BF16))
            dec_col = jnp.transpose(jnp.broadcast_to(dec[:, ks], (dk, dk)))[:, 0:1]
            state_sc[h] = state * dec_col + _dot_tn(k_out, v)
            o = o * lax.rsqrt(jnp.mean(o * o, axis=-1, keepdims=True) + RMS_EPS) * ng
            g = qkvg_ref[0, rows, g_off + h * dv:g_off + (h + 1) * dv]
            o = o * (g * jax.nn.sigmoid(g))
            o_ref[0, rows, h * dv:(h + 1) * dv] = o.astype(o_ref.dtype)


def gla_attention(qkvg, a1, w_a2p, b_a, norm_g, ts=512):
    B, S, W = qkvg.shape
    ts = min(ts, S)
    HK = GLA_HEADS * GLA_DK
    kern = functools.partial(_gla_body, ts=ts)
    return pl.pallas_call(
        kern,
        grid=(B, S // ts),
        in_specs=[pl.BlockSpec((1, ts, W), lambda b, i: (b, i, 0)),
                  pl.BlockSpec((1, ts, LANES), lambda b, i: (b, i, 0)),
                  pl.BlockSpec((LANES, HK), lambda b, i: (0, 0)),
                  pl.BlockSpec((1, HK), lambda b, i: (0, 0)),
                  pl.BlockSpec((1, GLA_DV), lambda b, i: (0, 0))],
        out_specs=pl.BlockSpec((1, ts, GLA_HEADS * GLA_DV), lambda b, i: (b, i, 0)),
        out_shape=jax.ShapeDtypeStruct((B, S, GLA_HEADS * GLA_DV), BF16),
        scratch_shapes=[pltpu.VMEM((GLA_HEADS, GLA_DK, GLA_DV), F32)],
        compiler_params=_cparams(("parallel", "arbitrary")),
        name="gla_attention",
    )(qkvg, a1, w_a2p, b_a.reshape(1, HK), norm_g.reshape(1, GLA_DV))


def _first_max(vals, ids, big):
    m = jnp.max(vals, axis=0, keepdims=True)
    first = jnp.min(jnp.where(vals == m, ids, big), axis=0, keepdims=True)
    return m, first


def _router_body(x_ref, wr_ref, br_ref, lower_ref, idx_ref, w_ref, rank_ref, cnt_ref, run_sc, *, tm):
    E, G, GS = N_EXPERTS, N_GROUPS, GROUP_SIZE

    @pl.when(pl.program_id(0) == 0)
    def _reset():
        run_sc[...] = jnp.zeros(run_sc.shape, F32)

    x = x_ref[...]
    x1 = x.astype(BF16)
    x2 = (x - x1.astype(F32)).astype(BF16)
    wr = wr_ref[...]
    w1 = wr.astype(BF16)
    w2 = (wr - w1.astype(F32)).astype(BF16)
    logits = _dot_nt(w1, x1) + _dot_nt(w1, x2) + _dot_nt(w2, x1)
    scores = jax.nn.sigmoid(logits)
    sel = scores + br_ref[...]
    neg = -jnp.inf
    eid = lax.broadcasted_iota(I32, (E, tm), 0)
    lid = lax.broadcasted_iota(I32, (GS, tm), 0)

    grp = []
    for g in range(G):
        sg = sel[g * GS:(g + 1) * GS, :]
        m1, f1 = _first_max(sg, lid, GS)
        m2 = jnp.max(jnp.where(lid == f1, neg, sg), axis=0, keepdims=True)
        grp.append(m1 + m2)
    grp = jnp.concatenate(grp, axis=0)
    gmask = jnp.zeros((G, tm), jnp.bool_)
    for _ in range(TOPK_GROUPS):
        _, f = _first_max(grp, lid, G)
        hit = lid == f
        gmask = jnp.logical_or(gmask, hit)
        grp = jnp.where(hit, neg, grp)
    emask = jnp.concatenate(
        [jnp.broadcast_to(gmask[g:g + 1, :], (GS, tm)) for g in range(G)], axis=0)
    sel = jnp.where(emask, sel, neg)

    chosen = jnp.zeros((E, tm), jnp.bool_)
    ids, ws = [], []
    for _ in range(TOP_K):
        _, f = _first_max(sel, eid, E)
        hit = eid == f
        ids.append(f)
        ws.append(jnp.sum(jnp.where(hit, scores, 0.0), axis=0, keepdims=True))
        chosen = jnp.logical_or(chosen, hit)
        sel = jnp.where(hit, neg, sel)
    ids = jnp.concatenate(ids, axis=0)
    ws = jnp.concatenate(ws, axis=0)
    ws = ws / jnp.sum(ws, axis=0, keepdims=True) * ROUTED_SCALE

    cf = jnp.where(chosen, 1.0, 0.0)
    before = _dot(cf.astype(BF16), lower_ref[...]) + run_sc[...]
    ranks = [jnp.sum(jnp.where(eid == ids[k:k + 1, :], before, 0.0), axis=0, keepdims=True)
             for k in range(TOP_K)]
    run_sc[...] = run_sc[...] + jnp.sum(cf, axis=1, keepdims=True)
    idx_ref[...] = ids
    w_ref[...] = ws
    rank_ref[...] = jnp.concatenate(ranks, axis=0).astype(I32)
    cnt_ref[...] = jnp.broadcast_to(run_sc[...], cnt_ref.shape).astype(I32)


def moe_router(x, w_router, b_router, tm=512):
    T, D = x.shape
    tm = min(tm, T)
    ar = jnp.arange(tm)
    lower = (ar[:, None] < ar[None, :]).astype(BF16)
    out = pl.BlockSpec((TOP_K, tm), lambda i: (0, i))
    return pl.pallas_call(
        functools.partial(_router_body, tm=tm),
        grid=(T // tm,),
        in_specs=[pl.BlockSpec((tm, D), lambda i: (i, 0)),
                  pl.BlockSpec((N_EXPERTS, D), lambda i: (0, 0)),
                  pl.BlockSpec((N_EXPERTS, 1), lambda i: (0, 0)),
                  pl.BlockSpec((tm, tm), lambda i: (0, 0))],
        out_specs=[out, out, out, pl.BlockSpec((N_EXPERTS, LANES), lambda i: (0, 0))],
        out_shape=[jax.ShapeDtypeStruct((TOP_K, T), I32), jax.ShapeDtypeStruct((TOP_K, T), F32),
                   jax.ShapeDtypeStruct((TOP_K, T), I32),
                   jax.ShapeDtypeStruct((N_EXPERTS, LANES), I32)],
        scratch_shapes=[pltpu.VMEM((N_EXPERTS, 1), F32)],
        compiler_params=_cparams(("arbitrary",)),
        name="moe_router",
    )(x, w_router.T, b_router.reshape(N_EXPERTS, 1), lower)


HALF = D_MODEL // 2


def _pack_rows(x):
    lo = pltpu.bitcast(x[:, :HALF].astype(BF16).astype(F32), I32)
    hi = pltpu.bitcast(x[:, HALF:].astype(BF16).astype(F32), I32)
    return lax.shift_right_logical(lo, 16) | (hi & jnp.int32(-65536))


def _unpack_rows(w):
    return pltpu.bitcast(w << 16, F32), pltpu.bitcast(w & jnp.int32(-65536), F32)


def _dispatch_body(x_ref, dest_hbm, xs_in, xs_hbm, idx_sm, pack_sc, sem_idx, sem_row, *, tm, nt):
    del xs_in
    i = pl.program_id(0)
    slot = i % 2
    n_rows = tm * TOP_K

    def idx_copy(tile, s):
        return pltpu.make_async_copy(dest_hbm.at[tile], idx_sm.at[s], sem_idx.at[s])

    def row_copy(s, t, row):
        return pltpu.make_async_copy(pack_sc.at[s, pl.ds(t, 1)], xs_hbm.at[pl.ds(row, 1)], sem_row.at[s])

    @pl.when(i == 0)
    def _first():
        idx_copy(0, 0).start()

    @pl.when(i + 1 < nt)
    def _prefetch():
        idx_copy(i + 1, 1 - slot).start()

    idx_copy(i, slot).wait()
    pack_sc[slot] = _pack_rows(x_ref[...])

    for t in range(tm):
        for k in range(TOP_K):
            row_copy(slot, t, idx_sm[slot, t * TOP_K + k]).start(priority=k % 2)

    def drain(s):
        pltpu.make_async_copy(xs_hbm.at[pl.ds(0, n_rows)], xs_hbm.at[pl.ds(n_rows, n_rows)], sem_row.at[s]).wait()

    @pl.when(i > 0)
    def _drain_prev():
        drain(1 - slot)

    @pl.when(i == nt - 1)
    def _drain_last():
        drain(slot)


def moe_dispatch(x, dest_tiles, n_rows_out, tm):
    T, D = x.shape
    nt = T // tm
    xs0 = jnp.zeros((n_rows_out, HALF), I32)
    return pl.pallas_call(
        functools.partial(_dispatch_body, tm=tm, nt=nt),
        grid=(nt,),
        in_specs=[pl.BlockSpec((tm, D), lambda i: (i, 0)),
                  pl.BlockSpec(memory_space=pl.ANY),
                  pl.BlockSpec(memory_space=pl.ANY)],
        out_specs=pl.BlockSpec(memory_space=pl.ANY),
        out_shape=jax.ShapeDtypeStruct((n_rows_out, HALF), I32),
        scratch_shapes=[pltpu.SMEM((2, tm * TOP_K), I32), pltpu.VMEM((2, tm, HALF), I32),
                        pltpu.SemaphoreType.DMA((2,)), pltpu.SemaphoreType.DMA((2,))],
        input_output_aliases={2: 0},
        compiler_params=_cparams(("arbitrary",)),
        name="moe_dispatch",
    )(x, dest_tiles, xs0)


def _expert_body(be_ref, nreal_ref, x_ref, wgu_ref, wd_ref, o_ref):
    @pl.when(pl.program_id(0) < nreal_ref[0])
    def _():
        lo, hi = _unpack_rows(x_ref[...])
        gu = _dot(lo.astype(BF16), wgu_ref[0, :HALF, :]) + _dot(hi.astype(BF16), wgu_ref[0, HALF:, :])
        gt, up = gu[:, :D_EXPERT], gu[:, D_EXPERT:]
        h = gt * jax.nn.sigmoid(gt) * up
        o_ref[...] = _pack_rows(_dot(h.astype(BF16), wd_ref[0]))

    @pl.when(pl.program_id(0) >= nreal_ref[0])
    def _():
        o_ref[...] = jnp.zeros(o_ref.shape, I32)


def moe_experts(xs, block_exp, n_real, w_gu, w_down):
    P = xs.shape[0]
    D, M = D_MODEL, EXPERT_BLOCK
    grid_spec = pltpu.PrefetchScalarGridSpec(
        num_scalar_prefetch=2,
        grid=(P // M,),
        in_specs=[pl.BlockSpec((M, HALF), lambda i, be, nr: (i, 0)),
                  pl.BlockSpec((1, D, 2 * D_EXPERT), lambda i, be, nr: (be[i], 0, 0)),
                  pl.BlockSpec((1, D_EXPERT, D), lambda i, be, nr: (be[i], 0, 0))],
        out_specs=pl.BlockSpec((M, HALF), lambda i, be, nr: (i, 0)),
    )
    return pl.pallas_call(
        _expert_body,
        grid_spec=grid_spec,
        out_shape=jax.ShapeDtypeStruct((P, HALF), I32),
        compiler_params=_cparams(("arbitrary",)),
        name="moe_experts",
    )(block_exp, n_real, xs, w_gu, w_down)


def _ffn_tail_body(x_ref, dest_hbm, ys_hbm, w_ref, p_ref, wsg_ref, wsd_ref, g_ref, b_ref, wg_ref, bg_ref, wp_ref,
                   o_ref, idx_sm, y_sc, sem_idx, sem_row, *, tm, nt):
    i = pl.program_id(0)
    slot = i % 2
    n_rows = tm * TOP_K

    def idx_copy(tile, s):
        return pltpu.make_async_copy(dest_hbm.at[tile], idx_sm.at[s], sem_idx.at[s])

    def row_copy(s, k, t, row):
        return pltpu.make_async_copy(ys_hbm.at[pl.ds(row, 1)], y_sc.at[s, pl.ds(k * tm + t, 1)], sem_row.at[s])

    def gather(s):
        for t in range(tm):
            for k in range(TOP_K):
                row_copy(s, k, t, idx_sm[s, t * TOP_K + k]).start(priority=k % 2)

    @pl.when(i == 0)
    def _first():
        idx_copy(0, 0).start()
        idx_copy(0, 0).wait()
        gather(0)

        @pl.when(nt > 1)
        def _():
            idx_copy(1, 1).start()

    @pl.when(i + 1 < nt)
    def _next():
        idx_copy(i + 1, 1 - slot).wait()
        gather(1 - slot)

        @pl.when(i + 2 < nt)
        def _():
            idx_copy(i + 2, slot).start()

    x = x_ref[...]
    gu = _dot(x.astype(BF16), wsg_ref[...].astype(BF16))
    gs, us = gu[:, :D_SHARED], gu[:, D_SHARED:]
    shared = _dot((gs * jax.nn.sigmoid(gs) * us).astype(BF16), wsd_ref[...].astype(BF16))
    pe = _dot(p_ref[...].astype(BF16), wp_ref[...].astype(BF16))

    pltpu.make_async_copy(ys_hbm.at[pl.ds(0, n_rows)], y_sc.at[slot], sem_row.at[slot]).wait()

    r_lo = jnp.zeros((tm, HALF), F32)
    r_hi = jnp.zeros((tm, HALF), F32)
    for k in range(TOP_K):
        lo, hi = _unpack_rows(y_sc[slot, k * tm:(k + 1) * tm, :])
        wk = w_ref[:, k:k + 1]
        r_lo = r_lo + lo * wk
        r_hi = r_hi + hi * wk
    routed = jnp.concatenate([r_lo, r_hi], axis=1)
    x2 = _layer_norm(DEEPNORM_ALPHA * x + (routed + shared), g_ref[...], b_ref[...])
    gate = jax.nn.sigmoid(_dot(x2.astype(BF16), wg_ref[...].astype(BF16)) + bg_ref[...])
    o_ref[...] = x2 + gate * pe


def ffn_tail(x, dest_tiles, ys, wt, p, ws_gu, ws_down, g, b, w_gate, b_gate, w_proj, tm):
    T, D = x.shape
    nt = T // tm
    full = lambda shape: pl.BlockSpec(shape, lambda i: tuple(0 for _ in shape))
    return pl.pallas_call(
        functools.partial(_ffn_tail_body, tm=tm, nt=nt),
        grid=(nt,),
        in_specs=[pl.BlockSpec((tm, D), lambda i: (i, 0)),
                  pl.BlockSpec(memory_space=pl.ANY),
                  pl.BlockSpec(memory_space=pl.ANY),
                  pl.BlockSpec((tm, TOP_K), lambda i: (i, 0)),
                  pl.BlockSpec((tm, PLE_DIM), lambda i: (i, 0)),
                  full((D, 2 * D_SHARED)), full((D_SHARED, D)),
                  full((1, D)), full((1, D)), full((D, D)), full((1, D)), full((PLE_DIM, D))],
        out_specs=pl.BlockSpec((tm, D), lambda i: (i, 0)),
        out_shape=jax.ShapeDtypeStruct((T, D), F32),
        scratch_shapes=[pltpu.SMEM((2, tm * TOP_K), I32), pltpu.VMEM((2, TOP_K * tm, HALF), I32),
                        pltpu.SemaphoreType.DMA((2,)), pltpu.SemaphoreType.DMA((2,))],
        compiler_params=_cparams(("arbitrary",)),
        name="ffn_tail",
    )(x, dest_tiles, ys, wt, p, ws_gu, ws_down, g.reshape(1, D), b.reshape(1, D), w_gate,
      b_gate.reshape(1, D), w_proj)


def _pad_cols(w, n):
    return jnp.pad(w, ((0, 0), (0, n - w.shape[1])))


def _qk_weights(wq, wk, q_scale=HEAD_DIM ** -0.5):
    return jnp.concatenate([wq * q_scale, wk], axis=1)


def _fox_layer(x, w_in, b_f, t):
    B, S, D = x.shape
    xt = x.reshape(B * S, D)
    qk = matmul(xt, _qk_weights(w_in[:, :D], w_in[:, D:2 * D], LOG2E * HEAD_DIM ** -0.5), BF16).reshape(B, S, 2 * D)
    vt = matmul_t_blocks(xt, w_in[:, 2 * D:3 * D].T, B, t, BF16)
    f = matmul(xt, _pad_cols(w_in[:, 3 * D:], LANES), F32).reshape(B, S, LANES)
    caug, cbs = fox_gates(f, b_f, t)
    cb = cbs[:, :, 0, :N_HEADS].transpose(0, 2, 1).reshape(-1)
    return fox_attention(qk, vt, caug, cb, t).reshape(B * S, D)


def _sb_layer(x, w_in, t):
    B, S, D = x.shape
    xt = x.reshape(B * S, D)
    qk = matmul(xt, _qk_weights(w_in[:, :D], w_in[:, D:2 * D], LOG2E * HEAD_DIM ** -0.5), BF16).reshape(B, S, 2 * D)
    vt = matmul_t_blocks(xt, w_in[:, 2 * D:].T, B, t, BF16)
    return sb_attention(qk, vt, t).reshape(B * S, D)


def _rope_tables(positions):
    half = ROPE_DIM // 2
    inv_freq = jnp.exp(-math.log(ROPE_THETA) * 2.0 * jnp.arange(half, dtype=F32) / ROPE_DIM)
    ang = positions.astype(F32).reshape(-1, 1) * inv_freq
    cos, sin = jnp.cos(ang), jnp.sin(ang)
    T = ang.shape[0]
    rest = HEAD_DIM - ROPE_DIM
    cos64 = jnp.concatenate([cos, cos, jnp.ones((T, rest), F32)], axis=1)
    sa64 = jnp.concatenate([-sin, jnp.zeros((T, HEAD_DIM - half), F32)], axis=1)
    sb64 = jnp.concatenate([jnp.zeros((T, half), F32), sin, jnp.zeros((T, rest), F32)], axis=1)
    return tuple(jnp.tile(a, (1, LANES // HEAD_DIM)) for a in (cos64, sa64, sb64))


def _dsa_layer(x, positions, w_in, tq, tk):
    B, S, D = x.shape
    xt = x.reshape(B * S, D)
    dh = HEAD_DIM
    o_k, o_v, o_qi = D, D + dh, D + 2 * dh
    o_ki = o_qi + IDX_HEADS * IDX_DIM
    o_wi = o_ki + IDX_DIM
    wq, wk, wv = w_in[:, :o_k], w_in[:, o_k:o_v], w_in[:, o_v:o_qi]
    wqi, wki, wwi = w_in[:, o_qi:o_ki], w_in[:, o_ki:o_wi], w_in[:, o_wi:]
    w_rope = jnp.concatenate([wq * (LOG2E * dh ** -0.5), wqi, wk, wk, wki, wki], axis=1)
    cos, sa, sb = _rope_tables(positions)
    proj = matmul_rope(xt, w_rope, cos, sa, sb, BF16).reshape(B, S, -1)
    vt = matmul_t_blocks(xt, jnp.concatenate([wv, wv], axis=1).T, B, tk, BF16).reshape(B, S // tk, LANES, tk)
    wit = matmul_t_rows(xt, _pad_cols(wwi * (IDX_HEADS ** -0.5 * IDX_DIM ** -0.5), LANES).T, B, F32)
    return dsa_attention(proj, vt, wit, tq, tk).reshape(B * S, D)


def _gla_layer(x, w_in, w_a2, b_a, norm_g):
    B, S, D = x.shape
    xt = x.reshape(B * S, D)
    W = 2 * GLA_HEADS * GLA_DK + 2 * GLA_HEADS * GLA_DV
    qkvg = matmul(xt, w_in[:, :W], F32).reshape(B, S, W)
    a1 = matmul(xt, _pad_cols(w_in[:, W:], LANES), F32).reshape(B, S, LANES)
    w_a2p = jnp.pad(w_a2, ((0, LANES - GLA_GATE_RANK), (0, 0)))
    return gla_attention(qkvg, a1, w_a2p, b_a, norm_g).reshape(B * S, GLA_HEADS * GLA_DV)


MOE_TOKEN_TILE = 256


def _moe_layer(x, p_i, w_router, b_router, w_gu, w_down, ws_gu, ws_down, g, b, w_gate, b_gate, w_proj):
    T, D = x.shape
    E, M = N_EXPERTS, EXPERT_BLOCK
    tm = min(MOE_TOKEN_TILE, T)
    idx, wts, rank, cnt = moe_router(x, w_router, b_router)
    counts = cnt[:, 0]
    padded = (counts + M - 1) // M * M
    pend = jnp.cumsum(padded)
    poff = pend - padded
    eids = jnp.arange(E, dtype=I32)[:, None, None]
    dest = rank + jnp.sum(jnp.where(idx[None] == eids, poff.astype(I32)[:, None, None], 0), axis=0)
    dest_tiles = dest.T.reshape(T // tm, tm * TOP_K)
    P = (T * TOP_K + M - 1) // M * M + E * M
    nb = P // M
    starts = jnp.arange(nb, dtype=I32) * M
    block_exp = jnp.minimum(jnp.sum((pend.astype(I32)[None, :] <= starts[:, None]).astype(I32), axis=1), E - 1)
    n_real = (pend[-1] // M).astype(I32).reshape(1)
    xs = moe_dispatch(x, dest_tiles, P, tm)
    ys = moe_experts(xs, block_exp, n_real, w_gu, w_down)
    return ffn_tail(x, dest_tiles, ys, wts.T, p_i, ws_gu, ws_down, g, b, w_gate, b_gate, w_proj, tm)


def kernel(x, p, positions, ln1_g, ln1_b, ln2_g, ln2_b, fox_w_in, fox_b_f, fox_w_out, sb_w_in, sb_w_out, dsa_w_in, dsa_w_out, gla_w_in, gla_w_a2, gla_b_a, gla_norm_g, gla_w_out, moe_w_router, moe_b_router, moe_w_gu, moe_w_down, shared_w_gu, shared_w_down, ple_w_proj, ple_w_gate, ple_b_gate):
    B, S, D = x.shape
    T = B * S
    depth = p.shape[0]
    t_attn = min(512, S)
    w_gu_bf = moe_w_gu.astype(BF16)
    w_down_bf = moe_w_down.astype(BF16)
    xt = x.reshape(T, D)
    for i in range(depth):
        m, j = i % 4, i // 4
        xb = xt.reshape(B, S, D)
        if m == 0:
            a, w_out = _fox_layer(xb, fox_w_in[j], fox_b_f[j], t_attn), fox_w_out[j]
        elif m == 1:
            a, w_out = _sb_layer(xb, sb_w_in[j], t_attn), sb_w_out[j]
        elif m == 2:
            a, w_out = _dsa_layer(xb, positions, dsa_w_in[j], min(256, S), min(512, S)), dsa_w_out[j]
        else:
            a, w_out = _gla_layer(xb, gla_w_in[j], gla_w_a2[j], gla_b_a[j], gla_norm_g[j]), gla_w_out[j]
        xt = matmul_residual_ln(a, w_out, xt, ln1_g[i], ln1_b[i])
        xt = _moe_layer(xt, p[i].reshape(T, -1), moe_w_router[i], moe_b_router[i], w_gu_bf[i], w_down_bf[i],
                        shared_w_gu[i], shared_w_down[i], ln2_g[i], ln2_b[i],
                        ple_w_gate[i], ple_b_gate[i], ple_w_proj[i])
    return xt.reshape(B, S, D)
```

```python
import functools
import math

import jax
import jax.numpy as jnp
from jax import lax
from jax.experimental import pallas as pl
from jax.experimental.pallas import tpu as pltpu

F32 = jnp.float32
BF16 = jnp.bfloat16
I32 = jnp.int32

D_MODEL = 1024
HEAD_DIM = 64
N_HEADS = D_MODEL // HEAD_DIM
N_PAIRS = N_HEADS // 2
LANES = 128
ROPE_DIM = HEAD_DIM // 4
ROPE_THETA = 500000.0
IDX_HEADS = 8
IDX_DIM = HEAD_DIM
IDX_TOPK_MAX = 256
GLA_HEADS = 4
GLA_DK = D_MODEL // 2 // GLA_HEADS
GLA_DV = D_MODEL // GLA_HEADS
GLA_GATE_RANK = 16
GLA_TAU = 16.0
GLA_CHUNK = 64
N_EXPERTS = 64
TOP_K = 8
N_GROUPS = 8
GROUP_SIZE = N_EXPERTS // N_GROUPS
TOPK_GROUPS = 4
D_EXPERT = 256
D_SHARED = 256
ROUTED_SCALE = 2.5
EXPERT_BLOCK = 512
PLE_DIM = 256
DEPTH = 4
DEEPNORM_ALPHA = (2 * DEPTH) ** 0.25
LN_EPS = 1e-5
RMS_EPS = 1e-6
INT_MIN = -(2 ** 31)
LOG2E = 1.4426950408889634
COUNT_CHAINS = 4
SB_SUB = 256
VMEM_LIMIT = 56 * 1024 * 1024


def _cparams(sem):
    return pltpu.CompilerParams(dimension_semantics=sem, vmem_limit_bytes=VMEM_LIMIT)


def _pick(n, cands):
    for c in cands:
        if n % c == 0:
            return c
    raise ValueError(f"no tile for {n}")


def _dot(a, b):
    return jnp.dot(a, b, preferred_element_type=F32)


def _dot_nt(a, b):
    return lax.dot_general(a, b, (((1,), (1,)), ((), ())), preferred_element_type=F32)


def _dot_tn(a, b):
    return lax.dot_general(a, b, (((0,), (0,)), ((), ())), preferred_element_type=F32)


def _split3(x):
    h1 = x.astype(BF16)
    r1 = x - h1.astype(F32)
    h2 = r1.astype(BF16)
    h3 = (r1 - h2.astype(F32)).astype(BF16)
    return h1, h2, h3


def _log_sigmoid(x):
    return jnp.minimum(x, 0.0) - jnp.log1p(jnp.exp(-jnp.abs(x)))


def _mm_body(x_ref, w_ref, o_ref):
    o_ref[...] = _dot(x_ref[...].astype(BF16), w_ref[...].astype(BF16)).astype(o_ref.dtype)


def matmul(x, w, out_dtype, tm=512):
    M, K = x.shape
    N = w.shape[1]
    tm = min(tm, M)
    tn = _pick(N, (512, 384, 256, 128))
    return pl.pallas_call(
        _mm_body,
        grid=(M // tm, N // tn),
        in_specs=[pl.BlockSpec((tm, K), lambda i, j: (i, 0)),
                  pl.BlockSpec((K, tn), lambda i, j: (0, j))],
        out_specs=pl.BlockSpec((tm, tn), lambda i, j: (i, j)),
        out_shape=jax.ShapeDtypeStruct((M, N), out_dtype),
        compiler_params=_cparams(("parallel", "parallel")),
        name="matmul",
    )(x, w)


def _mm_rope_body(x_ref, w_ref, cos_ref, sa_ref, sb_ref, o_ref):
    y = _dot(x_ref[...].astype(BF16), w_ref[...].astype(BF16))
    cos, sa, sb = cos_ref[...], sa_ref[...], sb_ref[...]
    for c in range(y.shape[1] // LANES):
        yc = y[:, c * LANES:(c + 1) * LANES]
        oc = yc * cos + pltpu.roll(yc, LANES - ROPE_DIM // 2, 1) * sa + pltpu.roll(yc, ROPE_DIM // 2, 1) * sb
        o_ref[:, c * LANES:(c + 1) * LANES] = oc.astype(o_ref.dtype)


def matmul_rope(x, w, cos, sa, sb, out_dtype, tm=512):
    M, K = x.shape
    N = w.shape[1]
    tm = min(tm, M)
    tn = _pick(N, (256, 128))
    tab = pl.BlockSpec((tm, LANES), lambda i, j: (i, 0))
    return pl.pallas_call(
        _mm_rope_body,
        grid=(M // tm, N // tn),
        in_specs=[pl.BlockSpec((tm, K), lambda i, j: (i, 0)),
                  pl.BlockSpec((K, tn), lambda i, j: (0, j)), tab, tab, tab],
        out_specs=pl.BlockSpec((tm, tn), lambda i, j: (i, j)),
        out_shape=jax.ShapeDtypeStruct((M, N), out_dtype),
        compiler_params=_cparams(("parallel", "parallel")),
        name="matmul_rope",
    )(x, w, cos, sa, sb)


def _layer_norm(y, g, b):
    mu = jnp.mean(y, axis=-1, keepdims=True)
    d = y - mu
    var = jnp.mean(d * d, axis=-1, keepdims=True)
    return d * lax.rsqrt(var + LN_EPS) * g + b


def _mm_res_ln_body(a_ref, w_ref, x_ref, g_ref, b_ref, o_ref):
    h = _dot(a_ref[...].astype(BF16), w_ref[...].astype(BF16))
    o_ref[...] = _layer_norm(DEEPNORM_ALPHA * x_ref[...] + h, g_ref[...], b_ref[...])


def matmul_residual_ln(a, w, x, g, b, tm=512):
    M, K = a.shape
    N = w.shape[1]
    tm = min(tm, M)
    return pl.pallas_call(
        _mm_res_ln_body,
        grid=(M // tm,),
        in_specs=[pl.BlockSpec((tm, K), lambda i: (i, 0)),
                  pl.BlockSpec((K, N), lambda i: (0, 0)),
                  pl.BlockSpec((tm, N), lambda i: (i, 0)),
                  pl.BlockSpec((1, N), lambda i: (0, 0)),
                  pl.BlockSpec((1, N), lambda i: (0, 0))],
        out_specs=pl.BlockSpec((tm, N), lambda i: (i, 0)),
        out_shape=jax.ShapeDtypeStruct((M, N), F32),
        compiler_params=_cparams(("parallel",)),
        name="matmul_residual_ln",
    )(a, w, x, g.reshape(1, N), b.reshape(1, N))


def _mm_t_blocks_body(wt_ref, x_ref, o_ref):
    xb = x_ref[...].astype(BF16)
    for j in range(o_ref.shape[1]):
        wj = wt_ref[j * LANES:(j + 1) * LANES, :].astype(BF16)
        o_ref[0, j, 0] = _dot_nt(wj, xb).astype(o_ref.dtype)


def _mm_t_rows_body(wt_ref, x_ref, o_ref):
    o_ref[0] = _dot_nt(wt_ref[...].astype(BF16), x_ref[...].astype(BF16)).astype(o_ref.dtype)


def matmul_t_blocks(xt, wt, B, t, out_dtype):
    T, K = xt.shape
    nk = T // B // t
    nj = wt.shape[0] // LANES
    return pl.pallas_call(
        _mm_t_blocks_body,
        grid=(B, nk),
        in_specs=[pl.BlockSpec((nj * LANES, K), lambda b, i: (0, 0)),
                  pl.BlockSpec((t, K), lambda b, i: (b * nk + i, 0))],
        out_specs=pl.BlockSpec((1, nj, 1, LANES, t), lambda b, i: (b, 0, i, 0, 0)),
        out_shape=jax.ShapeDtypeStruct((B, nj, nk, LANES, t), out_dtype),
        compiler_params=_cparams(("parallel", "parallel")),
        name="matmul_t_blocks",
    )(wt, xt)


def matmul_t_rows(xt, wt, B, out_dtype, tm=512):
    T, K = xt.shape
    S = T // B
    tm = min(tm, S)
    ns = S // tm
    return pl.pallas_call(
        _mm_t_rows_body,
        grid=(B, ns),
        in_specs=[pl.BlockSpec((LANES, K), lambda b, i: (0, 0)),
                  pl.BlockSpec((tm, K), lambda b, i: (b * ns + i, 0))],
        out_specs=pl.BlockSpec((1, LANES, tm), lambda b, i: (b, 0, i)),
        out_shape=jax.ShapeDtypeStruct((B, LANES, S), out_dtype),
        compiler_params=_cparams(("parallel", "parallel")),
        name="matmul_t_rows",
    )(wt, xt)


def _fox_gate_body(f_ref, bf_ref, tril_ref, place_ref, caug_ref, cb_ref, carry_sc, *, t):
    @pl.when(pl.program_id(1) == 0)
    def _reset():
        carry_sc[...] = jnp.zeros(carry_sc.shape, F32)

    log_f = _log_sigmoid(f_ref[0] + bf_ref[...])
    tril = tril_ref[...]
    h1, h2, h3 = _split3(log_f)
    c = _dot(tril, h1) + _dot(tril, h2) + _dot(tril, h3) + carry_sc[...]
    carry_sc[...] = c[t - 1:t, :]
    c = c * LOG2E
    c0 = c[0:1, :]
    d1, d2, d3 = _split3(c0 - c)
    caug = _dot(d1, place_ref[0]) + _dot(d2, place_ref[1]) + _dot(d3, place_ref[2])
    caug_ref[0] = caug.astype(BF16)
    cb_ref[0, 0] = c0


def fox_gates(f, b_f, t):
    B, S, _ = f.shape
    nk = S // t
    ar = jnp.arange(t)
    tril = (ar[None, :] <= ar[:, None]).astype(BF16)
    hh = jnp.arange(LANES)
    place = jnp.stack([((hh[None, :] == 3 * hh[:, None] + j) & (hh[:, None] < N_HEADS)).astype(BF16)
                       for j in range(3)])
    return pl.pallas_call(
        functools.partial(_fox_gate_body, t=t),
        grid=(B, nk),
        in_specs=[pl.BlockSpec((1, t, LANES), lambda b, i: (b, i, 0)),
                  pl.BlockSpec((1, LANES), lambda b, i: (0, 0)),
                  pl.BlockSpec((t, t), lambda b, i: (0, 0)),
                  pl.BlockSpec((3, LANES, LANES), lambda b, i: (0, 0, 0))],
        out_specs=[pl.BlockSpec((1, t, LANES), lambda b, i: (b, i, 0)),
                   pl.BlockSpec((1, 1, 1, LANES), lambda b, i: (b, i, 0, 0))],
        out_shape=[jax.ShapeDtypeStruct((B, S, LANES), BF16),
                   jax.ShapeDtypeStruct((B, nk, 1, LANES), F32)],
        scratch_shapes=[pltpu.VMEM((1, LANES), F32)],
        compiler_params=_cparams(("parallel", "arbitrary")),
        name="fox_gates",
    )(f, _pad_cols(b_f.reshape(1, -1), LANES), tril, place)


def _pair_halves(q2):
    lane = lax.broadcasted_iota(I32, q2.shape, 1)
    lo = lane < HEAD_DIM
    zero = jnp.zeros_like(q2)
    return lane, (jnp.where(lo, q2, zero), jnp.where(lo, zero, q2))


def _merge_heads_t(o_a, o_b):
    sub = lax.broadcasted_iota(I32, o_a.shape, 0)
    return jnp.transpose(jnp.where(sub < HEAD_DIM, o_a, o_b))


def _fox_body(cb_ref, q_ref, k_ref, vt_ref, caug_ref, o_ref, s_sc, m_sc, l_sc, acc_sc, *, t, nk):
    b, p, qb = pl.program_id(0), pl.program_id(1), pl.program_id(2)
    q2 = q_ref[0]
    lane, halves = _pair_halves(q2)
    rows = []
    for h in range(2):
        first = 3 * (2 * p + h)
        pick = jnp.logical_and(lane >= first, lane < first + 3)
        ones = jnp.where(pick, 1.0, 0.0).astype(BF16)
        rows.append(jnp.concatenate([halves[h], ones], axis=1))
    q_both = jnp.concatenate(rows, axis=0)
    m_sc[...] = jnp.full(m_sc.shape, -jnp.inf, F32)
    l_sc[...] = jnp.zeros(l_sc.shape, F32)
    acc_sc[...] = jnp.zeros(acc_sc.shape, F32)
    bases = [(b * N_HEADS + 2 * p + h) * nk for h in range(2)]

    def scores(kb):
        start = pl.multiple_of(kb * t, t)
        k_aug = jnp.concatenate([k_ref[0, pl.ds(start, t), :], caug_ref[0, pl.ds(start, t), :]], axis=1)
        return _dot_nt(k_aug, q_both)

    def softmax_step(kb, s, masked):
        if masked:
            kid = lax.broadcasted_iota(I32, (t, 2 * t), 0)
            qid = lax.broadcasted_iota(I32, (t, 2 * t), 1)
            qid = jnp.where(qid >= t, qid - t, qid)
            s = jnp.where(kid <= qid, s, -jnp.inf)
        off = jnp.concatenate([jnp.full((1, t), cb_ref[bs + qb] - cb_ref[bs + kb], F32) for bs in bases], axis=1)
        m_old = m_sc[...]
        m_new = jnp.maximum(m_old, jnp.max(s, axis=0, keepdims=True) + off)
        alpha = jnp.exp2(m_old - m_new)
        pr = jnp.exp2(s - (m_new - off))
        l_sc[...] = alpha * l_sc[...] + jnp.sum(pr, axis=0, keepdims=True)
        acc_sc[...] = alpha * acc_sc[...] + _dot(vt_ref[0, 0, kb], pr.astype(BF16))
        m_sc[...] = m_new

    s_sc[0] = scores(0)

    def full_block(kb, carry):
        s = s_sc[kb % 2]
        s_sc[(kb + 1) % 2] = scores(kb + 1)
        softmax_step(kb, s, False)
        return carry

    lax.fori_loop(0, qb, full_block, 0)
    softmax_step(qb, s_sc[qb % 2], True)
    o = acc_sc[...] / l_sc[...]
    o_ref[0] = _merge_heads_t(o[:, :t], o[:, t:]).astype(o_ref.dtype)


def fox_attention(qk, vt, caug, cb, t):
    B, S, _ = qk.shape
    nk = S // t
    grid_spec = pltpu.PrefetchScalarGridSpec(
        num_scalar_prefetch=1,
        grid=(B, N_PAIRS, nk),
        in_specs=[pl.BlockSpec((1, t, LANES), lambda b, p, i, cb: (b, i, p)),
                  pl.BlockSpec((1, S, LANES), lambda b, p, i, cb: (b, 0, N_PAIRS + p)),
                  pl.BlockSpec((1, 1, nk, LANES, t), lambda b, p, i, cb: (b, p, 0, 0, 0)),
                  pl.BlockSpec((1, S, LANES), lambda b, p, i, cb: (b, 0, 0))],
        out_specs=pl.BlockSpec((1, t, LANES), lambda b, p, i, cb: (b, i, p)),
        scratch_shapes=[pltpu.VMEM((2, t, 2 * t), F32), pltpu.VMEM((1, 2 * t), F32), pltpu.VMEM((1, 2 * t), F32),
                        pltpu.VMEM((LANES, 2 * t), F32)],
    )
    return pl.pallas_call(
        functools.partial(_fox_body, t=t, nk=nk),
        grid_spec=grid_spec,
        out_shape=jax.ShapeDtypeStruct((B, S, D_MODEL), BF16),
        compiler_params=_cparams(("parallel", "parallel", "arbitrary")),
        name="fox_attention",
    )(cb, qk, qk, vt, caug)


def _sb_body(q_ref, k_ref, vt_ref, upper_ref, o_ref, after_sc, acc_sc, *, t):
    qb = pl.program_id(2)
    _, halves = _pair_halves(q_ref[0])
    q_both = jnp.concatenate(halves, axis=0)
    after_sc[...] = jnp.zeros(after_sc.shape, F32)
    acc_sc[...] = jnp.zeros(acc_sc.shape, F32)
    neg_upper = upper_ref[...]

    def scores(kb):
        start = pl.multiple_of(kb * t, t)
        return _dot_nt(k_ref[0, pl.ds(start, t), :], q_both)

    def weigh(kb, z, masked):
        sp = jnp.maximum(z, 0.0) + jnp.log2(1.0 + jnp.exp2(-jnp.abs(z)))
        log_beta = z - sp
        if masked:
            kid = lax.broadcasted_iota(I32, (t, 2 * t), 0)
            qid = lax.broadcasted_iota(I32, (t, 2 * t), 1)
            strict = kid < jnp.where(qid >= t, qid - t, qid)
            sp = jnp.where(strict, sp, 0.0)
        spb = sp.astype(BF16)
        sub = neg_upper.shape[0]
        nsub = t // sub
        sums = [jnp.sum(sp[j * sub:(j + 1) * sub, :], axis=0, keepdims=True) for j in range(nsub)]
        later = after_sc[...]
        rests = [None] * nsub
        for j in reversed(range(nsub)):
            rests[j] = _dot(neg_upper, spb[j * sub:(j + 1) * sub, :]) + later
            later = later - sums[j]
        a = jnp.exp2(log_beta + jnp.concatenate(rests, axis=0))
        if masked:
            a = jnp.where(strict, a, 0.0)
        acc_sc[...] = acc_sc[...] + _dot(vt_ref[0, 0, kb], a.astype(BF16))
        after_sc[...] = later

    weigh(qb, scores(qb), True)

    def step(j, carry):
        kb = qb - 1 - j
        weigh(kb, scores(kb), False)
        return carry

    lax.fori_loop(0, qb, step, 0)
    o = acc_sc[...]
    o_ref[0] = _merge_heads_t(o[:, :t], o[:, t:]).astype(o_ref.dtype)


def sb_attention(qk, vt, t):
    B, S, _ = qk.shape
    nk = S // t
    ar = jnp.arange(min(SB_SUB, t))
    upper = -(ar[None, :] > ar[:, None]).astype(BF16)
    return pl.pallas_call(
        functools.partial(_sb_body, t=t),
        grid=(B, N_PAIRS, nk),
        in_specs=[pl.BlockSpec((1, t, LANES), lambda b, p, i: (b, i, p)),
                  pl.BlockSpec((1, S, LANES), lambda b, p, i: (b, 0, N_PAIRS + p)),
                  pl.BlockSpec((1, 1, nk, LANES, t), lambda b, p, i: (b, p, 0, 0, 0)),
                  pl.BlockSpec(upper.shape, lambda b, p, i: (0, 0))],
        out_specs=pl.BlockSpec((1, t, LANES), lambda b, p, i: (b, i, p)),
        out_shape=jax.ShapeDtypeStruct((B, S, D_MODEL), BF16),
        scratch_shapes=[pltpu.VMEM((1, 2 * t), F32), pltpu.VMEM((LANES, 2 * t), F32)],
        compiler_params=_cparams(("parallel", "parallel", "arbitrary")),
        name="sb_attention",
    )(qk, qk, vt, upper)


def _sortable_key(x):
    bits = pltpu.bitcast(x, I32)
    return bits ^ ((bits >> 31) & jnp.int32(0x7FFFFFFF))


def _dsa_body(q_ref, qi_ref, k_ref, ki_ref, vt_ref, wi_ref, lower_ref, o_ref,
              keys_sc, s_sc, m_sc, l_sc, acc_sc, *, tq, tk, topk):
    qb = pl.program_id(1)
    p = pl.program_id(2)
    nch = (qb * tq) // tk + 1
    kid = lax.broadcasted_iota(I32, (tk, tq), 0)
    qpos = qb * tq + lax.broadcasted_iota(I32, (tk, tq), 1)

    @pl.when(p == 0)
    def _select():
        def score_chunk(c, carry):
            start = pl.multiple_of(c * tk, tk)
            kidup = ki_ref[0, pl.ds(start, tk), :]
            score = jnp.zeros((tk, tq), F32)
            for hp in range(IDX_HEADS // 2):
                _, halves = _pair_halves(qi_ref[0, :, hp * LANES:(hp + 1) * LANES])
                for j in range(2):
                    h = 2 * hp + j
                    score = score + jnp.maximum(_dot_nt(kidup, halves[j]), 0.0) * wi_ref[0, h:h + 1, :]
            keys_sc[c] = jnp.where(c * tk + kid <= qpos, _sortable_key(score), jnp.int32(INT_MIN))
            return carry

        lax.fori_loop(0, nch, score_chunk, 0)

        def count_ge(thr):
            def body(c, cnts):
                kc = keys_sc[c]
                cnts = list(cnts)
                for j in range(tk // 8):
                    cnts[j % COUNT_CHAINS] = cnts[j % COUNT_CHAINS] + jnp.where(kc[j * 8:(j + 1) * 8, :] >= thr, 1, 0)
                return tuple(cnts)
            zero = jnp.zeros((8, tq), I32)
            cnts = lax.fori_loop(0, nch, body, (zero,) * COUNT_CHAINS)
            return jnp.sum(sum(cnts[1:], cnts[0]), axis=0, keepdims=True)

        def bit_step(i, thr):
            cand = thr + jnp.left_shift(jnp.int32(1), 31 - i)
            return jnp.where(count_ge(cand) >= topk, cand, thr)

        thr = lax.fori_loop(0, 32, bit_step, jnp.full((1, tq), INT_MIN, I32))
        n_ge = count_ge(thr)
        n_gt = count_ge(thr + 1)
        need = topk - n_gt
        tie = jnp.logical_and(n_ge - n_gt > need, thr > INT_MIN)
        any_tie = jnp.max(jnp.where(tie, 1, 0))

        def store_bias(c, sel):
            keys_sc[c] = pltpu.bitcast(jnp.where(sel, 0.0, -1e30).astype(F32), I32)

        @pl.when(any_tie == 0)
        def _no_ties():
            def body(c, carry):
                kc = keys_sc[c]
                store_bias(c, jnp.logical_and(kc >= thr, kc > INT_MIN))
                return carry
            lax.fori_loop(0, nch, body, 0)

        @pl.when(any_tie != 0)
        def _ties():
            needf = need.astype(F32)

            def body(c, seen):
                kc = keys_sc[c]
                eq = kc == thr
                eqf = jnp.where(eq, 1.0, 0.0)
                rank = _dot(lower_ref[...], eqf.astype(BF16)) + seen
                sel = jnp.logical_or(kc > thr, jnp.logical_and(eq, rank < needf))
                store_bias(c, jnp.logical_and(sel, kc > INT_MIN))
                return seen + jnp.sum(eqf, axis=0, keepdims=True)
            lax.fori_loop(0, nch, body, jnp.zeros((1, tq), F32))

    _, halves = _pair_halves(q_ref[0])
    q_both = jnp.concatenate(halves, axis=0)
    m_sc[...] = jnp.full(m_sc.shape, -1e30, F32)
    l_sc[...] = jnp.zeros(l_sc.shape, F32)
    acc_sc[...] = jnp.zeros(acc_sc.shape, F32)

    def scores(c):
        start = pl.multiple_of(c * tk, tk)
        return _dot_nt(k_ref[0, pl.ds(start, tk), :], q_both)

    def softmax_step(c, s_raw):
        bias = pltpu.bitcast(keys_sc[c], F32)
        s = s_raw + jnp.concatenate([bias, bias], axis=1)
        m_old = m_sc[...]
        m_new = jnp.maximum(m_old, jnp.max(s, axis=0, keepdims=True))
        alpha = jnp.exp2(m_old - m_new)
        pr = jnp.exp2(s - m_new)
        l_sc[...] = alpha * l_sc[...] + jnp.sum(pr, axis=0, keepdims=True)
        acc_sc[...] = alpha * acc_sc[...] + _dot(vt_ref[0, c], pr.astype(BF16))
        m_sc[...] = m_new

    s_sc[0] = scores(0)

    def attend(c, carry):
        s_raw = s_sc[c % 2]
        s_sc[(c + 1) % 2] = scores(c + 1)
        softmax_step(c, s_raw)
        return carry

    lax.fori_loop(0, nch - 1, attend, 0)
    softmax_step(nch - 1, s_sc[(nch - 1) % 2])
    o = acc_sc[...] / l_sc[...]
    o_ref[0] = _merge_heads_t(o[:, :tq], o[:, tq:]).astype(o_ref.dtype)


def dsa_attention(proj, vt, wit, tq, tk):
    B, S, _ = proj.shape
    topk = min(IDX_TOPK_MAX, S // 4)
    ar = jnp.arange(tk)
    lower = (ar[None, :] < ar[:, None]).astype(BF16)
    kern = functools.partial(_dsa_body, tq=tq, tk=tk, topk=topk)
    qi_blk = D_MODEL // (IDX_HEADS * IDX_DIM)
    k_blk = (D_MODEL + IDX_HEADS * IDX_DIM) // LANES
    nk = S // tk
    return pl.pallas_call(
        kern,
        grid=(B, S // tq, N_PAIRS),
        in_specs=[pl.BlockSpec((1, tq, LANES), lambda b, i, p: (b, i, p)),
                  pl.BlockSpec((1, tq, IDX_HEADS * IDX_DIM), lambda b, i, p: (b, i, qi_blk)),
                  pl.BlockSpec((1, S, LANES), lambda b, i, p: (b, 0, k_blk)),
                  pl.BlockSpec((1, S, LANES), lambda b, i, p: (b, 0, k_blk + 1)),
                  pl.BlockSpec((1, nk, LANES, tk), lambda b, i, p: (b, 0, 0, 0)),
                  pl.BlockSpec((1, LANES, tq), lambda b, i, p: (b, 0, i)),
                  pl.BlockSpec((tk, tk), lambda b, i, p: (0, 0))],
        out_specs=pl.BlockSpec((1, tq, LANES), lambda b, i, p: (b, i, p)),
        out_shape=jax.ShapeDtypeStruct((B, S, D_MODEL), BF16),
        scratch_shapes=[pltpu.VMEM((nk, tk, tq), I32), pltpu.VMEM((2, tk, 2 * tq), F32),
                        pltpu.VMEM((1, 2 * tq), F32), pltpu.VMEM((1, 2 * tq), F32),
                        pltpu.VMEM((LANES, 2 * tq), F32)],
        compiler_params=_cparams(("parallel", "arbitrary", "arbitrary")),
        name="dsa_attention",
    )(proj, proj, proj, proj, vt, wit, lower)


def _gla_body(qkvg_ref, a1_ref, wa2_ref, ba_ref, ng_ref, o_ref, state_sc, *, ts):
    C, H, dk, dv = GLA_CHUNK, GLA_HEADS, GLA_DK, GLA_DV

    @pl.when(pl.program_id(1) == 0)
    def _reset():
        state_sc[...] = jnp.zeros(state_sc.shape, F32)

    row = lax.broadcasted_iota(I32, (C, C), 0)
    colc = lax.broadcasted_iota(I32, (C, C), 1)
    causal = colc <= row
    tril = jnp.where(causal, 1.0, 0.0).astype(BF16)
    wa2 = wa2_ref[...].astype(BF16)
    ng = ng_ref[...]
    v_off, g_off = 2 * H * dk, 2 * H * dk + H * dv
    for n in range(ts // C):
        rows = slice(n * C, (n + 1) * C)
        za = _dot(a1_ref[0, rows, :].astype(BF16), wa2) + ba_ref[...]
        log_a = _log_sigmoid(za) / GLA_TAU
        h1, h2, h3 = _split3(log_a)
        bcum = _dot(tril, h1) + _dot(tril, h2) + _dot(tril, h3)
        eb = jnp.exp(bcum)
        enb = jnp.exp(-bcum)
        b_last = bcum[C - 1:C, :]
        eout = jnp.exp(b_last - bcum)
        dec = jnp.exp(b_last)
        for h in range(H):
            ks = slice(h * dk, (h + 1) * dk)
            q = qkvg_ref[0, rows, h * dk:(h + 1) * dk] * (dk ** -0.5)
            k = qkvg_ref[0, rows, H * dk + h * dk:H * dk + (h + 1) * dk]
            v = qkvg_ref[0, rows, v_off + h * dv:v_off + (h + 1) * dv].astype(BF16)
            q_in = (q * eb[:, ks]).astype(BF16)
            k_in = (k * enb[:, ks]).astype(BF16)
            k_out = (k * eout[:, ks]).astype(BF16)
            att = jnp.where(causal, _dot_nt(q_in, k_in), 0.0)
            state = state_sc[h]
            o = _dot(att.astype(BF16), v) + _dot(q_in, state.astype(BF16))
            dec_col = jnp.transpose(jnp.broadcast_to(dec[:, ks], (dk, dk)))[:, 0:1]
            state_sc[h] = state * dec_col + _dot_tn(k_out, v)
            o = o * lax.rsqrt(jnp.mean(o * o, axis=-1, keepdims=True) + RMS_EPS) * ng
            g = qkvg_ref[0, rows, g_off + h * dv:g_off + (h + 1) * dv]
            o = o * (g * jax.nn.sigmoid(g))
            o_ref[0, rows, h * dv:(h + 1) * dv] = o.astype(o_ref.dtype)


def gla_attention(qkvg, a1, w_a2p, b_a, norm_g, ts=512):
    B, S, W = qkvg.shape
    ts = min(ts, S)
    HK = GLA_HEADS * GLA_DK
    kern = functools.partial(_gla_body, ts=ts)
    return pl.pallas_call(
        kern,
        grid=(B, S // ts),
        in_specs=[pl.BlockSpec((1, ts, W), lambda b, i: (b, i, 0)),
                  pl.BlockSpec((1, ts, LANES), lambda b, i: (b, i, 0)),
                  pl.BlockSpec((LANES, HK), lambda b, i: (0, 0)),
                  pl.BlockSpec((1, HK), lambda b, i: (0, 0)),
                  pl.BlockSpec((1, GLA_DV), lambda b, i: (0, 0))],
        out_specs=pl.BlockSpec((1, ts, GLA_HEADS * GLA_DV), lambda b, i: (b, i, 0)),
        out_shape=jax.ShapeDtypeStruct((B, S, GLA_HEADS * GLA_DV), BF16),
        scratch_shapes=[pltpu.VMEM((GLA_HEADS, GLA_DK, GLA_DV), F32)],
        compiler_params=_cparams(("parallel", "arbitrary")),
        name="gla_attention",
    )(qkvg, a1, w_a2p, b_a.reshape(1, HK), norm_g.reshape(1, GLA_DV))


def _first_max(vals, ids, big):
    m = jnp.max(vals, axis=0, keepdims=True)
    first = jnp.min(jnp.where(vals == m, ids, big), axis=0, keepdims=True)
    return m, first


def _router_body(x_ref, wr_ref, br_ref, lower_ref, idx_ref, w_ref, rank_ref, cnt_ref, run_sc, *, tm):
    E, G, GS = N_EXPERTS, N_GROUPS, GROUP_SIZE

    @pl.when(pl.program_id(0) == 0)
    def _reset():
        run_sc[...] = jnp.zeros(run_sc.shape, F32)

    x = x_ref[...]
    x1 = x.astype(BF16)
    x2 = (x - x1.astype(F32)).astype(BF16)
    wr = wr_ref[...]
    w1 = wr.astype(BF16)
    w2 = (wr - w1.astype(F32)).astype(BF16)
    logits = _dot_nt(w1, x1) + _dot_nt(w1, x2) + _dot_nt(w2, x1)
    scores = jax.nn.sigmoid(logits)
    sel = scores + br_ref[...]
    neg = -jnp.inf
    eid = lax.broadcasted_iota(I32, (E, tm), 0)
    lid = lax.broadcasted_iota(I32, (GS, tm), 0)

    grp = []
    for g in range(G):
        sg = sel[g * GS:(g + 1) * GS, :]
        m1, f1 = _first_max(sg, lid, GS)
        m2 = jnp.max(jnp.where(lid == f1, neg, sg), axis=0, keepdims=True)
        grp.append(m1 + m2)
    grp = jnp.concatenate(grp, axis=0)
    gmask = jnp.zeros((G, tm), jnp.bool_)
    for _ in range(TOPK_GROUPS):
        _, f = _first_max(grp, lid, G)
        hit = lid == f
        gmask = jnp.logical_or(gmask, hit)
        grp = jnp.where(hit, neg, grp)
    emask = jnp.concatenate(
        [jnp.broadcast_to(gmask[g:g + 1, :], (GS, tm)) for g in range(G)], axis=0)
    sel = jnp.where(emask, sel, neg)

    chosen = jnp.zeros((E, tm), jnp.bool_)
    ids, ws = [], []
    for _ in range(TOP_K):
        _, f = _first_max(sel, eid, E)
        hit = eid == f
        ids.append(f)
        ws.append(jnp.sum(jnp.where(hit, scores, 0.0), axis=0, keepdims=True))
        chosen = jnp.logical_or(chosen, hit)
        sel = jnp.where(hit, neg, sel)
    ids = jnp.concatenate(ids, axis=0)
    ws = jnp.concatenate(ws, axis=0)
    ws = ws / jnp.sum(ws, axis=0, keepdims=True) * ROUTED_SCALE

    cf = jnp.where(chosen, 1.0, 0.0)
    before = _dot(cf.astype(BF16), lower_ref[...]) + run_sc[...]
    ranks = [jnp.sum(jnp.where(eid == ids[k:k + 1, :], before, 0.0), axis=0, keepdims=True)
             for k in range(TOP_K)]
    run_sc[...] = run_sc[...] + jnp.sum(cf, axis=1, keepdims=True)
    idx_ref[...] = ids
    w_ref[...] = ws
    rank_ref[...] = jnp.concatenate(ranks, axis=0).astype(I32)
    cnt_ref[...] = jnp.broadcast_to(run_sc[...], cnt_ref.shape).astype(I32)


def moe_router(x, w_router, b_router, tm=512):
    T, D = x.shape
    tm = min(tm, T)
    ar = jnp.arange(tm)
    lower = (ar[:, None] < ar[None, :]).astype(BF16)
    out = pl.BlockSpec((TOP_K, tm), lambda i: (0, i))
    return pl.pallas_call(
        functools.partial(_router_body, tm=tm),
        grid=(T // tm,),
        in_specs=[pl.BlockSpec((tm, D), lambda i: (i, 0)),
                  pl.BlockSpec((N_EXPERTS, D), lambda i: (0, 0)),
                  pl.BlockSpec((N_EXPERTS, 1), lambda i: (0, 0)),
                  pl.BlockSpec((tm, tm), lambda i: (0, 0))],
        out_specs=[out, out, out, pl.BlockSpec((N_EXPERTS, LANES), lambda i: (0, 0))],
        out_shape=[jax.ShapeDtypeStruct((TOP_K, T), I32), jax.ShapeDtypeStruct((TOP_K, T), F32),
                   jax.ShapeDtypeStruct((TOP_K, T), I32),
                   jax.ShapeDtypeStruct((N_EXPERTS, LANES), I32)],
        scratch_shapes=[pltpu.VMEM((N_EXPERTS, 1), F32)],
        compiler_params=_cparams(("arbitrary",)),
        name="moe_router",
    )(x, w_router.T, b_router.reshape(N_EXPERTS, 1), lower)


HALF = D_MODEL // 2


def _pack_rows(x):
    lo = pltpu.bitcast(x[:, :HALF].astype(BF16).astype(F32), I32)
    hi = pltpu.bitcast(x[:, HALF:].astype(BF16).astype(F32), I32)
    return lax.shift_right_logical(lo, 16) | (hi & jnp.int32(-65536))


def _unpack_rows(w):
    return pltpu.bitcast(w << 16, F32), pltpu.bitcast(w & jnp.int32(-65536), F32)


def _dispatch_body(x_ref, dest_hbm, xs_in, xs_hbm, idx_sm, pack_sc, sem_idx, sem_row, *, tm, nt):
    del xs_in
    i = pl.program_id(0)
    slot = i % 2
    n_rows = tm * TOP_K

    def idx_copy(tile, s):
        return pltpu.make_async_copy(dest_hbm.at[tile], idx_sm.at[s], sem_idx.at[s])

    def row_copy(s, t, row):
        return pltpu.make_async_copy(pack_sc.at[s, pl.ds(t, 1)], xs_hbm.at[pl.ds(row, 1)], sem_row.at[s])

    @pl.when(i == 0)
    def _first():
        idx_copy(0, 0).start()

    @pl.when(i + 1 < nt)
    def _prefetch():
        idx_copy(i + 1, 1 - slot).start()

    idx_copy(i, slot).wait()
    pack_sc[slot] = _pack_rows(x_ref[...])

    for t in range(tm):
        for k in range(TOP_K):
            row_copy(slot, t, idx_sm[slot, t * TOP_K + k]).start(priority=k % 2)

    def drain(s):
        pltpu.make_async_copy(xs_hbm.at[pl.ds(0, n_rows)], xs_hbm.at[pl.ds(n_rows, n_rows)], sem_row.at[s]).wait()

    @pl.when(i > 0)
    def _drain_prev():
        drain(1 - slot)

    @pl.when(i == nt - 1)
    def _drain_last():
        drain(slot)


def moe_dispatch(x, dest_tiles, xs_init, tm):
    T, D = x.shape
    nt = T // tm
    n_rows_out = xs_init.shape[0]
    xs0 = xs_init
    return pl.pallas_call(
        functools.partial(_dispatch_body, tm=tm, nt=nt),
        grid=(nt,),
        in_specs=[pl.BlockSpec((tm, D), lambda i: (i, 0)),
                  pl.BlockSpec(memory_space=pl.ANY),
                  pl.BlockSpec(memory_space=pl.ANY)],
        out_specs=pl.BlockSpec(memory_space=pl.ANY),
        out_shape=jax.ShapeDtypeStruct((n_rows_out, HALF), I32),
        scratch_shapes=[pltpu.SMEM((2, tm * TOP_K), I32), pltpu.VMEM((2, tm, HALF), I32),
                        pltpu.SemaphoreType.DMA((2,)), pltpu.SemaphoreType.DMA((2,))],
        input_output_aliases={2: 0},
        compiler_params=_cparams(("arbitrary",)),
        name="moe_dispatch",
    )(x, dest_tiles, xs0)


def _expert_body(be_ref, nreal_ref, x_ref, wgu_ref, wd_ref, o_ref):
    @pl.when(pl.program_id(0) < nreal_ref[0])
    def _():
        lo, hi = _unpack_rows(x_ref[...])
        gu = _dot(lo.astype(BF16), wgu_ref[0, :HALF, :]) + _dot(hi.astype(BF16), wgu_ref[0, HALF:, :])
        gt, up = gu[:, :D_EXPERT], gu[:, D_EXPERT:]
        h = gt * jax.nn.sigmoid(gt) * up
        o_ref[...] = _pack_rows(_dot(h.astype(BF16), wd_ref[0]))

    @pl.when(pl.program_id(0) >= nreal_ref[0])
    def _():
        o_ref[...] = jnp.zeros(o_ref.shape, I32)


def moe_experts(xs, block_exp, n_real, w_gu, w_down):
    P = xs.shape[0]
    D, M = D_MODEL, EXPERT_BLOCK
    grid_spec = pltpu.PrefetchScalarGridSpec(
        num_scalar_prefetch=2,
        grid=(P // M,),
        in_specs=[pl.BlockSpec((M, HALF), lambda i, be, nr: (i, 0)),
                  pl.BlockSpec((1, D, 2 * D_EXPERT), lambda i, be, nr: (be[i], 0, 0)),
                  pl.BlockSpec((1, D_EXPERT, D), lambda i, be, nr: (be[i], 0, 0))],
        out_specs=pl.BlockSpec((M, HALF), lambda i, be, nr: (i, 0)),
    )
    return pl.pallas_call(
        _expert_body,
        grid_spec=grid_spec,
        out_shape=jax.ShapeDtypeStruct((P, HALF), I32),
        compiler_params=_cparams(("arbitrary",)),
        name="moe_experts",
    )(block_exp, n_real, xs, w_gu, w_down)


def _ffn_tail_body(x_ref, dest_hbm, ys_hbm, w_ref, p_ref, wsg_ref, wsd_ref, g_ref, b_ref, wg_ref, bg_ref, wp_ref,
                   o_ref, idx_sm, y_sc, sem_idx, sem_row, *, tm, nt):
    i = pl.program_id(0)
    slot = i % 2
    n_rows = tm * TOP_K

    def idx_copy(tile, s):
        return pltpu.make_async_copy(dest_hbm.at[tile], idx_sm.at[s], sem_idx.at[s])

    def row_copy(s, k, t, row):
        return pltpu.make_async_copy(ys_hbm.at[pl.ds(row, 1)], y_sc.at[s, pl.ds(k * tm + t, 1)], sem_row.at[s])

    def gather(s):
        for t in range(tm):
            for k in range(TOP_K):
                row_copy(s, k, t, idx_sm[s, t * TOP_K + k]).start(priority=k % 2)

    @pl.when(i == 0)
    def _first():
        idx_copy(0, 0).start()
        idx_copy(0, 0).wait()
        gather(0)

        @pl.when(nt > 1)
        def _():
            idx_copy(1, 1).start()

    @pl.when(i + 1 < nt)
    def _next():
        idx_copy(i + 1, 1 - slot).wait()
        gather(1 - slot)

        @pl.when(i + 2 < nt)
        def _():
            idx_copy(i + 2, slot).start()

    x = x_ref[...]
    gu = _dot(x.astype(BF16), wsg_ref[...].astype(BF16))
    gs, us = gu[:, :D_SHARED], gu[:, D_SHARED:]
    shared = _dot((gs * jax.nn.sigmoid(gs) * us).astype(BF16), wsd_ref[...].astype(BF16))
    pe = _dot(p_ref[...].astype(BF16), wp_ref[...].astype(BF16))

    pltpu.make_async_copy(ys_hbm.at[pl.ds(0, n_rows)], y_sc.at[slot], sem_row.at[slot]).wait()

    r_lo = jnp.zeros((tm, HALF), F32)
    r_hi = jnp.zeros((tm, HALF), F32)
    for k in range(TOP_K):
        lo, hi = _unpack_rows(y_sc[slot, k * tm:(k + 1) * tm, :])
        wk = w_ref[:, k:k + 1]
        r_lo = r_lo + lo * wk
        r_hi = r_hi + hi * wk
    routed = jnp.concatenate([r_lo, r_hi], axis=1)
    x2 = _layer_norm(DEEPNORM_ALPHA * x + (routed + shared), g_ref[...], b_ref[...])
    gate = jax.nn.sigmoid(_dot(x2.astype(BF16), wg_ref[...].astype(BF16)) + bg_ref[...])
    o_ref[...] = x2 + gate * pe


def ffn_tail(x, dest_tiles, ys, wt, p, ws_gu, ws_down, g, b, w_gate, b_gate, w_proj, tm):
    T, D = x.shape
    nt = T // tm
    full = lambda shape: pl.BlockSpec(shape, lambda i: tuple(0 for _ in shape))
    return pl.pallas_call(
        functools.partial(_ffn_tail_body, tm=tm, nt=nt),
        grid=(nt,),
        in_specs=[pl.BlockSpec((tm, D), lambda i: (i, 0)),
                  pl.BlockSpec(memory_space=pl.ANY),
                  pl.BlockSpec(memory_space=pl.ANY),
                  pl.BlockSpec((tm, TOP_K), lambda i: (i, 0)),
                  pl.BlockSpec((tm, PLE_DIM), lambda i: (i, 0)),
                  full((D, 2 * D_SHARED)), full((D_SHARED, D)),
                  full((1, D)), full((1, D)), full((D, D)), full((1, D)), full((PLE_DIM, D))],
        out_specs=pl.BlockSpec((tm, D), lambda i: (i, 0)),
        out_shape=jax.ShapeDtypeStruct((T, D), F32),
        scratch_shapes=[pltpu.SMEM((2, tm * TOP_K), I32), pltpu.VMEM((2, TOP_K * tm, HALF), I32),
                        pltpu.SemaphoreType.DMA((2,)), pltpu.SemaphoreType.DMA((2,))],
        compiler_params=_cparams(("arbitrary",)),
        name="ffn_tail",
    )(x, dest_tiles, ys, wt, p, ws_gu, ws_down, g.reshape(1, D), b.reshape(1, D), w_gate,
      b_gate.reshape(1, D), w_proj)


def _pad_cols(w, n):
    return jnp.pad(w, ((0, 0), (0, n - w.shape[1])))


def _qk_weights(wq, wk, q_scale=HEAD_DIM ** -0.5):
    return jnp.concatenate([wq * q_scale, wk], axis=1)


def _fox_layer(x, w_in, b_f, t):
    B, S, D = x.shape
    xt = x.reshape(B * S, D)
    qk = matmul(xt, _qk_weights(w_in[:, :D], w_in[:, D:2 * D], LOG2E * HEAD_DIM ** -0.5), BF16).reshape(B, S, 2 * D)
    vt = matmul_t_blocks(xt, w_in[:, 2 * D:3 * D].T, B, t, BF16)
    f = matmul(xt, _pad_cols(w_in[:, 3 * D:], LANES), F32).reshape(B, S, LANES)
    caug, cbs = fox_gates(f, b_f, t)
    cb = cbs[:, :, 0, :N_HEADS].transpose(0, 2, 1).reshape(-1)
    return fox_attention(qk, vt, caug, cb, t).reshape(B * S, D)


def _sb_layer(x, w_in, t):
    B, S, D = x.shape
    xt = x.reshape(B * S, D)
    qk = matmul(xt, _qk_weights(w_in[:, :D], w_in[:, D:2 * D], LOG2E * HEAD_DIM ** -0.5), BF16).reshape(B, S, 2 * D)
    vt = matmul_t_blocks(xt, w_in[:, 2 * D:].T, B, t, BF16)
    return sb_attention(qk, vt, t).reshape(B * S, D)


def _rope_tables(positions):
    half = ROPE_DIM // 2
    inv_freq = jnp.exp(-math.log(ROPE_THETA) * 2.0 * jnp.arange(half, dtype=F32) / ROPE_DIM)
    ang = positions.astype(F32).reshape(-1, 1) * inv_freq
    cos, sin = jnp.cos(ang), jnp.sin(ang)
    T = ang.shape[0]
    rest = HEAD_DIM - ROPE_DIM
    cos64 = jnp.concatenate([cos, cos, jnp.ones((T, rest), F32)], axis=1)
    sa64 = jnp.concatenate([-sin, jnp.zeros((T, HEAD_DIM - half), F32)], axis=1)
    sb64 = jnp.concatenate([jnp.zeros((T, half), F32), sin, jnp.zeros((T, rest), F32)], axis=1)
    return tuple(jnp.tile(a, (1, LANES // HEAD_DIM)) for a in (cos64, sa64, sb64))


def _dsa_layer(x, positions, w_in, tq, tk):
    B, S, D = x.shape
    xt = x.reshape(B * S, D)
    dh = HEAD_DIM
    o_k, o_v, o_qi = D, D + dh, D + 2 * dh
    o_ki = o_qi + IDX_HEADS * IDX_DIM
    o_wi = o_ki + IDX_DIM
    wq, wk, wv = w_in[:, :o_k], w_in[:, o_k:o_v], w_in[:, o_v:o_qi]
    wqi, wki, wwi = w_in[:, o_qi:o_ki], w_in[:, o_ki:o_wi], w_in[:, o_wi:]
    w_rope = jnp.concatenate([wq * (LOG2E * dh ** -0.5), wqi, wk, wk, wki, wki], axis=1)
    cos, sa, sb = _rope_tables(positions)
    proj = matmul_rope(xt, w_rope, cos, sa, sb, BF16).reshape(B, S, -1)
    vt = matmul_t_blocks(xt, jnp.concatenate([wv, wv], axis=1).T, B, tk, BF16).reshape(B, S // tk, LANES, tk)
    wit = matmul_t_rows(xt, _pad_cols(wwi * (IDX_HEADS ** -0.5 * IDX_DIM ** -0.5), LANES).T, B, F32)
    return dsa_attention(proj, vt, wit, tq, tk).reshape(B * S, D)


def _gla_layer(x, w_in, w_a2, b_a, norm_g):
    B, S, D = x.shape
    xt = x.reshape(B * S, D)
    W = 2 * GLA_HEADS * GLA_DK + 2 * GLA_HEADS * GLA_DV
    qkvg = matmul(xt, w_in[:, :W], F32).reshape(B, S, W)
    a1 = matmul(xt, _pad_cols(w_in[:, W:], LANES), F32).reshape(B, S, LANES)
    w_a2p = jnp.pad(w_a2, ((0, LANES - GLA_GATE_RANK), (0, 0)))
    return gla_attention(qkvg, a1, w_a2p, b_a, norm_g).reshape(B * S, GLA_HEADS * GLA_DV)


MOE_TOKEN_TILE = 256


def _moe_layer(x, p_i, w_router, b_router, w_gu, w_down, ws_gu, ws_down, g, b, w_gate, b_gate, w_proj, xs_buf=None):
    T, D = x.shape
    E, M = N_EXPERTS, EXPERT_BLOCK
    tm = min(MOE_TOKEN_TILE, T)
    idx, wts, rank, cnt = moe_router(x, w_router, b_router)
    counts = cnt[:, 0]
    padded = (counts + M - 1) // M * M
    pend = jnp.cumsum(padded)
    poff = pend - padded
    eids = jnp.arange(E, dtype=I32)[:, None, None]
    dest = rank + jnp.sum(jnp.where(idx[None] == eids, poff.astype(I32)[:, None, None], 0), axis=0)
    dest_tiles = dest.T.reshape(T // tm, tm * TOP_K)
    P = (T * TOP_K + M - 1) // M * M + E * M
    nb = P // M
    starts = jnp.arange(nb, dtype=I32) * M
    block_exp = jnp.minimum(jnp.sum((pend.astype(I32)[None, :] <= starts[:, None]).astype(I32), axis=1), E - 1)
    n_real = (pend[-1] // M).astype(I32).reshape(1)
    if xs_buf is None:
        xs_buf = jnp.zeros((P, HALF), I32)
    xs = moe_dispatch(x, dest_tiles, xs_buf, tm)
    ys = moe_experts(xs, block_exp, n_real, w_gu, w_down)
    return ffn_tail(x, dest_tiles, ys, wts.T, p_i, ws_gu, ws_down, g, b, w_gate, b_gate, w_proj, tm), xs


def kernel(x, p, positions, ln1_g, ln1_b, ln2_g, ln2_b, fox_w_in, fox_b_f, fox_w_out, sb_w_in, sb_w_out, dsa_w_in, dsa_w_out, gla_w_in, gla_w_a2, gla_b_a, gla_norm_g, gla_w_out, moe_w_router, moe_b_router, moe_w_gu, moe_w_down, shared_w_gu, shared_w_down, ple_w_proj, ple_w_gate, ple_b_gate):
    B, S, D = x.shape
    T = B * S
    depth = p.shape[0]
    t_attn = min(512, S)
    w_gu_bf = moe_w_gu.astype(BF16)
    w_down_bf = moe_w_down.astype(BF16)
    xt = x.reshape(T, D)
    xs_buf = None
    for i in range(depth):
        m, j = i % 4, i // 4
        xb = xt.reshape(B, S, D)
        if m == 0:
            a, w_out = _fox_layer(xb, fox_w_in[j], fox_b_f[j], t_attn), fox_w_out[j]
        elif m == 1:
            a, w_out = _sb_layer(xb, sb_w_in[j], t_attn), sb_w_out[j]
        elif m == 2:
            a, w_out = _dsa_layer(xb, positions, dsa_w_in[j], min(256, S), min(512, S)), dsa_w_out[j]
        else:
            a, w_out = _gla_layer(xb, gla_w_in[j], gla_w_a2[j], gla_b_a[j], gla_norm_g[j]), gla_w_out[j]
        xt = matmul_residual_ln(a, w_out, xt, ln1_g[i], ln1_b[i])
        xt, xs_buf = _moe_layer(xt, p[i].reshape(T, -1), moe_w_router[i], moe_b_router[i], w_gu_bf[i], w_down_bf[i],
                                shared_w_gu[i], shared_w_down[i], ln2_g[i], ln2_b[i],
                                ple_w_gate[i], ple_b_gate[i], ple_w_proj[i], xs_buf)
    return xt.reshape(B, S, D)
```

```python
import functools
import math

import jax
import jax.numpy as jnp
from jax import lax
from jax.experimental import pallas as pl
from jax.experimental.pallas import tpu as pltpu

F32 = jnp.float32
BF16 = jnp.bfloat16
I32 = jnp.int32

D_MODEL = 1024
HEAD_DIM = 64
N_HEADS = D_MODEL // HEAD_DIM
N_PAIRS = N_HEADS // 2
LANES = 128
ROPE_DIM = HEAD_DIM // 4
ROPE_THETA = 500000.0
IDX_HEADS = 8
IDX_DIM = HEAD_DIM
IDX_TOPK_MAX = 256
GLA_HEADS = 4
GLA_DK = D_MODEL // 2 // GLA_HEADS
GLA_DV = D_MODEL // GLA_HEADS
GLA_GATE_RANK = 16
GLA_TAU = 16.0
GLA_CHUNK = 64
N_EXPERTS = 64
TOP_K = 8
N_GROUPS = 8
GROUP_SIZE = N_EXPERTS // N_GROUPS
TOPK_GROUPS = 4
D_EXPERT = 256
D_SHARED = 256
ROUTED_SCALE = 2.5
EXPERT_BLOCK = 1024
PLE_DIM = 256
DEPTH = 4
DEEPNORM_ALPHA = (2 * DEPTH) ** 0.25
LN_EPS = 1e-5
RMS_EPS = 1e-6
INT_MIN = -(2 ** 31)
LOG2E = 1.4426950408889634
COUNT_CHAINS = 4
SB_SUB = 256
VMEM_LIMIT = 56 * 1024 * 1024


def _cparams(sem):
    return pltpu.CompilerParams(dimension_semantics=sem, vmem_limit_bytes=VMEM_LIMIT)


def _pick(n, cands):
    for c in cands:
        if n % c == 0:
            return c
    raise ValueError(f"no tile for {n}")


def _dot(a, b):
    return jnp.dot(a, b, preferred_element_type=F32)


def _dot_nt(a, b):
    return lax.dot_general(a, b, (((1,), (1,)), ((), ())), preferred_element_type=F32)


def _dot_tn(a, b):
    return lax.dot_general(a, b, (((0,), (0,)), ((), ())), preferred_element_type=F32)


def _split3(x):
    h1 = x.astype(BF16)
    r1 = x - h1.astype(F32)
    h2 = r1.astype(BF16)
    h3 = (r1 - h2.astype(F32)).astype(BF16)
    return h1, h2, h3


def _log_sigmoid(x):
    return jnp.minimum(x, 0.0) - jnp.log1p(jnp.exp(-jnp.abs(x)))


def _mm_body(x_ref, w_ref, o_ref):
    o_ref[...] = _dot(x_ref[...].astype(BF16), w_ref[...].astype(BF16)).astype(o_ref.dtype)


def matmul(x, w, out_dtype, tm=512):
    M, K = x.shape
    N = w.shape[1]
    tm = min(tm, M)
    tn = _pick(N, (512, 384, 256, 128))
    return pl.pallas_call(
        _mm_body,
        grid=(M // tm, N // tn),
        in_specs=[pl.BlockSpec((tm, K), lambda i, j: (i, 0)),
                  pl.BlockSpec((K, tn), lambda i, j: (0, j))],
        out_specs=pl.BlockSpec((tm, tn), lambda i, j: (i, j)),
        out_shape=jax.ShapeDtypeStruct((M, N), out_dtype),
        compiler_params=_cparams(("parallel", "parallel")),
        name="matmul",
    )(x, w)


def _mm_rope_body(x_ref, w_ref, cos_ref, sa_ref, sb_ref, o_ref):
    y = _dot(x_ref[...].astype(BF16), w_ref[...].astype(BF16))
    cos, sa, sb = cos_ref[...], sa_ref[...], sb_ref[...]
    for c in range(y.shape[1] // LANES):
        yc = y[:, c * LANES:(c + 1) * LANES]
        oc = yc * cos + pltpu.roll(yc, LANES - ROPE_DIM // 2, 1) * sa + pltpu.roll(yc, ROPE_DIM // 2, 1) * sb
        o_ref[:, c * LANES:(c + 1) * LANES] = oc.astype(o_ref.dtype)


def matmul_rope(x, w, cos, sa, sb, out_dtype, tm=512):
    M, K = x.shape
    N = w.shape[1]
    tm = min(tm, M)
    tn = _pick(N, (256, 128))
    tab = pl.BlockSpec((tm, LANES), lambda i, j: (i, 0))
    return pl.pallas_call(
        _mm_rope_body,
        grid=(M // tm, N // tn),
        in_specs=[pl.BlockSpec((tm, K), lambda i, j: (i, 0)),
                  pl.BlockSpec((K, tn), lambda i, j: (0, j)), tab, tab, tab],
        out_specs=pl.BlockSpec((tm, tn), lambda i, j: (i, j)),
        out_shape=jax.ShapeDtypeStruct((M, N), out_dtype),
        compiler_params=_cparams(("parallel", "parallel")),
        name="matmul_rope",
    )(x, w, cos, sa, sb)


def _layer_norm(y, g, b):
    mu = jnp.mean(y, axis=-1, keepdims=True)
    d = y - mu
    var = jnp.mean(d * d, axis=-1, keepdims=True)
    return d * lax.rsqrt(var + LN_EPS) * g + b


def _mm_res_ln_body(a_ref, w_ref, x_ref, g_ref, b_ref, o_ref):
    h = _dot(a_ref[...].astype(BF16), w_ref[...].astype(BF16))
    o_ref[...] = _layer_norm(DEEPNORM_ALPHA * x_ref[...] + h, g_ref[...], b_ref[...])


def matmul_residual_ln(a, w, x, g, b, tm=512):
    M, K = a.shape
    N = w.shape[1]
    tm = min(tm, M)
    return pl.pallas_call(
        _mm_res_ln_body,
        grid=(M // tm,),
        in_specs=[pl.BlockSpec((tm, K), lambda i: (i, 0)),
                  pl.BlockSpec((K, N), lambda i: (0, 0)),
                  pl.BlockSpec((tm, N), lambda i: (i, 0)),
                  pl.BlockSpec((1, N), lambda i: (0, 0)),
                  pl.BlockSpec((1, N), lambda i: (0, 0))],
        out_specs=pl.BlockSpec((tm, N), lambda i: (i, 0)),
        out_shape=jax.ShapeDtypeStruct((M, N), F32),
        compiler_params=_cparams(("parallel",)),
        name="matmul_residual_ln",
    )(a, w, x, g.reshape(1, N), b.reshape(1, N))


def _mm_t_blocks_body(wt_ref, x_ref, o_ref):
    xb = x_ref[...].astype(BF16)
    for j in range(o_ref.shape[1]):
        wj = wt_ref[j * LANES:(j + 1) * LANES, :].astype(BF16)
        o_ref[0, j, 0] = _dot_nt(wj, xb).astype(o_ref.dtype)


def _mm_t_rows_body(wt_ref, x_ref, o_ref):
    o_ref[0] = _dot_nt(wt_ref[...].astype(BF16), x_ref[...].astype(BF16)).astype(o_ref.dtype)


def matmul_t_blocks(xt, wt, B, t, out_dtype):
    T, K = xt.shape
    nk = T // B // t
    nj = wt.shape[0] // LANES
    return pl.pallas_call(
        _mm_t_blocks_body,
        grid=(B, nk),
        in_specs=[pl.BlockSpec((nj * LANES, K), lambda b, i: (0, 0)),
                  pl.BlockSpec((t, K), lambda b, i: (b * nk + i, 0))],
        out_specs=pl.BlockSpec((1, nj, 1, LANES, t), lambda b, i: (b, 0, i, 0, 0)),
        out_shape=jax.ShapeDtypeStruct((B, nj, nk, LANES, t), out_dtype),
        compiler_params=_cparams(("parallel", "parallel")),
        name="matmul_t_blocks",
    )(wt, xt)


def matmul_t_rows(xt, wt, B, out_dtype, tm=512):
    T, K = xt.shape
    S = T // B
    tm = min(tm, S)
    ns = S // tm
    return pl.pallas_call(
        _mm_t_rows_body,
        grid=(B, ns),
        in_specs=[pl.BlockSpec((LANES, K), lambda b, i: (0, 0)),
                  pl.BlockSpec((tm, K), lambda b, i: (b * ns + i, 0))],
        out_specs=pl.BlockSpec((1, LANES, tm), lambda b, i: (b, 0, i)),
        out_shape=jax.ShapeDtypeStruct((B, LANES, S), out_dtype),
        compiler_params=_cparams(("parallel", "parallel")),
        name="matmul_t_rows",
    )(wt, xt)


def _fox_gate_body(f_ref, bf_ref, tril_ref, place_ref, caug_ref, cb_ref, carry_sc, *, t):
    @pl.when(pl.program_id(1) == 0)
    def _reset():
        carry_sc[...] = jnp.zeros(carry_sc.shape, F32)

    log_f = _log_sigmoid(f_ref[0] + bf_ref[...])
    tril = tril_ref[...]
    h1, h2, h3 = _split3(log_f)
    c = _dot(tril, h1) + _dot(tril, h2) + _dot(tril, h3) + carry_sc[...]
    carry_sc[...] = c[t - 1:t, :]
    c = c * LOG2E
    c0 = c[0:1, :]
    d1, d2, d3 = _split3(c0 - c)
    caug = _dot(d1, place_ref[0]) + _dot(d2, place_ref[1]) + _dot(d3, place_ref[2])
    caug_ref[0] = caug.astype(BF16)
    cb_ref[0, 0] = c0


def fox_gates(f, b_f, t):
    B, S, _ = f.shape
    nk = S // t
    ar = jnp.arange(t)
    tril = (ar[None, :] <= ar[:, None]).astype(BF16)
    hh = jnp.arange(LANES)
    place = jnp.stack([((hh[None, :] == 3 * hh[:, None] + j) & (hh[:, None] < N_HEADS)).astype(BF16)
                       for j in range(3)])
    return pl.pallas_call(
        functools.partial(_fox_gate_body, t=t),
        grid=(B, nk),
        in_specs=[pl.BlockSpec((1, t, LANES), lambda b, i: (b, i, 0)),
                  pl.BlockSpec((1, LANES), lambda b, i: (0, 0)),
                  pl.BlockSpec((t, t), lambda b, i: (0, 0)),
                  pl.BlockSpec((3, LANES, LANES), lambda b, i: (0, 0, 0))],
        out_specs=[pl.BlockSpec((1, t, LANES), lambda b, i: (b, i, 0)),
                   pl.BlockSpec((1, 1, 1, LANES), lambda b, i: (b, i, 0, 0))],
        out_shape=[jax.ShapeDtypeStruct((B, S, LANES), BF16),
                   jax.ShapeDtypeStruct((B, nk, 1, LANES), F32)],
        scratch_shapes=[pltpu.VMEM((1, LANES), F32)],
        compiler_params=_cparams(("parallel", "arbitrary")),
        name="fox_gates",
    )(f, _pad_cols(b_f.reshape(1, -1), LANES), tril, place)


def _pair_halves(q2):
    lane = lax.broadcasted_iota(I32, q2.shape, 1)
    lo = lane < HEAD_DIM
    zero = jnp.zeros_like(q2)
    return lane, (jnp.where(lo, q2, zero), jnp.where(lo, zero, q2))


def _merge_heads_t(o_a, o_b):
    sub = lax.broadcasted_iota(I32, o_a.shape, 0)
    return jnp.transpose(jnp.where(sub < HEAD_DIM, o_a, o_b))


def _fox_body(cb_ref, q_ref, k_ref, vt_ref, caug_ref, o_ref, s_sc, m_sc, l_sc, acc_sc, *, t, nk):
    b, p, qb = pl.program_id(0), pl.program_id(1), pl.program_id(2)
    q2 = q_ref[0]
    lane, halves = _pair_halves(q2)
    rows = []
    for h in range(2):
        first = 3 * (2 * p + h)
        pick = jnp.logical_and(lane >= first, lane < first + 3)
        ones = jnp.where(pick, 1.0, 0.0).astype(BF16)
        rows.append(jnp.concatenate([halves[h], ones], axis=1))
    q_both = jnp.concatenate(rows, axis=0)
    m_sc[...] = jnp.full(m_sc.shape, -jnp.inf, F32)
    l_sc[...] = jnp.zeros(l_sc.shape, F32)
    acc_sc[...] = jnp.zeros(acc_sc.shape, F32)
    bases = [(b * N_HEADS + 2 * p + h) * nk for h in range(2)]

    def scores(kb):
        start = pl.multiple_of(kb * t, t)
        k_aug = jnp.concatenate([k_ref[0, pl.ds(start, t), :], caug_ref[0, pl.ds(start, t), :]], axis=1)
        return _dot_nt(k_aug, q_both)

    def softmax_step(kb, s, masked):
        if masked:
            kid = lax.broadcasted_iota(I32, (t, 2 * t), 0)
            qid = lax.broadcasted_iota(I32, (t, 2 * t), 1)
            qid = jnp.where(qid >= t, qid - t, qid)
            s = jnp.where(kid <= qid, s, -jnp.inf)
        off = jnp.concatenate([jnp.full((1, t), cb_ref[bs + qb] - cb_ref[bs + kb], F32) for bs in bases], axis=1)
        m_old = m_sc[...]
        m_new = jnp.maximum(m_old, jnp.max(s, axis=0, keepdims=True) + off)
        alpha = jnp.exp2(m_old - m_new)
        pr = jnp.exp2(s - (m_new - off))
        l_sc[...] = alpha * l_sc[...] + jnp.sum(pr, axis=0, keepdims=True)
        acc_sc[...] = alpha * acc_sc[...] + _dot(vt_ref[0, 0, kb], pr.astype(BF16))
        m_sc[...] = m_new

    s_sc[0] = scores(0)

    def full_block(kb, carry):
        s = s_sc[kb % 2]
        s_sc[(kb + 1) % 2] = scores(kb + 1)
        softmax_step(kb, s, False)
        return carry

    lax.fori_loop(0, qb, full_block, 0)
    softmax_step(qb, s_sc[qb % 2], True)
    o = acc_sc[...] / l_sc[...]
    o_ref[0] = _merge_heads_t(o[:, :t], o[:, t:]).astype(o_ref.dtype)


def fox_attention(qk, vt, caug, cb, t):
    B, S, _ = qk.shape
    nk = S // t
    grid_spec = pltpu.PrefetchScalarGridSpec(
        num_scalar_prefetch=1,
        grid=(B, N_PAIRS, nk),
        in_specs=[pl.BlockSpec((1, t, LANES), lambda b, p, i, cb: (b, i, p)),
                  pl.BlockSpec((1, S, LANES), lambda b, p, i, cb: (b, 0, N_PAIRS + p)),
                  pl.BlockSpec((1, 1, nk, LANES, t), lambda b, p, i, cb: (b, p, 0, 0, 0)),
                  pl.BlockSpec((1, S, LANES), lambda b, p, i, cb: (b, 0, 0))],
        out_specs=pl.BlockSpec((1, t, LANES), lambda b, p, i, cb: (b, i, p)),
        scratch_shapes=[pltpu.VMEM((2, t, 2 * t), F32), pltpu.VMEM((1, 2 * t), F32), pltpu.VMEM((1, 2 * t), F32),
                        pltpu.VMEM((LANES, 2 * t), F32)],
    )
    return pl.pallas_call(
        functools.partial(_fox_body, t=t, nk=nk),
        grid_spec=grid_spec,
        out_shape=jax.ShapeDtypeStruct((B, S, D_MODEL), BF16),
        compiler_params=_cparams(("parallel", "parallel", "arbitrary")),
        name="fox_attention",
    )(cb, qk, qk, vt, caug)


def _sb_body(q_ref, k_ref, vt_ref, upper_ref, o_ref, after_sc, acc_sc, *, t):
    qb = pl.program_id(2)
    _, halves = _pair_halves(q_ref[0])
    q_both = jnp.concatenate(halves, axis=0)
    after_sc[...] = jnp.zeros(after_sc.shape, F32)
    acc_sc[...] = jnp.zeros(acc_sc.shape, F32)
    neg_upper = upper_ref[...]

    def scores(kb):
        start = pl.multiple_of(kb * t, t)
        return _dot_nt(k_ref[0, pl.ds(start, t), :], q_both)

    def weigh(kb, z, masked):
        sp = jnp.maximum(z, 0.0) + jnp.log2(1.0 + jnp.exp2(-jnp.abs(z)))
        log_beta = z - sp
        if masked:
            kid = lax.broadcasted_iota(I32, (t, 2 * t), 0)
            qid = lax.broadcasted_iota(I32, (t, 2 * t), 1)
            strict = kid < jnp.where(qid >= t, qid - t, qid)
            sp = jnp.where(strict, sp, 0.0)
        spb = sp.astype(BF16)
        sub = neg_upper.shape[0]
        nsub = t // sub
        sums = [jnp.sum(sp[j * sub:(j + 1) * sub, :], axis=0, keepdims=True) for j in range(nsub)]
        later = after_sc[...]
        rests = [None] * nsub
        for j in reversed(range(nsub)):
            rests[j] = _dot(neg_upper, spb[j * sub:(j + 1) * sub, :]) + later
            later = later - sums[j]
        a = jnp.exp2(log_beta + jnp.concatenate(rests, axis=0))
        if masked:
            a = jnp.where(strict, a, 0.0)
        acc_sc[...] = acc_sc[...] + _dot(vt_ref[0, 0, kb], a.astype(BF16))
        after_sc[...] = later

    weigh(qb, scores(qb), True)

    def step(j, carry):
        kb = qb - 1 - j
        weigh(kb, scores(kb), False)
        return carry

    lax.fori_loop(0, qb, step, 0)
    o = acc_sc[...]
    o_ref[0] = _merge_heads_t(o[:, :t], o[:, t:]).astype(o_ref.dtype)


def sb_attention(qk, vt, t):
    B, S, _ = qk.shape
    nk = S // t
    ar = jnp.arange(min(SB_SUB, t))
    upper = -(ar[None, :] > ar[:, None]).astype(BF16)
    return pl.pallas_call(
        functools.partial(_sb_body, t=t),
        grid=(B, N_PAIRS, nk),
        in_specs=[pl.BlockSpec((1, t, LANES), lambda b, p, i: (b, i, p)),
                  pl.BlockSpec((1, S, LANES), lambda b, p, i: (b, 0, N_PAIRS + p)),
                  pl.BlockSpec((1, 1, nk, LANES, t), lambda b, p, i: (b, p, 0, 0, 0)),
                  pl.BlockSpec(upper.shape, lambda b, p, i: (0, 0))],
        out_specs=pl.BlockSpec((1, t, LANES), lambda b, p, i: (b, i, p)),
        out_shape=jax.ShapeDtypeStruct((B, S, D_MODEL), BF16),
        scratch_shapes=[pltpu.VMEM((1, 2 * t), F32), pltpu.VMEM((LANES, 2 * t), F32)],
        compiler_params=_cparams(("parallel", "parallel", "arbitrary")),
        name="sb_attention",
    )(qk, qk, vt, upper)


def _sortable_key(x):
    bits = pltpu.bitcast(x, I32)
    return bits ^ ((bits >> 31) & jnp.int32(0x7FFFFFFF))


def _dsa_body(q_ref, qi_ref, k_ref, ki_ref, vt_ref, wi_ref, lower_ref, o_ref,
              keys_sc, s_sc, m_sc, l_sc, acc_sc, *, tq, tk, topk):
    qb = pl.program_id(1)
    p = pl.program_id(2)
    nch = (qb * tq) // tk + 1
    kid = lax.broadcasted_iota(I32, (tk, tq), 0)
    qpos = qb * tq + lax.broadcasted_iota(I32, (tk, tq), 1)

    @pl.when(p == 0)
    def _select():
        def score_chunk(c, carry):
            start = pl.multiple_of(c * tk, tk)
            kidup = ki_ref[0, pl.ds(start, tk), :]
            score = jnp.zeros((tk, tq), F32)
            for hp in range(IDX_HEADS // 2):
                _, halves = _pair_halves(qi_ref[0, :, hp * LANES:(hp + 1) * LANES])
                for j in range(2):
                    h = 2 * hp + j
                    score = score + jnp.maximum(_dot_nt(kidup, halves[j]), 0.0) * wi_ref[0, h:h + 1, :]
            keys_sc[c] = jnp.where(c * tk + kid <= qpos, _sortable_key(score), jnp.int32(INT_MIN))
            return carry

        lax.fori_loop(0, nch, score_chunk, 0)

        def count_ge(thr):
            def body(c, cnts):
                kc = keys_sc[c]
                cnts = list(cnts)
                for j in range(tk // 8):
                    cnts[j % COUNT_CHAINS] = cnts[j % COUNT_CHAINS] + jnp.where(kc[j * 8:(j + 1) * 8, :] >= thr, 1, 0)
                return tuple(cnts)
            zero = jnp.zeros((8, tq), I32)
            cnts = lax.fori_loop(0, nch, body, (zero,) * COUNT_CHAINS)
            return jnp.sum(sum(cnts[1:], cnts[0]), axis=0, keepdims=True)

        def bit_step(i, thr):
            cand = thr + jnp.left_shift(jnp.int32(1), 31 - i)
            return jnp.where(count_ge(cand) >= topk, cand, thr)

        thr = lax.fori_loop(0, 32, bit_step, jnp.full((1, tq), INT_MIN, I32))
        n_ge = count_ge(thr)
        n_gt = count_ge(thr + 1)
        need = topk - n_gt
        tie = jnp.logical_and(n_ge - n_gt > need, thr > INT_MIN)
        any_tie = jnp.max(jnp.where(tie, 1, 0))

        def store_bias(c, sel):
            keys_sc[c] = pltpu.bitcast(jnp.where(sel, 0.0, -1e30).astype(F32), I32)

        @pl.when(any_tie == 0)
        def _no_ties():
            def body(c, carry):
                kc = keys_sc[c]
                store_bias(c, jnp.logical_and(kc >= thr, kc > INT_MIN))
                return carry
            lax.fori_loop(0, nch, body, 0)

        @pl.when(any_tie != 0)
        def _ties():
            needf = need.astype(F32)

            def body(c, seen):
                kc = keys_sc[c]
                eq = kc == thr
                eqf = jnp.where(eq, 1.0, 0.0)
                rank = _dot(lower_ref[...], eqf.astype(BF16)) + seen
                sel = jnp.logical_or(kc > thr, jnp.logical_and(eq, rank < needf))
                store_bias(c, jnp.logical_and(sel, kc > INT_MIN))
                return seen + jnp.sum(eqf, axis=0, keepdims=True)
            lax.fori_loop(0, nch, body, jnp.zeros((1, tq), F32))

    _, halves = _pair_halves(q_ref[0])
    q_both = jnp.concatenate(halves, axis=0)
    m_sc[...] = jnp.full(m_sc.shape, -1e30, F32)
    l_sc[...] = jnp.zeros(l_sc.shape, F32)
    acc_sc[...] = jnp.zeros(acc_sc.shape, F32)

    def scores(c):
        start = pl.multiple_of(c * tk, tk)
        return _dot_nt(k_ref[0, pl.ds(start, tk), :], q_both)

    def softmax_step(c, s_raw):
        bias = pltpu.bitcast(keys_sc[c], F32)
        s = s_raw + jnp.concatenate([bias, bias], axis=1)
        m_old = m_sc[...]
        m_new = jnp.maximum(m_old, jnp.max(s, axis=0, keepdims=True))
        alpha = jnp.exp2(m_old - m_new)
        pr = jnp.exp2(s - m_new)
        l_sc[...] = alpha * l_sc[...] + jnp.sum(pr, axis=0, keepdims=True)
        acc_sc[...] = alpha * acc_sc[...] + _dot(vt_ref[0, c], pr.astype(BF16))
        m_sc[...] = m_new

    s_sc[0] = scores(0)

    def attend(c, carry):
        s_raw = s_sc[c % 2]
        s_sc[(c + 1) % 2] = scores(c + 1)
        softmax_step(c, s_raw)
        return carry

    lax.fori_loop(0, nch - 1, attend, 0)
    softmax_step(nch - 1, s_sc[(nch - 1) % 2])
    o = acc_sc[...] / l_sc[...]
    o_ref[0] = _merge_heads_t(o[:, :tq], o[:, tq:]).astype(o_ref.dtype)


def dsa_attention(proj, vt, wit, tq, tk):
    B, S, _ = proj.shape
    topk = min(IDX_TOPK_MAX, S // 4)
    ar = jnp.arange(tk)
    lower = (ar[None, :] < ar[:, None]).astype(BF16)
    kern = functools.partial(_dsa_body, tq=tq, tk=tk, topk=topk)
    qi_blk = D_MODEL // (IDX_HEADS * IDX_DIM)
    k_blk = (D_MODEL + IDX_HEADS * IDX_DIM) // LANES
    nk = S // tk
    return pl.pallas_call(
        kern,
        grid=(B, S // tq, N_PAIRS),
        in_specs=[pl.BlockSpec((1, tq, LANES), lambda b, i, p: (b, i, p)),
                  pl.BlockSpec((1, tq, IDX_HEADS * IDX_DIM), lambda b, i, p: (b, i, qi_blk)),
                  pl.BlockSpec((1, S, LANES), lambda b, i, p: (b, 0, k_blk)),
                  pl.BlockSpec((1, S, LANES), lambda b, i, p: (b, 0, k_blk + 1)),
                  pl.BlockSpec((1, nk, LANES, tk), lambda b, i, p: (b, 0, 0, 0)),
                  pl.BlockSpec((1, LANES, tq), lambda b, i, p: (b, 0, i)),
                  pl.BlockSpec((tk, tk), lambda b, i, p: (0, 0))],
        out_specs=pl.BlockSpec((1, tq, LANES), lambda b, i, p: (b, i, p)),
        out_shape=jax.ShapeDtypeStruct((B, S, D_MODEL), BF16),
        scratch_shapes=[pltpu.VMEM((nk, tk, tq), I32), pltpu.VMEM((2, tk, 2 * tq), F32),
                        pltpu.VMEM((1, 2 * tq), F32), pltpu.VMEM((1, 2 * tq), F32),
                        pltpu.VMEM((LANES, 2 * tq), F32)],
        compiler_params=_cparams(("parallel", "arbitrary", "arbitrary")),
        name="dsa_attention",
    )(proj, proj, proj, proj, vt, wit, lower)


def _gla_body(qkvg_ref, a1_ref, wa2_ref, ba_ref, ng_ref, o_ref, state_sc, *, ts):
    C, H, dk, dv = GLA_CHUNK, GLA_HEADS, GLA_DK, GLA_DV

    @pl.when(pl.program_id(1) == 0)
    def _reset():
        state_sc[...] = jnp.zeros(state_sc.shape, F32)

    row = lax.broadcasted_iota(I32, (C, C), 0)
    colc = lax.broadcasted_iota(I32, (C, C), 1)
    causal = colc <= row
    tril = jnp.where(causal, 1.0, 0.0).astype(BF16)
    wa2 = wa2_ref[...].astype(BF16)
    ng = ng_ref[...]
    v_off, g_off = 2 * H * dk, 2 * H * dk + H * dv
    for n in range(ts // C):
        rows = slice(n * C, (n + 1) * C)
        za = _dot(a1_ref[0, rows, :].astype(BF16), wa2) + ba_ref[...]
        log_a = _log_sigmoid(za) / GLA_TAU
        h1, h2, h3 = _split3(log_a)
        bcum = _dot(tril, h1) + _dot(tril, h2) + _dot(tril, h3)
        eb = jnp.exp(bcum)
        enb = jnp.exp(-bcum)
        b_last = bcum[C - 1:C, :]
        eout = jnp.exp(b_last - bcum)
        dec = jnp.exp(b_last)
        for h in range(H):
            ks = slice(h * dk, (h + 1) * dk)
            q = qkvg_ref[0, rows, h * dk:(h + 1) * dk] * (dk ** -0.5)
            k = qkvg_ref[0, rows, H * dk + h * dk:H * dk + (h + 1) * dk]
            v = qkvg_ref[0, rows, v_off + h * dv:v_off + (h + 1) * dv].astype(BF16)
            q_in = (q * eb[:, ks]).astype(BF16)
            k_in = (k * enb[:, ks]).astype(BF16)
            k_out = (k * eout[:, ks]).astype(BF16)
            att = jnp.where(causal, _dot_nt(q_in, k_in), 0.0)
            state = state_sc[h]
            o = _dot(att.astype(BF16), v) + _dot(q_in, state.astype(BF16))
            dec_col = jnp.transpose(jnp.broadcast_to(dec[:, ks], (dk, dk)))[:, 0:1]
            state_sc[h] = state * dec_col + _dot_tn(k_out, v)
            o = o * lax.rsqrt(jnp.mean(o * o, axis=-1, keepdims=True) + RMS_EPS) * ng
            g = qkvg_ref[0, rows, g_off + h * dv:g_off + (h + 1) * dv]
            o = o * (g * jax.nn.sigmoid(g))
            o_ref[0, rows, h * dv:(h + 1) * dv] = o.astype(o_ref.dtype)


def gla_attention(qkvg, a1, w_a2p, b_a, norm_g, ts=512):
    B, S, W = qkvg.shape
    ts = min(ts, S)
    HK = GLA_HEADS * GLA_DK
    kern = functools.partial(_gla_body, ts=ts)
    return pl.pallas_call(
        kern,
        grid=(B, S // ts),
        in_specs=[pl.BlockSpec((1, ts, W), lambda b, i: (b, i, 0)),
                  pl.BlockSpec((1, ts, LANES), lambda b, i: (b, i, 0)),
                  pl.BlockSpec((LANES, HK), lambda b, i: (0, 0)),
                  pl.BlockSpec((1, HK), lambda b, i: (0, 0)),
                  pl.BlockSpec((1, GLA_DV), lambda b, i: (0, 0))],
        out_specs=pl.BlockSpec((1, ts, GLA_HEADS * GLA_DV), lambda b, i: (b, i, 0)),
        out_shape=jax.ShapeDtypeStruct((B, S, GLA_HEADS * GLA_DV), BF16),
        scratch_shapes=[pltpu.VMEM((GLA_HEADS, GLA_DK, GLA_DV), F32)],
        compiler_params=_cparams(("parallel", "arbitrary")),
        name="gla_attention",
    )(qkvg, a1, w_a2p, b_a.reshape(1, HK), norm_g.reshape(1, GLA_DV))


def _first_max(vals, ids, big):
    m = jnp.max(vals, axis=0, keepdims=True)
    first = jnp.min(jnp.where(vals == m, ids, big), axis=0, keepdims=True)
    return m, first


def _router_body(x_ref, wr_ref, br_ref, lower_ref, idx_ref, w_ref, rank_ref, cnt_ref, run_sc, *, tm):
    E, G, GS = N_EXPERTS, N_GROUPS, GROUP_SIZE

    @pl.when(pl.program_id(0) == 0)
    def _reset():
        run_sc[...] = jnp.zeros(run_sc.shape, F32)

    x = x_ref[...]
    x1 = x.astype(BF16)
    x2 = (x - x1.astype(F32)).astype(BF16)
    wr = wr_ref[...]
    w1 = wr.astype(BF16)
    w2 = (wr - w1.astype(F32)).astype(BF16)
    logits = _dot_nt(w1, x1) + _dot_nt(w1, x2) + _dot_nt(w2, x1)
    scores = jax.nn.sigmoid(logits)
    sel = scores + br_ref[...]
    neg = -jnp.inf
    eid = lax.broadcasted_iota(I32, (E, tm), 0)
    lid = lax.broadcasted_iota(I32, (GS, tm), 0)

    grp = []
    for g in range(G):
        sg = sel[g * GS:(g + 1) * GS, :]
        m1, f1 = _first_max(sg, lid, GS)
        m2 = jnp.max(jnp.where(lid == f1, neg, sg), axis=0, keepdims=True)
        grp.append(m1 + m2)
    grp = jnp.concatenate(grp, axis=0)
    gmask = jnp.zeros((G, tm), jnp.bool_)
    for _ in range(TOPK_GROUPS):
        _, f = _first_max(grp, lid, G)
        hit = lid == f
        gmask = jnp.logical_or(gmask, hit)
        grp = jnp.where(hit, neg, grp)
    emask = jnp.concatenate(
        [jnp.broadcast_to(gmask[g:g + 1, :], (GS, tm)) for g in range(G)], axis=0)
    sel = jnp.where(emask, sel, neg)

    chosen = jnp.zeros((E, tm), jnp.bool_)
    ids, ws = [], []
    for _ in range(TOP_K):
        _, f = _first_max(sel, eid, E)
        hit = eid == f
        ids.append(f)
        ws.append(jnp.sum(jnp.where(hit, scores, 0.0), axis=0, keepdims=True))
        chosen = jnp.logical_or(chosen, hit)
        sel = jnp.where(hit, neg, sel)
    ids = jnp.concatenate(ids, axis=0)
    ws = jnp.concatenate(ws, axis=0)
    ws = ws / jnp.sum(ws, axis=0, keepdims=True) * ROUTED_SCALE

    cf = jnp.where(chosen, 1.0, 0.0)
    before = _dot(cf.astype(BF16), lower_ref[...]) + run_sc[...]
    ranks = [jnp.sum(jnp.where(eid == ids[k:k + 1, :], before, 0.0), axis=0, keepdims=True)
             for k in range(TOP_K)]
    run_sc[...] = run_sc[...] + jnp.sum(cf, axis=1, keepdims=True)
    idx_ref[...] = ids
    w_ref[...] = ws
    rank_ref[...] = jnp.concatenate(ranks, axis=0).astype(I32)
    cnt_ref[...] = jnp.broadcast_to(run_sc[...], cnt_ref.shape).astype(I32)


def moe_router(x, w_router, b_router, tm=512):
    T, D = x.shape
    tm = min(tm, T)
    ar = jnp.arange(tm)
    lower = (ar[:, None] < ar[None, :]).astype(BF16)
    out = pl.BlockSpec((TOP_K, tm), lambda i: (0, i))
    return pl.pallas_call(
        functools.partial(_router_body, tm=tm),
        grid=(T // tm,),
        in_specs=[pl.BlockSpec((tm, D), lambda i: (i, 0)),
                  pl.BlockSpec((N_EXPERTS, D), lambda i: (0, 0)),
                  pl.BlockSpec((N_EXPERTS, 1), lambda i: (0, 0)),
                  pl.BlockSpec((tm, tm), lambda i: (0, 0))],
        out_specs=[out, out, out, pl.BlockSpec((N_EXPERTS, LANES), lambda i: (0, 0))],
        out_shape=[jax.ShapeDtypeStruct((TOP_K, T), I32), jax.ShapeDtypeStruct((TOP_K, T), F32),
                   jax.ShapeDtypeStruct((TOP_K, T), I32),
                   jax.ShapeDtypeStruct((N_EXPERTS, LANES), I32)],
        scratch_shapes=[pltpu.VMEM((N_EXPERTS, 1), F32)],
        compiler_params=_cparams(("arbitrary",)),
        name="moe_router",
    )(x, w_router.T, b_router.reshape(N_EXPERTS, 1), lower)


HALF = D_MODEL // 2


def _pack_rows(x):
    lo = pltpu.bitcast(x[:, :HALF].astype(BF16).astype(F32), I32)
    hi = pltpu.bitcast(x[:, HALF:].astype(BF16).astype(F32), I32)
    return lax.shift_right_logical(lo, 16) | (hi & jnp.int32(-65536))


def _unpack_rows(w):
    return pltpu.bitcast(w << 16, F32), pltpu.bitcast(w & jnp.int32(-65536), F32)


def _dispatch_body(x_ref, dest_hbm, xs_in, xs_hbm, idx_sm, pack_sc, sem_idx, sem_row, *, tm, nt):
    del xs_in
    i = pl.program_id(0)
    slot = i % 2
    n_rows = tm * TOP_K

    def idx_copy(tile, s):
        return pltpu.make_async_copy(dest_hbm.at[tile], idx_sm.at[s], sem_idx.at[s])

    def row_copy(s, t, row):
        return pltpu.make_async_copy(pack_sc.at[s, pl.ds(t, 1)], xs_hbm.at[pl.ds(row, 1)], sem_row.at[s])

    @pl.when(i == 0)
    def _first():
        idx_copy(0, 0).start()

    @pl.when(i + 1 < nt)
    def _prefetch():
        idx_copy(i + 1, 1 - slot).start()

    idx_copy(i, slot).wait()
    pack_sc[slot] = _pack_rows(x_ref[...])

    for t in range(tm):
        for k in range(TOP_K):
            row_copy(slot, t, idx_sm[slot, t * TOP_K + k]).start(priority=k % 2)

    def drain(s):
        pltpu.make_async_copy(xs_hbm.at[pl.ds(0, n_rows)], xs_hbm.at[pl.ds(n_rows, n_rows)], sem_row.at[s]).wait()

    @pl.when(i > 0)
    def _drain_prev():
        drain(1 - slot)

    @pl.when(i == nt - 1)
    def _drain_last():
        drain(slot)


def moe_dispatch(x, dest_tiles, xs_init, tm):
    T, D = x.shape
    nt = T // tm
    n_rows_out = xs_init.shape[0]
    xs0 = xs_init
    return pl.pallas_call(
        functools.partial(_dispatch_body, tm=tm, nt=nt),
        grid=(nt,),
        in_specs=[pl.BlockSpec((tm, D), lambda i: (i, 0)),
                  pl.BlockSpec(memory_space=pl.ANY),
                  pl.BlockSpec(memory_space=pl.ANY)],
        out_specs=pl.BlockSpec(memory_space=pl.ANY),
        out_shape=jax.ShapeDtypeStruct((n_rows_out, HALF), I32),
        scratch_shapes=[pltpu.SMEM((2, tm * TOP_K), I32), pltpu.VMEM((2, tm, HALF), I32),
                        pltpu.SemaphoreType.DMA((2,)), pltpu.SemaphoreType.DMA((2,))],
        input_output_aliases={2: 0},
        compiler_params=_cparams(("arbitrary",)),
        name="moe_dispatch",
    )(x, dest_tiles, xs0)


def _expert_body(be_ref, nreal_ref, x_ref, wgu_ref, wd_ref, o_ref):
    @pl.when(pl.program_id(0) < nreal_ref[0])
    def _():
        lo, hi = _unpack_rows(x_ref[...])
        gu = _dot(lo.astype(BF16), wgu_ref[0, :HALF, :]) + _dot(hi.astype(BF16), wgu_ref[0, HALF:, :])
        gt, up = gu[:, :D_EXPERT], gu[:, D_EXPERT:]
        h = gt * jax.nn.sigmoid(gt) * up
        o_ref[...] = _pack_rows(_dot(h.astype(BF16), wd_ref[0]))

    @pl.when(pl.program_id(0) >= nreal_ref[0])
    def _():
        o_ref[...] = jnp.zeros(o_ref.shape, I32)


def moe_experts(xs, block_exp, n_real, w_gu, w_down):
    P = xs.shape[0]
    D, M = D_MODEL, EXPERT_BLOCK
    grid_spec = pltpu.PrefetchScalarGridSpec(
        num_scalar_prefetch=2,
        grid=(P // M,),
        in_specs=[pl.BlockSpec((M, HALF), lambda i, be, nr: (i, 0)),
                  pl.BlockSpec((1, D, 2 * D_EXPERT), lambda i, be, nr: (be[i], 0, 0)),
                  pl.BlockSpec((1, D_EXPERT, D), lambda i, be, nr: (be[i], 0, 0))],
        out_specs=pl.BlockSpec((M, HALF), lambda i, be, nr: (i, 0)),
    )
    return pl.pallas_call(
        _expert_body,
        grid_spec=grid_spec,
        out_shape=jax.ShapeDtypeStruct((P, HALF), I32),
        compiler_params=_cparams(("arbitrary",)),
        name="moe_experts",
    )(block_exp, n_real, xs, w_gu, w_down)


def _ffn_tail_body(x_ref, dest_hbm, ys_hbm, w_ref, p_ref, wsg_ref, wsd_ref, g_ref, b_ref, wg_ref, bg_ref, wp_ref,
                   o_ref, idx_sm, y_sc, sem_idx, sem_row, *, tm, nt):
    i = pl.program_id(0)
    slot = i % 2
    n_rows = tm * TOP_K

    def idx_copy(tile, s):
        return pltpu.make_async_copy(dest_hbm.at[tile], idx_sm.at[s], sem_idx.at[s])

    def row_copy(s, k, t, row):
        return pltpu.make_async_copy(ys_hbm.at[pl.ds(row, 1)], y_sc.at[s, pl.ds(k * tm + t, 1)], sem_row.at[s])

    def gather(s):
        for t in range(tm):
            for k in range(TOP_K):
                row_copy(s, k, t, idx_sm[s, t * TOP_K + k]).start(priority=k % 2)

    @pl.when(i == 0)
    def _first():
        idx_copy(0, 0).start()
        idx_copy(0, 0).wait()
        gather(0)

        @pl.when(nt > 1)
        def _():
            idx_copy(1, 1).start()

    @pl.when(i + 1 < nt)
    def _next():
        idx_copy(i + 1, 1 - slot).wait()
        gather(1 - slot)

        @pl.when(i + 2 < nt)
        def _():
            idx_copy(i + 2, slot).start()

    x = x_ref[...]
    gu = _dot(x.astype(BF16), wsg_ref[...].astype(BF16))
    gs, us = gu[:, :D_SHARED], gu[:, D_SHARED:]
    shared = _dot((gs * jax.nn.sigmoid(gs) * us).astype(BF16), wsd_ref[...].astype(BF16))
    pe = _dot(p_ref[...].astype(BF16), wp_ref[...].astype(BF16))

    pltpu.make_async_copy(ys_hbm.at[pl.ds(0, n_rows)], y_sc.at[slot], sem_row.at[slot]).wait()

    r_lo = jnp.zeros((tm, HALF), F32)
    r_hi = jnp.zeros((tm, HALF), F32)
    for k in range(TOP_K):
        lo, hi = _unpack_rows(y_sc[slot, k * tm:(k + 1) * tm, :])
        wk = w_ref[:, k:k + 1]
        r_lo = r_lo + lo * wk
        r_hi = r_hi + hi * wk
    routed = jnp.concatenate([r_lo, r_hi], axis=1)
    x2 = _layer_norm(DEEPNORM_ALPHA * x + (routed + shared), g_ref[...], b_ref[...])
    gate = jax.nn.sigmoid(_dot(x2.astype(BF16), wg_ref[...].astype(BF16)) + bg_ref[...])
    o_ref[...] = x2 + gate * pe


def ffn_tail(x, dest_tiles, ys, wt, p, ws_gu, ws_down, g, b, w_gate, b_gate, w_proj, tm):
    T, D = x.shape
    nt = T // tm
    full = lambda shape: pl.BlockSpec(shape, lambda i: tuple(0 for _ in shape))
    return pl.pallas_call(
        functools.partial(_ffn_tail_body, tm=tm, nt=nt),
        grid=(nt,),
        in_specs=[pl.BlockSpec((tm, D), lambda i: (i, 0)),
                  pl.BlockSpec(memory_space=pl.ANY),
                  pl.BlockSpec(memory_space=pl.ANY),
                  pl.BlockSpec((tm, TOP_K), lambda i: (i, 0)),
                  pl.BlockSpec((tm, PLE_DIM), lambda i: (i, 0)),
                  full((D, 2 * D_SHARED)), full((D_SHARED, D)),
                  full((1, D)), full((1, D)), full((D, D)), full((1, D)), full((PLE_DIM, D))],
        out_specs=pl.BlockSpec((tm, D), lambda i: (i, 0)),
        out_shape=jax.ShapeDtypeStruct((T, D), F32),
        scratch_shapes=[pltpu.SMEM((2, tm * TOP_K), I32), pltpu.VMEM((2, TOP_K * tm, HALF), I32),
                        pltpu.SemaphoreType.DMA((2,)), pltpu.SemaphoreType.DMA((2,))],
        compiler_params=_cparams(("arbitrary",)),
        name="ffn_tail",
    )(x, dest_tiles, ys, wt, p, ws_gu, ws_down, g.reshape(1, D), b.reshape(1, D), w_gate,
      b_gate.reshape(1, D), w_proj)


def _pad_cols(w, n):
    return jnp.pad(w, ((0, 0), (0, n - w.shape[1])))


def _qk_weights(wq, wk, q_scale=HEAD_DIM ** -0.5):
    return jnp.concatenate([wq * q_scale, wk], axis=1)


def _fox_layer(x, w_in, b_f, t):
    B, S, D = x.shape
    xt = x.reshape(B * S, D)
    qk = matmul(xt, _qk_weights(w_in[:, :D], w_in[:, D:2 * D], LOG2E * HEAD_DIM ** -0.5), BF16).reshape(B, S, 2 * D)
    vt = matmul_t_blocks(xt, w_in[:, 2 * D:3 * D].T, B, t, BF16)
    f = matmul(xt, _pad_cols(w_in[:, 3 * D:], LANES), F32).reshape(B, S, LANES)
    caug, cbs = fox_gates(f, b_f, t)
    cb = cbs[:, :, 0, :N_HEADS].transpose(0, 2, 1).reshape(-1)
    return fox_attention(qk, vt, caug, cb, t).reshape(B * S, D)


def _sb_layer(x, w_in, t):
    B, S, D = x.shape
    xt = x.reshape(B * S, D)
    qk = matmul(xt, _qk_weights(w_in[:, :D], w_in[:, D:2 * D], LOG2E * HEAD_DIM ** -0.5), BF16).reshape(B, S, 2 * D)
    vt = matmul_t_blocks(xt, w_in[:, 2 * D:].T, B, t, BF16)
    return sb_attention(qk, vt, t).reshape(B * S, D)


def _rope_tables(positions):
    half = ROPE_DIM // 2
    inv_freq = jnp.exp(-math.log(ROPE_THETA) * 2.0 * jnp.arange(half, dtype=F32) / ROPE_DIM)
    ang = positions.astype(F32).reshape(-1, 1) * inv_freq
    cos, sin = jnp.cos(ang), jnp.sin(ang)
    T = ang.shape[0]
    rest = HEAD_DIM - ROPE_DIM
    cos64 = jnp.concatenate([cos, cos, jnp.ones((T, rest), F32)], axis=1)
    sa64 = jnp.concatenate([-sin, jnp.zeros((T, HEAD_DIM - half), F32)], axis=1)
    sb64 = jnp.concatenate([jnp.zeros((T, half), F32), sin, jnp.zeros((T, rest), F32)], axis=1)
    return tuple(jnp.tile(a, (1, LANES // HEAD_DIM)) for a in (cos64, sa64, sb64))


def _dsa_layer(x, positions, w_in, tq, tk):
    B, S, D = x.shape
    xt = x.reshape(B * S, D)
    dh = HEAD_DIM
    o_k, o_v, o_qi = D, D + dh, D + 2 * dh
    o_ki = o_qi + IDX_HEADS * IDX_DIM
    o_wi = o_ki + IDX_DIM
    wq, wk, wv = w_in[:, :o_k], w_in[:, o_k:o_v], w_in[:, o_v:o_qi]
    wqi, wki, wwi = w_in[:, o_qi:o_ki], w_in[:, o_ki:o_wi], w_in[:, o_wi:]
    w_rope = jnp.concatenate([wq * (LOG2E * dh ** -0.5), wqi, wk, wk, wki, wki], axis=1)
    cos, sa, sb = _rope_tables(positions)
    proj = matmul_rope(xt, w_rope, cos, sa, sb, BF16).reshape(B, S, -1)
    vt = matmul_t_blocks(xt, jnp.concatenate([wv, wv], axis=1).T, B, tk, BF16).reshape(B, S // tk, LANES, tk)
    wit = matmul_t_rows(xt, _pad_cols(wwi * (IDX_HEADS ** -0.5 * IDX_DIM ** -0.5), LANES).T, B, F32)
    return dsa_attention(proj, vt, wit, tq, tk).reshape(B * S, D)


def _gla_layer(x, w_in, w_a2, b_a, norm_g):
    B, S, D = x.shape
    xt = x.reshape(B * S, D)
    W = 2 * GLA_HEADS * GLA_DK + 2 * GLA_HEADS * GLA_DV
    qkvg = matmul(xt, w_in[:, :W], F32).reshape(B, S, W)
    a1 = matmul(xt, _pad_cols(w_in[:, W:], LANES), F32).reshape(B, S, LANES)
    w_a2p = jnp.pad(w_a2, ((0, LANES - GLA_GATE_RANK), (0, 0)))
    return gla_attention(qkvg, a1, w_a2p, b_a, norm_g).reshape(B * S, GLA_HEADS * GLA_DV)


MOE_TOKEN_TILE = 256


def _moe_layer(x, p_i, w_router, b_router, w_gu, w_down, ws_gu, ws_down, g, b, w_gate, b_gate, w_proj, xs_buf=None):
    T, D = x.shape
    E, M = N_EXPERTS, EXPERT_BLOCK
    tm = min(MOE_TOKEN_TILE, T)
    idx, wts, rank, cnt = moe_router(x, w_router, b_router)
    counts = cnt[:, 0]
    padded = (counts + M - 1) // M * M
    pend = jnp.cumsum(padded)
    poff = pend - padded
    eids = jnp.arange(E, dtype=I32)[:, None, None]
    dest = rank + jnp.sum(jnp.where(idx[None] == eids, poff.astype(I32)[:, None, None], 0), axis=0)
    dest_tiles = dest.T.reshape(T // tm, tm * TOP_K)
    P = (T * TOP_K + M - 1) // M * M + E * M
    nb = P // M
    starts = jnp.arange(nb, dtype=I32) * M
    block_exp = jnp.minimum(jnp.sum((pend.astype(I32)[None, :] <= starts[:, None]).astype(I32), axis=1), E - 1)
    n_real = (pend[-1] // M).astype(I32).reshape(1)
    if xs_buf is None:
        xs_buf = jnp.zeros((P, HALF), I32)
    xs = moe_dispatch(x, dest_tiles, xs_buf, tm)
    ys = moe_experts(xs, block_exp, n_real, w_gu, w_down)
    return ffn_tail(x, dest_tiles, ys, wts.T, p_i, ws_gu, ws_down, g, b, w_gate, b_gate, w_proj, tm), xs


def kernel(x, p, positions, ln1_g, ln1_b, ln2_g, ln2_b, fox_w_in, fox_b_f, fox_w_out, sb_w_in, sb_w_out, dsa_w_in, dsa_w_out, gla_w_in, gla_w_a2, gla_b_a, gla_norm_g, gla_w_out, moe_w_router, moe_b_router, moe_w_gu, moe_w_down, shared_w_gu, shared_w_down, ple_w_proj, ple_w_gate, ple_b_gate):
    B, S, D = x.shape
    T = B * S
    depth = p.shape[0]
    t_attn = min(512, S)
    w_gu_bf = moe_w_gu.astype(BF16)
    w_down_bf = moe_w_down.astype(BF16)
    xt = x.reshape(T, D)
    xs_buf = None
    for i in range(depth):
        m, j = i % 4, i // 4
        xb = xt.reshape(B, S, D)
        if m == 0:
            a, w_out = _fox_layer(xb, fox_w_in[j], fox_b_f[j], t_attn), fox_w_out[j]
        elif m == 1:
            a, w_out = _sb_layer(xb, sb_w_in[j], t_attn), sb_w_out[j]
        elif m == 2:
            a, w_out = _dsa_layer(xb, positions, dsa_w_in[j], min(256, S), min(512, S)), dsa_w_out[j]
        else:
            a, w_out = _gla_layer(xb, gla_w_in[j], gla_w_a2[j], gla_b_a[j], gla_norm_g[j]), gla_w_out[j]
        xt = matmul_residual_ln(a, w_out, xt, ln1_g[i], ln1_b[i])
        xt, xs_buf = _moe_layer(xt, p[i].reshape(T, -1), moe_w_router[i], moe_b_router[i], w_gu_bf[i], w_down_bf[i],
                                shared_w_gu[i], shared_w_down[i], ln2_g[i], ln2_b[i],
                                ple_w_gate[i], ple_b_gate[i], ple_w_proj[i], xs_buf)
    return xt.reshape(B, S, D)
```
